```python
import jax, jax.numpy as jnp
from jax import lax
import numpy as np

D_MODEL = 1024
BATCH = 2
SEQ = 8192
DEPTH = 2
DEC_BATCH = 32
DEC_SEQ = 4
PAST_LEN = 8192
PAGE_SIZE = 128

HEAD_DIM = 64
NSA_WIDTH = D_MODEL // 2
RWKV_WIDTH = D_MODEL // 4
RET_WIDTH = D_MODEL - NSA_WIDTH - RWKV_WIDTH
D_MIX = NSA_WIDTH + RWKV_WIDTH + RET_WIDTH
NSA_HEADS = NSA_WIDTH // HEAD_DIM
NSA_KV_HEADS = 2
NSA_GROUP = NSA_HEADS // NSA_KV_HEADS
CMP_LEN = 32
CMP_STRIDE = 16
SEL_LEN = 64
SEL_TOP = 16
WINDOW = 512
Q_BLOCK = 128
RWKV_HEADS = RWKV_WIDTH // HEAD_DIM
RWKV_W_RANK = 64
RWKV_A_RANK = 64
RWKV_G_RANK = 128
RWKV_GN_EPS = 64e-5
RET_HEADS = RET_WIDTH // HEAD_DIM
RET_CHUNK = 128
ROPE_BASE = 10000.0
D_FF = 2816
N_EXPERTS = 8
TOP_K = 2
D_EXPERT = 3584
N_DENSE = (DEPTH + 1) // 2
N_MOE = DEPTH // 2
NSA_KV_COLS = NSA_KV_HEADS * HEAD_DIM
NSA_IN = NSA_WIDTH + 6 * NSA_KV_COLS + 3 * NSA_HEADS
RWKV_IN = 3 * RWKV_WIDTH + RWKV_W_RANK + RWKV_A_RANK + RWKV_G_RANK
RET_IN = 4 * RET_WIDTH
N_IN = NSA_IN + RWKV_IN + RET_IN
EPS = 1e-6
GN_EPS = 1e-5
NEG = -1e30
TINY = 1e-30
FORCE = 1e9

kernel_name = 'hymba_nsa_rwkv7_retention_moe_step'


def rms_norm(x, g):
    xf = x.astype(jnp.float32)
    y = xf * lax.rsqrt(jnp.mean(xf * xf, axis=-1, keepdims=True) + EPS)
    return (y * g.astype(jnp.float32)).astype(x.dtype)


def group_norm(x, g, eps):
    xf = x.astype(jnp.float32)
    mu = jnp.mean(xf, axis=-1, keepdims=True)
    var = jnp.mean(jnp.square(xf - mu), axis=-1, keepdims=True)
    return (xf - mu) * lax.rsqrt(var + eps) * g.astype(jnp.float32)


def masked_softmax(s, mask):
    s = jnp.where(mask, s, NEG)
    m = jnp.max(s, axis=-1, keepdims=True)
    e = jnp.where(mask, jnp.exp(s - m), 0.0)
    return e / jnp.maximum(jnp.sum(e, axis=-1, keepdims=True), TINY)


def alibi_slopes(n):
    return 2.0 ** (-8.0 * jnp.arange(1, n + 1, dtype=jnp.float32) / n)


def rotary(x, pos):
    half = x.shape[-1] // 2
    freqs = ROPE_BASE ** (-jnp.arange(half, dtype=jnp.float32) / half)
    ang = pos.astype(jnp.float32)[:, None] * freqs[None, :]
    cos, sin = jnp.cos(ang)[None, :, None, :], jnp.sin(ang)[None, :, None, :]
    x1, x2 = x[..., :half], x[..., half:]
    return jnp.concatenate([x1 * cos - x2 * sin, x1 * sin + x2 * cos], axis=-1)


def nsa_compress(rows, pe, w1, w2):
    B, T, G, d = rows.shape
    n_cmp = (T - CMP_LEN) // CMP_STRIDE + 1
    idx = (jnp.arange(n_cmp, dtype=jnp.int32) * CMP_STRIDE)[:, None] + jnp.arange(CMP_LEN, dtype=jnp.int32)[None, :]
    blk = rows[:, idx] + pe[None, None, :, None, :]
    blk = jnp.transpose(blk, (0, 1, 3, 2, 4)).reshape(B, n_cmp, G, CMP_LEN * d)
    return jax.nn.gelu(blk @ w1) @ w2


def to_sel_blocks(rows):
    B, T, G, d = rows.shape
    n_sel = -(-T // SEL_LEN)
    rows = jnp.pad(rows, ((0, 0), (0, n_sel * SEL_LEN - T), (0, 0), (0, 0)))
    return jnp.transpose(rows.reshape(B, n_sel, SEL_LEN, G, d), (0, 3, 1, 2, 4))


def nsa_branches(q, pos, kc, vc, ks_blk, vs_blk, kw, vw, pos_w, slopes):
    B, Tq, H, d = q.shape
    G, R = NSA_KV_HEADS, NSA_GROUP
    scale = d ** -0.5
    qg = q.reshape(B, Tq, G, R, d)
    sl = slopes.reshape(G, R)
    posf = pos.astype(jnp.float32)
    n_cmp = kc.shape[1]
    c_start = jnp.arange(n_cmp, dtype=jnp.int32) * CMP_STRIDE
    d_cmp = posf[:, None] - (c_start + CMP_LEN - 1).astype(jnp.float32)[None, :]
    s = jnp.einsum('btgrd,bngd->bgrtn', qg, kc).astype(jnp.float32) * scale - sl[None, :, :, None, None] * d_cmp
    p_cmp = masked_softmax(s, (d_cmp >= 0.0)[None, None, None])
    o_cmp = jnp.einsum('bgrtn,bngd->btgrd', p_cmp.astype(vc.dtype), vc)
    n_sel = ks_blk.shape[2]
    s_start = jnp.arange(n_sel, dtype=jnp.int32) * SEL_LEN
    cover = jnp.maximum(jnp.minimum(c_start[:, None] + CMP_LEN, s_start[None, :] + SEL_LEN)
                        - jnp.maximum(c_start[:, None], s_start[None, :]), 0).astype(jnp.float32) / CMP_LEN
    imp = jnp.einsum('bgrtn,nj->bgtj', p_cmp, cover)
    blk = jnp.arange(n_sel, dtype=jnp.int32)
    avail = s_start[None, :] <= pos[:, None]
    forced = (blk[None, :] == (pos // SEL_LEN)[:, None]) | (blk[None, :] == 0)
    imp = jnp.where(avail, jnp.where(forced, FORCE, imp), -FORCE)
    n_top = min(SEL_TOP, n_sel)
    _, idx = lax.top_k(imp, n_top)
    b_i = jnp.arange(B)[:, None, None, None]
    g_i = jnp.arange(G)[None, :, None, None]
    ks_g = ks_blk[b_i, g_i, idx]
    vs_g = vs_blk[b_i, g_i, idx]
    kpos = idx[..., None] * SEL_LEN + jnp.arange(SEL_LEN, dtype=jnp.int32)
    d_sel = (pos[None, None, :, None, None] - kpos).astype(jnp.float32)[:, :, None]
    s = jnp.einsum('btgrd,bgtnsd->bgrtns', qg, ks_g).astype(jnp.float32) * scale - sl[None, :, :, None, None, None] * d_sel
    mask = jnp.broadcast_to(d_sel >= 0.0, s.shape)
    p_sel = masked_softmax(s.reshape(B, G, R, Tq, -1), mask.reshape(B, G, R, Tq, -1)).reshape(s.shape)
    o_sel = jnp.einsum('bgrtns,bgtnsd->btgrd', p_sel.astype(vs_g.dtype), vs_g)
    d_win = posf[:, None] - pos_w.astype(jnp.float32)[None, :]
    s = jnp.einsum('btgrd,bwgd->bgrtw', qg, kw).astype(jnp.float32) * scale - sl[None, :, :, None, None] * d_win
    win_mask = (d_win >= 0.0) & (d_win < WINDOW) & (pos_w >= 0)[None, :]
    p_win = masked_softmax(s, win_mask[None, None, None])
    o_win = jnp.einsum('bgrtw,bwgd->btgrd', p_win.astype(vw.dtype), vw)
    return jnp.stack([o_cmp, o_sel, o_win]).reshape(3, B, Tq, H, d)


def nsa_prompt(q, kc, vc, ks_blk, vs_blk, kw, vw, slopes):
    B, T, H, d = q.shape
    n_blk = T // Q_BLOCK
    span = WINDOW + Q_BLOCK
    pad = ((0, 0), (WINDOW, 0), (0, 0), (0, 0))
    kw_p, vw_p = jnp.pad(kw, pad), jnp.pad(vw, pad)

    def one_block(i):
        q0 = i * Q_BLOCK
        pos = q0 + jnp.arange(Q_BLOCK, dtype=jnp.int32)
        pos_w = q0 - WINDOW + jnp.arange(span, dtype=jnp.int32)
        return nsa_branches(lax.dynamic_slice_in_dim(q, q0, Q_BLOCK, 1), pos, kc, vc, ks_blk, vs_blk,
                            lax.dynamic_slice_in_dim(kw_p, q0, span, 1),
                            lax.dynamic_slice_in_dim(vw_p, q0, span, 1), pos_w, slopes)

    o = lax.map(one_block, jnp.arange(n_blk, dtype=jnp.int32))
    return jnp.moveaxis(o, 0, 2).reshape(3, B, T, H, d)


def rwkv_group(u, shift0, S0, lp):
    B, T, _ = u.shape
    W = RWKV_WIDTH
    uf = u.astype(jnp.float32)
    prev = jnp.concatenate([shift0.astype(jnp.float32)[:, None], uf[:, :-1]], axis=1)
    um = uf + (prev - uf) * lp['rwkv_mu']
    r, k, v = um[..., :W], um[..., W:2 * W], um[..., 2 * W:3 * W]
    o = 3 * W
    wd = um[..., o:o + RWKV_W_RANK]
    ad = um[..., o + RWKV_W_RANK:o + RWKV_W_RANK + RWKV_A_RANK]
    gd = um[..., o + RWKV_W_RANK + RWKV_A_RANK:]
    w = lp['rwkv_w0'] + jnp.tanh(wd) @ lp['rwkv_w_up']
    decay = jnp.exp(-jnp.exp(-jax.nn.softplus(-w) - 0.5))
    a = jax.nn.sigmoid(lp['rwkv_a0'] + ad @ lp['rwkv_a_up'])
    g = jax.nn.sigmoid(gd) @ lp['rwkv_g_up']
    kk = k * lp['rwkv_k_k']
    k = k * (1.0 + (a - 1.0) * lp['rwkv_k_a'])
    hd = lambda t: t.reshape(B, T, RWKV_HEADS, HEAD_DIM).astype(jnp.float32)
    r, k, v, decay, a, g, kk = hd(r), hd(k), hd(v), hd(decay), hd(a), hd(g), hd(kk)
    kk = kk / jnp.maximum(jnp.sqrt(jnp.sum(kk * kk, axis=-1, keepdims=True)), 1e-12)

    def step(S, inp):
        r_t, w_t, k_t, v_t, kk_t, a_t = inp
        sa = jnp.einsum('bhvk,bhk->bhv', S, kk_t)
        S = S * w_t[:, :, None, :] - sa[..., None] * (kk_t * a_t)[:, :, None, :] + v_t[..., None] * k_t[:, :, None, :]
        return S, jnp.einsum('bhvk,bhk->bhv', S, r_t)

    xs = tuple(jnp.moveaxis(t, 1, 0) for t in (r, decay, k, v, kk, a))
    S_T, ys = lax.scan(step, S0.astype(jnp.float32), xs)
    y = group_norm(jnp.moveaxis(ys, 0, 1), lp['rwkv_ln_g'], RWKV_GN_EPS) + lp['rwkv_ln_b']
    y = y + jnp.sum(r * k * lp['rwkv_r_k'], axis=-1, keepdims=True) * v
    y = y * g
    return y.reshape(B, T, W).astype(u.dtype), u[:, -1], S_T


def retention_group(u, pos, S0, ln_g):
    B, T, _ = u.shape
    uf = u.astype(jnp.float32)
    q, k, v, g = [t.reshape(B, T, RET_HEADS, HEAD_DIM) for t in jnp.split(uf, 4, axis=-1)]
    q = rotary(q, pos)
    k = rotary(k, pos) * HEAD_DIM ** -0.5
    lg = jnp.log(1.0 - 2.0 ** (-5.0 - jnp.arange(RET_HEADS, dtype=jnp.float32)))
    C = RET_CHUNK if T % RET_CHUNK == 0 else T
    nC = T // C
    n = jnp.arange(C, dtype=jnp.float32)
    diff = n[:, None] - n[None, :]
    dmask = jnp.where(diff[None] >= 0, jnp.exp(jnp.maximum(diff, 0.0)[None] * lg[:, None, None]), 0.0)
    q_dec = jnp.exp((n[:, None] + 1.0) * lg[None, :])
    k_dec = jnp.exp((C - 1.0 - n)[:, None] * lg[None, :])
    s_dec = jnp.exp(C * lg)

    def chunk(S, inp):
        qc, kc, vc = inp
        att = jnp.einsum('bnhd,bmhd->bhnm', qc, kc) * dmask
        out = jnp.einsum('bhnm,bmhe->bnhe', att, vc) + jnp.einsum('bnhd,bhde->bnhe', qc, S) * q_dec[None, :, :, None]
        S = S * s_dec[None, :, None, None] + jnp.einsum('bmhd,bmhe->bhde', kc * k_dec[None, :, :, None], vc)
        return S, out

    to_chunks = lambda t: jnp.moveaxis(t.reshape(B, nC, C, RET_HEADS, HEAD_DIM), 1, 0)
    S_T, o = lax.scan(chunk, S0.astype(jnp.float32), (to_chunks(q), to_chunks(k), to_chunks(v)))
    o = jnp.moveaxis(o, 0, 1).reshape(B, T, RET_HEADS, HEAD_DIM)
    y = jax.nn.silu(g) * group_norm(o, ln_g, GN_EPS)
    return y.reshape(B, T, RET_WIDTH).astype(u.dtype), S_T


def mixing_sublayer(x, lp, past_len, past_cmp, past_sel, win_buf, rwkv_S, rwkv_shift, ret_S):
    B, T, _ = x.shape
    h = rms_norm(x, lp['norm_attn'])
    P = h @ lp['w_in']
    c = P[..., :NSA_IN]
    q = rms_norm(c[..., :NSA_WIDTH].reshape(B, T, NSA_HEADS, HEAD_DIM), lp['nsa_q_norm'])
    kv = c[..., NSA_WIDTH:NSA_WIDTH + 6 * NSA_KV_COLS].reshape(B, T, 3, 2, NSA_KV_HEADS, HEAD_DIM)
    kv_cmp, kv_sel, kv_win = kv[:, :, 0], kv[:, :, 1], kv[:, :, 2]
    gates = jax.nn.sigmoid(c[..., NSA_WIDTH + 6 * NSA_KV_COLS:].astype(jnp.float32)).reshape(B, T, NSA_HEADS, 3)
    pos = past_len + jnp.arange(T, dtype=jnp.int32)
    slopes = alibi_slopes(NSA_HEADS)
    prompt = past_cmp is None
    if prompt:
        rows_cmp, rows_sel, rows_win = kv_cmp, kv_sel, kv_win
        new_win = kv_win[:, T - min(WINDOW, T):]
    else:
        rows_cmp = jnp.concatenate([past_cmp, kv_cmp.astype(past_cmp.dtype)], axis=1)
        rows_sel = jnp.concatenate([past_sel, kv_sel.astype(past_sel.dtype)], axis=1)
        rows_win = jnp.concatenate([win_buf, kv_win.astype(win_buf.dtype)], axis=1)
        new_win = rows_win[:, T:]
    k_norm = lp['nsa_k_norm']
    kc = rms_norm(nsa_compress(rows_cmp[:, :, 0], lp['nsa_cmp_pe'][0], lp['nsa_cmp_w1'][0], lp['nsa_cmp_w2'][0]), k_norm[0])
    vc = nsa_compress(rows_cmp[:, :, 1], lp['nsa_cmp_pe'][1], lp['nsa_cmp_w1'][1], lp['nsa_cmp_w2'][1])
    ks_blk = to_sel_blocks(rms_norm(rows_sel[:, :, 0], k_norm[1]))
    vs_blk = to_sel_blocks(rows_sel[:, :, 1])
    kw = rms_norm(rows_win[:, :, 0], k_norm[2])
    vw = rows_win[:, :, 1]
    if prompt:
        o3 = nsa_prompt(q, kc, vc, ks_blk, vs_blk, kw, vw, slopes)
    else:
        wb = win_buf.shape[1]
        pos_w = past_len - wb + jnp.arange(wb + T, dtype=jnp.int32)
        o3 = nsa_branches(q, pos, kc, vc, ks_blk, vs_blk, kw, vw, pos_w, slopes)
    o_nsa = jnp.einsum('btha,abthd->bthd', gates.astype(o3.dtype), o3)
    o_nsa = rms_norm(o_nsa, lp['nsa_out_norm']).reshape(B, T, NSA_WIDTH)
    y_rwkv, new_shift, new_rwkv = rwkv_group(P[..., NSA_IN:NSA_IN + RWKV_IN], rwkv_shift, rwkv_S, lp)
    y_ret, new_ret = retention_group(P[..., NSA_IN + RWKV_IN:], pos, ret_S, lp['ret_ln_g'])
    mix = jnp.concatenate([o_nsa, y_rwkv.astype(o_nsa.dtype), y_ret.astype(o_nsa.dtype)], axis=-1)
    x = x + mix @ lp['w_out']
    return x, (kv_cmp, kv_sel, new_win, new_rwkv, new_shift, new_ret)


def swiglu(h, wg, wu, wd):
    return (jax.nn.silu(h @ wg) * (h @ wu)) @ wd


def moe_swiglu(h, router, wg, wu, wd):
    logits = (h @ router).astype(jnp.float32)
    top_val, top_idx = lax.top_k(logits, TOP_K)
    gate = jax.nn.softmax(top_val, axis=-1)
    combine = jnp.einsum('...k,...ke->...e', gate, jax.nn.one_hot(top_idx, N_EXPERTS, dtype=jnp.float32))
    y = jnp.zeros_like(h)
    for e in range(N_EXPERTS):
        y = y + combine[..., e:e + 1].astype(h.dtype) * swiglu(h, wg[e], wu[e], wd[e])
    return y


def setup_inputs(seed: int = 0) -> dict:
    key = jax.random.key(seed)
    ks = iter(jax.random.split(key, 48))

    def nrm(shape, scale):
        return scale * jax.random.normal(next(ks), shape, jnp.float32)

    n_pages = PAST_LEN // PAGE_SIZE
    n_used = DEC_BATCH * n_pages
    n_pool = n_used + (n_used + 3) // 4
    win_buf = min(WINDOW, PAST_LEN)
    inp = {}
    inp['x_prompt'] = nrm((BATCH, SEQ, D_MODEL), 1.0)
    inp['x_sample'] = nrm((DEC_BATCH, DEC_SEQ, D_MODEL), 1.0)
    inp['cache_nsa_cmp'] = nrm((DEPTH, n_pool, PAGE_SIZE, 2, NSA_KV_HEADS, HEAD_DIM), 1.0)
    inp['cache_nsa_sel'] = nrm((DEPTH, n_pool, PAGE_SIZE, 2, NSA_KV_HEADS, HEAD_DIM), 1.0)
    inp['cache_nsa_win'] = nrm((DEPTH, DEC_BATCH, win_buf, 2, NSA_KV_HEADS, HEAD_DIM), 1.0)
    inp['state_rwkv'] = nrm((DEPTH, DEC_BATCH, RWKV_HEADS, HEAD_DIM, HEAD_DIM), 0.3)
    inp['state_rwkv_shift'] = nrm((DEPTH, DEC_BATCH, RWKV_IN), 1.0)
    inp['state_ret'] = nrm((DEPTH, DEC_BATCH, RET_HEADS, HEAD_DIM, HEAD_DIM), 0.3)
    inp['page_table'] = jax.random.permutation(next(ks), n_pool)[:n_used].reshape(DEC_BATCH, n_pages).astype(jnp.int32)
    inp['norm_attn'] = 1.0 + nrm((DEPTH, D_MODEL), 0.02)
    inp['norm_ffn'] = 1.0 + nrm((DEPTH, D_MODEL), 0.02)
    inp['w_in'] = nrm((DEPTH, D_MODEL, N_IN), D_MODEL ** -0.5)
    inp['w_out'] = nrm((DEPTH, D_MIX, D_MODEL), D_MIX ** -0.5)
    inp['nsa_q_norm'] = 1.0 + nrm((DEPTH, HEAD_DIM), 0.02)
    inp['nsa_k_norm'] = 1.0 + nrm((DEPTH, 3, HEAD_DIM), 0.02)
    inp['nsa_cmp_pe'] = nrm((DEPTH, 2, CMP_LEN, HEAD_DIM), 0.1)
    inp['nsa_cmp_w1'] = nrm((DEPTH, 2, CMP_LEN * HEAD_DIM, HEAD_DIM), (CMP_LEN * HEAD_DIM) ** -0.5)
    inp['nsa_cmp_w2'] = nrm((DEPTH, 2, HEAD_DIM, HEAD_DIM), HEAD_DIM ** -0.5)
    inp['nsa_out_norm'] = 1.0 + nrm((DEPTH, NSA_HEADS, HEAD_DIM), 0.02)
    inp['rwkv_mu'] = jax.random.uniform(next(ks), (DEPTH, RWKV_IN), jnp.float32, 0.1, 0.9)
    inp['rwkv_w0'] = nrm((DEPTH, RWKV_WIDTH), 0.5)
    inp['rwkv_w_up'] = nrm((DEPTH, RWKV_W_RANK, RWKV_WIDTH), 0.1 * RWKV_W_RANK ** -0.5)
    inp['rwkv_a0'] = nrm((DEPTH, RWKV_WIDTH), 0.3)
    inp['rwkv_a_up'] = nrm((DEPTH, RWKV_A_RANK, RWKV_WIDTH), RWKV_A_RANK ** -0.5)
    inp['rwkv_g_up'] = nrm((DEPTH, RWKV_G_RANK, RWKV_WIDTH), RWKV_G_RANK ** -0.5)
    inp['rwkv_k_k'] = 0.85 + nrm((DEPTH, RWKV_WIDTH), 0.02)
    inp['rwkv_k_a'] = 1.0 + nrm((DEPTH, RWKV_WIDTH), 0.02)
    inp['rwkv_r_k'] = nrm((DEPTH, RWKV_HEADS, HEAD_DIM), 0.1)
    inp['rwkv_ln_g'] = 1.0 + nrm((DEPTH, RWKV_HEADS, HEAD_DIM), 0.02)
    inp['rwkv_ln_b'] = nrm((DEPTH, RWKV_HEADS, HEAD_DIM), 0.02)
    inp['ret_ln_g'] = 1.0 + nrm((DEPTH, RET_HEADS, HEAD_DIM), 0.02)
    inp['ffn_w_gate'] = nrm((N_DENSE, D_MODEL, D_FF), D_MODEL ** -0.5)
    inp['ffn_w_up'] = nrm((N_DENSE, D_MODEL, D_FF), D_MODEL ** -0.5)
    inp['ffn_w_down'] = nrm((N_DENSE, D_FF, D_MODEL), D_FF ** -0.5)
    inp['moe_router'] = nrm((N_MOE, D_MODEL, N_EXPERTS), D_MODEL ** -0.5)
    inp['moe_w_gate'] = nrm((N_MOE, N_EXPERTS, D_MODEL, D_EXPERT), D_MODEL ** -0.5)
    inp['moe_w_up'] = nrm((N_MOE, N_EXPERTS, D_MODEL, D_EXPERT), D_MODEL ** -0.5)
    inp['moe_w_down'] = nrm((N_MOE, N_EXPERTS, D_EXPERT, D_MODEL), D_EXPERT ** -0.5)
    return inp


def reference(x_prompt, x_sample, cache_nsa_cmp, cache_nsa_sel, cache_nsa_win, state_rwkv, state_rwkv_shift,
              state_ret, page_table, norm_attn, norm_ffn, w_in, w_out, nsa_q_norm, nsa_k_norm, nsa_cmp_pe,
              nsa_cmp_w1, nsa_cmp_w2, nsa_out_norm, rwkv_mu, rwkv_w0, rwkv_w_up, rwkv_a0, rwkv_a_up, rwkv_g_up,
              rwkv_k_k, rwkv_k_a, rwkv_r_k, rwkv_ln_g, rwkv_ln_b, ret_ln_g, ffn_w_gate, ffn_w_up, ffn_w_down,
              moe_router, moe_w_gate, moe_w_up, moe_w_down):
    Bp, Bs = x_prompt.shape[0], x_sample.shape[0]
    past_len = page_table.shape[1] * cache_nsa_cmp.shape[2]
    xp, xs = x_prompt, x_sample
    outs_p, outs_s = [], []
    for l in range(DEPTH):
        lp = {'norm_attn': norm_attn[l], 'w_in': w_in[l], 'w_out': w_out[l], 'nsa_q_norm': nsa_q_norm[l],
              'nsa_k_norm': nsa_k_norm[l], 'nsa_cmp_pe': nsa_cmp_pe[l], 'nsa_cmp_w1': nsa_cmp_w1[l],
              'nsa_cmp_w2': nsa_cmp_w2[l], 'nsa_out_norm': nsa_out_norm[l], 'rwkv_mu': rwkv_mu[l],
              'rwkv_w0': rwkv_w0[l], 'rwkv_w_up': rwkv_w_up[l], 'rwkv_a0': rwkv_a0[l], 'rwkv_a_up': rwkv_a_up[l],
              'rwkv_g_up': rwkv_g_up[l], 'rwkv_k_k': rwkv_k_k[l], 'rwkv_k_a': rwkv_k_a[l], 'rwkv_r_k': rwkv_r_k[l],
              'rwkv_ln_g': rwkv_ln_g[l], 'rwkv_ln_b': rwkv_ln_b[l], 'ret_ln_g': ret_ln_g[l]}
        xp, st = mixing_sublayer(xp, lp, 0, None, None, None,
                                 jnp.zeros((Bp, RWKV_HEADS, HEAD_DIM, HEAD_DIM), jnp.float32),
                                 jnp.zeros((Bp, RWKV_IN), xp.dtype),
                                 jnp.zeros((Bp, RET_HEADS, HEAD_DIM, HEAD_DIM), jnp.float32))
        outs_p.append(st)
        past_cmp = cache_nsa_cmp[l][page_table].reshape(Bs, past_len, 2, NSA_KV_HEADS, HEAD_DIM)
        past_sel = cache_nsa_sel[l][page_table].reshape(Bs, past_len, 2, NSA_KV_HEADS, HEAD_DIM)
        xs, st = mixing_sublayer(xs, lp, past_len, past_cmp, past_sel, cache_nsa_win[l], state_rwkv[l],
                                 state_rwkv_shift[l], state_ret[l])
        outs_s.append(st)
        hp, hs = rms_norm(xp, norm_ffn[l]), rms_norm(xs, norm_ffn[l])
        i = l // 2
        if l % 2 == 0:
            xp = xp + swiglu(hp, ffn_w_gate[i], ffn_w_up[i], ffn_w_down[i])
            xs = xs + swiglu(hs, ffn_w_gate[i], ffn_w_up[i], ffn_w_down[i])
        else:
            xp = xp + moe_swiglu(hp, moe_router[i], moe_w_gate[i], moe_w_up[i], moe_w_down[i])
            xs = xs + moe_swiglu(hs, moe_router[i], moe_w_gate[i], moe_w_up[i], moe_w_down[i])
    kv_cmp_p, kv_sel_p, win_p, rwkv_p, shift_p, ret_p = [jnp.stack([o[j] for o in outs_p]) for j in range(6)]
    kv_cmp_s, kv_sel_s, win_s, rwkv_s, shift_s, ret_s = [jnp.stack([o[j] for o in outs_s]) for j in range(6)]
    return (xp, xs, kv_cmp_p, kv_sel_p, win_p, rwkv_p, shift_p, ret_p,
            kv_cmp_s, kv_sel_s, win_s, rwkv_s, shift_s, ret_s)
```

```python
import functools

import jax
import jax.numpy as jnp
from jax import lax
from jax.experimental import pallas as pl
from jax.experimental.pallas import tpu as pltpu

D_MODEL = 1024
DEPTH = 2
HEAD_DIM = 64
NSA_WIDTH = D_MODEL // 2
RWKV_WIDTH = D_MODEL // 4
RET_WIDTH = D_MODEL - NSA_WIDTH - RWKV_WIDTH
NSA_HEADS = NSA_WIDTH // HEAD_DIM
NSA_KV_HEADS = 2
NSA_GROUP = NSA_HEADS // NSA_KV_HEADS
CMP_LEN = 32
CMP_STRIDE = 16
SEL_LEN = 64
SEL_TOP = 16
WINDOW = 512
Q_BLOCK = 128
RWKV_HEADS = RWKV_WIDTH // HEAD_DIM
RWKV_W_RANK = 64
RWKV_A_RANK = 64
RWKV_G_RANK = 128
RWKV_GN_EPS = 64e-5
RET_HEADS = RET_WIDTH // HEAD_DIM
RET_CHUNK = 128
ROPE_BASE = 10000.0
N_EXPERTS = 8
TOP_K = 2
NSA_KV_COLS = NSA_KV_HEADS * HEAD_DIM
NSA_IN = NSA_WIDTH + 6 * NSA_KV_COLS + 3 * NSA_HEADS
RWKV_IN = 3 * RWKV_WIDTH + RWKV_W_RANK + RWKV_A_RANK + RWKV_G_RANK
RET_IN = 4 * RET_WIDTH
EPS = 1e-6
GN_EPS = 1e-5
NEG = -1e30
TINY = 1e-30
FORCE = 1e9

VMEM_LIMIT_BYTES = 56 * 1024 * 1024


def _pick_tile(n, target):
    t = min(n, target)
    while n % t:
        t //= 2
    return t


def _ffn_body(x_ref, g_ref, s_ref, wg_ref, wu_ref, wd_ref, y_ref, h_scr, acc_scr):
    j = pl.program_id(1)

    @pl.when(j == 0)
    def _():
        x = x_ref[...]
        h = x * lax.rsqrt(jnp.mean(x * x, axis=-1, keepdims=True) + EPS) * g_ref[...]
        h_scr[...] = h.astype(jnp.bfloat16)
        acc_scr[...] = jnp.zeros_like(acc_scr)

    h = h_scr[...]
    a = jnp.dot(h, wg_ref[...].astype(jnp.bfloat16), preferred_element_type=jnp.float32)
    b = jnp.dot(h, wu_ref[...].astype(jnp.bfloat16), preferred_element_type=jnp.float32)
    z = (a * jax.nn.sigmoid(a)) * b
    acc_scr[...] += jnp.dot(z.astype(jnp.bfloat16), wd_ref[...].astype(jnp.bfloat16),
                            preferred_element_type=jnp.float32)

    @pl.when(j == pl.num_programs(1) - 1)
    def _():
        y_ref[...] = acc_scr[...] * s_ref[...]


def _ffn(x, g, scale, wg, wu, wd, *, tm=512, tf=256):
    M, D = x.shape
    F = wg.shape[1]
    tm = _pick_tile(M, tm)
    tf = _pick_tile(F, tf)
    return pl.pallas_call(
        _ffn_body,
        grid=(M // tm, F // tf),
        in_specs=[
            pl.BlockSpec((tm, D), lambda i, j: (i, 0)),
            pl.BlockSpec((1, D), lambda i, j: (0, 0)),
            pl.BlockSpec((tm, 1), lambda i, j: (i, 0)),
            pl.BlockSpec((D, tf), lambda i, j: (0, j)),
            pl.BlockSpec((D, tf), lambda i, j: (0, j)),
            pl.BlockSpec((tf, D), lambda i, j: (j, 0)),
        ],
        out_specs=pl.BlockSpec((tm, D), lambda i, j: (i, 0)),
        out_shape=jax.ShapeDtypeStruct((M, D), jnp.float32),
        scratch_shapes=[pltpu.VMEM((tm, D), jnp.bfloat16), pltpu.VMEM((tm, D), jnp.float32)],
        compiler_params=pltpu.CompilerParams(
            dimension_semantics=("parallel", "arbitrary"), vmem_limit_bytes=VMEM_LIMIT_BYTES),
        name="ffn",
    )(x, g.reshape(1, D), scale, wg, wu, wd)


def _rms_norm(x, g):
    xf = x.astype(jnp.float32)
    y = xf * lax.rsqrt(jnp.mean(xf * xf, axis=-1, keepdims=True) + EPS)
    return (y * g.astype(jnp.float32)).astype(x.dtype)


def _group_norm(x, g, eps):
    xf = x.astype(jnp.float32)
    mu = jnp.mean(xf, axis=-1, keepdims=True)
    var = jnp.mean(jnp.square(xf - mu), axis=-1, keepdims=True)
    return (xf - mu) * lax.rsqrt(var + eps) * g.astype(jnp.float32)


def _masked_softmax(s, mask):
    s = jnp.where(mask, s, NEG)
    m = jnp.max(s, axis=-1, keepdims=True)
    e = jnp.where(mask, jnp.exp(s - m), 0.0)
    return e / jnp.maximum(jnp.sum(e, axis=-1, keepdims=True), TINY)


def _alibi_slopes(n):
    return 2.0 ** (-8.0 * jnp.arange(1, n + 1, dtype=jnp.float32) / n)


def _rotary(x, pos):
    half = x.shape[-1] // 2
    freqs = ROPE_BASE ** (-jnp.arange(half, dtype=jnp.float32) / half)
    ang = pos.astype(jnp.float32)[:, None] * freqs[None, :]
    cos, sin = jnp.cos(ang)[None, :, None, :], jnp.sin(ang)[None, :, None, :]
    x1, x2 = x[..., :half], x[..., half:]
    return jnp.concatenate([x1 * cos - x2 * sin, x1 * sin + x2 * cos], axis=-1)


def _nsa_compress(rows, pe, w1, w2):
    B, T, G, d = rows.shape
    n_cmp = (T - CMP_LEN) // CMP_STRIDE + 1
    idx = (jnp.arange(n_cmp, dtype=jnp.int32) * CMP_STRIDE)[:, None] + jnp.arange(CMP_LEN, dtype=jnp.int32)[None, :]
    blk = rows[:, idx] + pe[None, None, :, None, :]
    blk = jnp.transpose(blk, (0, 1, 3, 2, 4)).reshape(B, n_cmp, G, CMP_LEN * d)
    return jax.nn.gelu(blk @ w1) @ w2


def _to_sel_blocks(rows):
    B, T, G, d = rows.shape
    n_sel = -(-T // SEL_LEN)
    rows = jnp.pad(rows, ((0, 0), (0, n_sel * SEL_LEN - T), (0, 0), (0, 0)))
    return jnp.transpose(rows.reshape(B, n_sel, SEL_LEN, G, d), (0, 3, 1, 2, 4))


def _nsa_branches(q, pos, kc, vc, ks_blk, vs_blk, kw, vw, pos_w, slopes):
    B, Tq, H, d = q.shape
    G, R = NSA_KV_HEADS, NSA_GROUP
    scale = d ** -0.5
    qg = q.reshape(B, Tq, G, R, d)
    sl = slopes.reshape(G, R)
    posf = pos.astype(jnp.float32)
    n_cmp = kc.shape[1]
    c_start = jnp.arange(n_cmp, dtype=jnp.int32) * CMP_STRIDE
    d_cmp = posf[:, None] - (c_start + CMP_LEN - 1).astype(jnp.float32)[None, :]
    s = jnp.einsum('btgrd,bngd->bgrtn', qg, kc).astype(jnp.float32) * scale - sl[None, :, :, None, None] * d_cmp
    p_cmp = _masked_softmax(s, (d_cmp >= 0.0)[None, None, None])
    o_cmp = jnp.einsum('bgrtn,bngd->btgrd', p_cmp.astype(vc.dtype), vc)
    n_sel = ks_blk.shape[2]
    s_start = jnp.arange(n_sel, dtype=jnp.int32) * SEL_LEN
    cover = jnp.maximum(jnp.minimum(c_start[:, None] + CMP_LEN, s_start[None, :] + SEL_LEN)
                        - jnp.maximum(c_start[:, None], s_start[None, :]), 0).astype(jnp.float32) / CMP_LEN
    imp = jnp.einsum('bgrtn,nj->bgtj', p_cmp, cover)
    blk = jnp.arange(n_sel, dtype=jnp.int32)
    avail = s_start[None, :] <= pos[:, None]
    forced = (blk[None, :] == (pos // SEL_LEN)[:, None]) | (blk[None, :] == 0)
    imp = jnp.where(avail, jnp.where(forced, FORCE, imp), -FORCE)
    n_top = min(SEL_TOP, n_sel)
    _, idx = lax.top_k(imp, n_top)
    b_i = jnp.arange(B)[:, None, None, None]
    g_i = jnp.arange(G)[None, :, None, None]
    ks_g = ks_blk[b_i, g_i, idx]
    vs_g = vs_blk[b_i, g_i, idx]
    kpos = idx[..., None] * SEL_LEN + jnp.arange(SEL_LEN, dtype=jnp.int32)
    d_sel = (pos[None, None, :, None, None] - kpos).astype(jnp.float32)[:, :, None]
    s = jnp.einsum('btgrd,bgtnsd->bgrtns', qg, ks_g).astype(jnp.float32) * scale - sl[None, :, :, None, None, None] * d_sel
    mask = jnp.broadcast_to(d_sel >= 0.0, s.shape)
    p_sel = _masked_softmax(s.reshape(B, G, R, Tq, -1), mask.reshape(B, G, R, Tq, -1)).reshape(s.shape)
    o_sel = jnp.einsum('bgrtns,bgtnsd->btgrd', p_sel.astype(vs_g.dtype), vs_g)
    d_win = posf[:, None] - pos_w.astype(jnp.float32)[None, :]
    s = jnp.einsum('btgrd,bwgd->bgrtw', qg, kw).astype(jnp.float32) * scale - sl[None, :, :, None, None] * d_win
    win_mask = (d_win >= 0.0) & (d_win < WINDOW) & (pos_w >= 0)[None, :]
    p_win = _masked_softmax(s, win_mask[None, None, None])
    o_win = jnp.einsum('bgrtw,bwgd->btgrd', p_win.astype(vw.dtype), vw)
    return jnp.stack([o_cmp, o_sel, o_win]).reshape(3, B, Tq, H, d)


def _nsa_prompt(q, kc, vc, ks_blk, vs_blk, kw, vw, slopes):
    B, T, H, d = q.shape
    n_blk = T // Q_BLOCK
    span = WINDOW + Q_BLOCK
    pad = ((0, 0), (WINDOW, 0), (0, 0), (0, 0))
    kw_p, vw_p = jnp.pad(kw, pad), jnp.pad(vw, pad)

    def one_block(i):
        q0 = i * Q_BLOCK
        pos = q0 + jnp.arange(Q_BLOCK, dtype=jnp.int32)
        pos_w = q0 - WINDOW + jnp.arange(span, dtype=jnp.int32)
        return _nsa_branches(lax.dynamic_slice_in_dim(q, q0, Q_BLOCK, 1), pos, kc, vc, ks_blk, vs_blk,
                             lax.dynamic_slice_in_dim(kw_p, q0, span, 1),
                             lax.dynamic_slice_in_dim(vw_p, q0, span, 1), pos_w, slopes)

    o = lax.map(one_block, jnp.arange(n_blk, dtype=jnp.int32))
    return jnp.moveaxis(o, 0, 2).reshape(3, B, T, H, d)


def _rwkv_group(u, shift0, S0, lp):
    B, T, _ = u.shape
    W = RWKV_WIDTH
    uf = u.astype(jnp.float32)
    prev = jnp.concatenate([shift0.astype(jnp.float32)[:, None], uf[:, :-1]], axis=1)
    um = uf + (prev - uf) * lp['rwkv_mu']
    r, k, v = um[..., :W], um[..., W:2 * W], um[..., 2 * W:3 * W]
    o = 3 * W
    wd = um[..., o:o + RWKV_W_RANK]
    ad = um[..., o + RWKV_W_RANK:o + RWKV_W_RANK + RWKV_A_RANK]
    gd = um[..., o + RWKV_W_RANK + RWKV_A_RANK:]
    w = lp['rwkv_w0'] + jnp.tanh(wd) @ lp['rwkv_w_up']
    decay = jnp.exp(-jnp.exp(-jax.nn.softplus(-w) - 0.5))
    a = jax.nn.sigmoid(lp['rwkv_a0'] + ad @ lp['rwkv_a_up'])
    g = jax.nn.sigmoid(gd) @ lp['rwkv_g_up']
    kk = k * lp['rwkv_k_k']
    k = k * (1.0 + (a - 1.0) * lp['rwkv_k_a'])
    hd = lambda t: t.reshape(B, T, RWKV_HEADS, HEAD_DIM).astype(jnp.float32)
    r, k, v, decay, a, g, kk = hd(r), hd(k), hd(v), hd(decay), hd(a), hd(g), hd(kk)
    kk = kk / jnp.maximum(jnp.sqrt(jnp.sum(kk * kk, axis=-1, keepdims=True)), 1e-12)

    def step(S, inp):
        r_t, w_t, k_t, v_t, kk_t, a_t = inp
        sa = jnp.einsum('bhvk,bhk->bhv', S, kk_t)
        S = S * w_t[:, :, None, :] - sa[..., None] * (kk_t * a_t)[:, :, None, :] + v_t[..., None] * k_t[:, :, None, :]
        return S, jnp.einsum('bhvk,bhk->bhv', S, r_t)

    xs = tuple(jnp.moveaxis(t, 1, 0) for t in (r, decay, k, v, kk, a))
    S_T, ys = lax.scan(step, S0.astype(jnp.float32), xs)
    y = _group_norm(jnp.moveaxis(ys, 0, 1), lp['rwkv_ln_g'], RWKV_GN_EPS) + lp['rwkv_ln_b']
    y = y + jnp.sum(r * k * lp['rwkv_r_k'], axis=-1, keepdims=True) * v
    y = y * g
    return y.reshape(B, T, W).astype(u.dtype), u[:, -1], S_T


def _retention_group(u, pos, S0, ln_g):
    B, T, _ = u.shape
    uf = u.astype(jnp.float32)
    q, k, v, g = [t.reshape(B, T, RET_HEADS, HEAD_DIM) for t in jnp.split(uf, 4, axis=-1)]
    q = _rotary(q, pos)
    k = _rotary(k, pos) * HEAD_DIM ** -0.5
    lg = jnp.log(1.0 - 2.0 ** (-5.0 - jnp.arange(RET_HEADS, dtype=jnp.float32)))
    C = RET_CHUNK if T % RET_CHUNK == 0 else T
    nC = T // C
    n = jnp.arange(C, dtype=jnp.float32)
    diff = n[:, None] - n[None, :]
    dmask = jnp.where(diff[None] >= 0, jnp.exp(jnp.maximum(diff, 0.0)[None] * lg[:, None, None]), 0.0)
    q_dec = jnp.exp((n[:, None] + 1.0) * lg[None, :])
    k_dec = jnp.exp((C - 1.0 - n)[:, None] * lg[None, :])
    s_dec = jnp.exp(C * lg)

    def chunk(S, inp):
        qc, kc, vc = inp
        att = jnp.einsum('bnhd,bmhd->bhnm', qc, kc) * dmask
        out = jnp.einsum('bhnm,bmhe->bnhe', att, vc) + jnp.einsum('bnhd,bhde->bnhe', qc, S) * q_dec[None, :, :, None]
        S = S * s_dec[None, :, None, None] + jnp.einsum('bmhd,bmhe->bhde', kc * k_dec[None, :, :, None], vc)
        return S, out

    to_chunks = lambda t: jnp.moveaxis(t.reshape(B, nC, C, RET_HEADS, HEAD_DIM), 1, 0)
    S_T, o = lax.scan(chunk, S0.astype(jnp.float32), (to_chunks(q), to_chunks(k), to_chunks(v)))
    o = jnp.moveaxis(o, 0, 1).reshape(B, T, RET_HEADS, HEAD_DIM)
    y = jax.nn.silu(g) * _group_norm(o, ln_g, GN_EPS)
    return y.reshape(B, T, RET_WIDTH).astype(u.dtype), S_T


def _mixing_sublayer(x, lp, past_len, past_cmp, past_sel, win_buf, rwkv_S, rwkv_shift, ret_S):
    B, T, _ = x.shape
    h = _rms_norm(x, lp['norm_attn'])
    P = h @ lp['w_in']
    c = P[..., :NSA_IN]
    q = _rms_norm(c[..., :NSA_WIDTH].reshape(B, T, NSA_HEADS, HEAD_DIM), lp['nsa_q_norm'])
    kv = c[..., NSA_WIDTH:NSA_WIDTH + 6 * NSA_KV_COLS].reshape(B, T, 3, 2, NSA_KV_HEADS, HEAD_DIM)
    kv_cmp, kv_sel, kv_win = kv[:, :, 0], kv[:, :, 1], kv[:, :, 2]
    gates = jax.nn.sigmoid(c[..., NSA_WIDTH + 6 * NSA_KV_COLS:].astype(jnp.float32)).reshape(B, T, NSA_HEADS, 3)
    pos = past_len + jnp.arange(T, dtype=jnp.int32)
    slopes = _alibi_slopes(NSA_HEADS)
    prompt = past_cmp is None
    if prompt:
        rows_cmp, rows_sel, rows_win = kv_cmp, kv_sel, kv_win
        new_win = kv_win[:, T - min(WINDOW, T):]
    else:
        rows_cmp = jnp.concatenate([past_cmp, kv_cmp.astype(past_cmp.dtype)], axis=1)
        rows_sel = jnp.concatenate([past_sel, kv_sel.astype(past_sel.dtype)], axis=1)
        rows_win = jnp.concatenate([win_buf, kv_win.astype(win_buf.dtype)], axis=1)
        new_win = rows_win[:, T:]
    k_norm = lp['nsa_k_norm']
    kc = _rms_norm(_nsa_compress(rows_cmp[:, :, 0], lp['nsa_cmp_pe'][0], lp['nsa_cmp_w1'][0], lp['nsa_cmp_w2'][0]), k_norm[0])
    vc = _nsa_compress(rows_cmp[:, :, 1], lp['nsa_cmp_pe'][1], lp['nsa_cmp_w1'][1], lp['nsa_cmp_w2'][1])
    ks_blk = _to_sel_blocks(_rms_norm(rows_sel[:, :, 0], k_norm[1]))
    vs_blk = _to_sel_blocks(rows_sel[:, :, 1])
    kw = _rms_norm(rows_win[:, :, 0], k_norm[2])
    vw = rows_win[:, :, 1]
    if prompt:
        o3 = _nsa_prompt(q, kc, vc, ks_blk, vs_blk, kw, vw, slopes)
    else:
        wb = win_buf.shape[1]
        pos_w = past_len - wb + jnp.arange(wb + T, dtype=jnp.int32)
        o3 = _nsa_branches(q, pos, kc, vc, ks_blk, vs_blk, kw, vw, pos_w, slopes)
    o_nsa = jnp.einsum('btha,abthd->bthd', gates.astype(o3.dtype), o3)
    o_nsa = _rms_norm(o_nsa, lp['nsa_out_norm']).reshape(B, T, NSA_WIDTH)
    y_rwkv, new_shift, new_rwkv = _rwkv_group(P[..., NSA_IN:NSA_IN + RWKV_IN], rwkv_shift, rwkv_S, lp)
    y_ret, new_ret = _retention_group(P[..., NSA_IN + RWKV_IN:], pos, ret_S, lp['ret_ln_g'])
    mix = jnp.concatenate([o_nsa, y_rwkv.astype(o_nsa.dtype), y_ret.astype(o_nsa.dtype)], axis=-1)
    x = x + mix @ lp['w_out']
    return x, (kv_cmp, kv_sel, new_win, new_rwkv, new_shift, new_ret)


def _dense_ffn(x, g, wg, wu, wd):
    shp = x.shape
    x2 = x.reshape(-1, shp[-1])
    ones = jnp.ones((x2.shape[0], 1), jnp.float32)
    return (x2 + _ffn(x2, g, ones, wg, wu, wd)).reshape(shp)


def _moe_ffn(x, g, router, wg, wu, wd):
    shp = x.shape
    x2 = x.reshape(-1, shp[-1])
    h = _rms_norm(x2, g)
    logits = (h @ router).astype(jnp.float32)
    top_val, top_idx = lax.top_k(logits, TOP_K)
    gate = jax.nn.softmax(top_val, axis=-1)
    combine = jnp.einsum('...k,...ke->...e', gate, jax.nn.one_hot(top_idx, N_EXPERTS, dtype=jnp.float32))
    y = x2
    for e in range(N_EXPERTS):
        y = y + _ffn(x2, g, combine[:, e:e + 1], wg[e], wu[e], wd[e])
    return y.reshape(shp)


def kernel(x_prompt, x_sample, cache_nsa_cmp, cache_nsa_sel, cache_nsa_win, state_rwkv, state_rwkv_shift,
           state_ret, page_table, norm_attn, norm_ffn, w_in, w_out, nsa_q_norm, nsa_k_norm, nsa_cmp_pe,
           nsa_cmp_w1, nsa_cmp_w2, nsa_out_norm, rwkv_mu, rwkv_w0, rwkv_w_up, rwkv_a0, rwkv_a_up, rwkv_g_up,
           rwkv_k_k, rwkv_k_a, rwkv_r_k, rwkv_ln_g, rwkv_ln_b, ret_ln_g, ffn_w_gate, ffn_w_up, ffn_w_down,
           moe_router, moe_w_gate, moe_w_up, moe_w_down):
    Bp, Bs = x_prompt.shape[0], x_sample.shape[0]
    past_len = page_table.shape[1] * cache_nsa_cmp.shape[2]
    xp, xs = x_prompt, x_sample
    outs_p, outs_s = [], []
    for l in range(DEPTH):
        lp = {'norm_attn': norm_attn[l], 'w_in': w_in[l], 'w_out': w_out[l], 'nsa_q_norm': nsa_q_norm[l],
              'nsa_k_norm': nsa_k_norm[l], 'nsa_cmp_pe': nsa_cmp_pe[l], 'nsa_cmp_w1': nsa_cmp_w1[l],
              'nsa_cmp_w2': nsa_cmp_w2[l], 'nsa_out_norm': nsa_out_norm[l], 'rwkv_mu': rwkv_mu[l],
              'rwkv_w0': rwkv_w0[l], 'rwkv_w_up': rwkv_w_up[l], 'rwkv_a0': rwkv_a0[l], 'rwkv_a_up': rwkv_a_up[l],
              'rwkv_g_up': rwkv_g_up[l], 'rwkv_k_k': rwkv_k_k[l], 'rwkv_k_a': rwkv_k_a[l], 'rwkv_r_k': rwkv_r_k[l],
              'rwkv_ln_g': rwkv_ln_g[l], 'rwkv_ln_b': rwkv_ln_b[l], 'ret_ln_g': ret_ln_g[l]}
        xp, st = _mixing_sublayer(xp, lp, 0, None, None, None,
                                  jnp.zeros((Bp, RWKV_HEADS, HEAD_DIM, HEAD_DIM), jnp.float32),
                                  jnp.zeros((Bp, RWKV_IN), xp.dtype),
                                  jnp.zeros((Bp, RET_HEADS, HEAD_DIM, HEAD_DIM), jnp.float32))
        outs_p.append(st)
        past_cmp = cache_nsa_cmp[l][page_table].reshape(Bs, past_len, 2, NSA_KV_HEADS, HEAD_DIM)
        past_sel = cache_nsa_sel[l][page_table].reshape(Bs, past_len, 2, NSA_KV_HEADS, HEAD_DIM)
        xs, st = _mixing_sublayer(xs, lp, past_len, past_cmp, past_sel, cache_nsa_win[l], state_rwkv[l],
                                  state_rwkv_shift[l], state_ret[l])
        outs_s.append(st)
        i = l // 2
        if l % 2 == 0:
            xp = _dense_ffn(xp, norm_ffn[l], ffn_w_gate[i], ffn_w_up[i], ffn_w_down[i])
            xs = _dense_ffn(xs, norm_ffn[l], ffn_w_gate[i], ffn_w_up[i], ffn_w_down[i])
        else:
            xp = _moe_ffn(xp, norm_ffn[l], moe_router[i], moe_w_gate[i], moe_w_up[i], moe_w_down[i])
            xs = _moe_ffn(xs, norm_ffn[l], moe_router[i], moe_w_gate[i], moe_w_up[i], moe_w_down[i])
    kv_cmp_p, kv_sel_p, win_p, rwkv_p, shift_p, ret_p = [jnp.stack([o[j] for o in outs_p]) for j in range(6)]
    kv_cmp_s, kv_sel_s, win_s, rwkv_s, shift_s, ret_s = [jnp.stack([o[j] for o in outs_s]) for j in range(6)]
    return (xp, xs, kv_cmp_p, kv_sel_p, win_p, rwkv_p, shift_p, ret_p,
            kv_cmp_s, kv_sel_s, win_s, rwkv_s, shift_s, ret_s)
```

```python
import functools

import jax
import jax.numpy as jnp
from jax import lax
from jax.experimental import pallas as pl
from jax.experimental.pallas import tpu as pltpu

D_MODEL = 1024
DEPTH = 2
HEAD_DIM = 64
NSA_WIDTH = D_MODEL // 2
RWKV_WIDTH = D_MODEL // 4
RET_WIDTH = D_MODEL - NSA_WIDTH - RWKV_WIDTH
NSA_HEADS = NSA_WIDTH // HEAD_DIM
NSA_KV_HEADS = 2
NSA_GROUP = NSA_HEADS // NSA_KV_HEADS
CMP_LEN = 32
CMP_STRIDE = 16
SEL_LEN = 64
SEL_TOP = 16
WINDOW = 512
Q_BLOCK = 128
RWKV_HEADS = RWKV_WIDTH // HEAD_DIM
RWKV_W_RANK = 64
RWKV_A_RANK = 64
RWKV_G_RANK = 128
RWKV_GN_EPS = 64e-5
RET_HEADS = RET_WIDTH // HEAD_DIM
RET_CHUNK = 128
ROPE_BASE = 10000.0
N_EXPERTS = 8
TOP_K = 2
NSA_KV_COLS = NSA_KV_HEADS * HEAD_DIM
NSA_IN = NSA_WIDTH + 6 * NSA_KV_COLS + 3 * NSA_HEADS
RWKV_IN = 3 * RWKV_WIDTH + RWKV_W_RANK + RWKV_A_RANK + RWKV_G_RANK
RET_IN = 4 * RET_WIDTH
EPS = 1e-6
GN_EPS = 1e-5
NEG = -1e30
TINY = 1e-30
FORCE = 1e9

VMEM_LIMIT_BYTES = 56 * 1024 * 1024


def _pick_tile(n, target):
    t = min(n, target)
    while n % t:
        t //= 2
    return t


def _ffn_body(x_ref, g_ref, s_ref, wg_ref, wu_ref, wd_ref, y_ref, h_scr, acc_scr):
    j = pl.program_id(1)

    @pl.when(j == 0)
    def _():
        x = x_ref[...]
        h = x * lax.rsqrt(jnp.mean(x * x, axis=-1, keepdims=True) + EPS) * g_ref[...]
        h_scr[...] = h.astype(jnp.bfloat16)
        acc_scr[...] = jnp.zeros_like(acc_scr)

    h = h_scr[...]
    a = jnp.dot(h, wg_ref[...].astype(jnp.bfloat16), preferred_element_type=jnp.float32)
    b = jnp.dot(h, wu_ref[...].astype(jnp.bfloat16), preferred_element_type=jnp.float32)
    z = (a * jax.nn.sigmoid(a)) * b
    acc_scr[...] += jnp.dot(z.astype(jnp.bfloat16), wd_ref[...].astype(jnp.bfloat16),
                            preferred_element_type=jnp.float32)

    @pl.when(j == pl.num_programs(1) - 1)
    def _():
        y_ref[...] = acc_scr[...] * s_ref[...]


def _ffn(x, g, scale, wg, wu, wd, *, tm=512, tf=256):
    M, D = x.shape
    F = wg.shape[1]
    tm = _pick_tile(M, tm)
    tf = _pick_tile(F, tf)
    return pl.pallas_call(
        _ffn_body,
        grid=(M // tm, F // tf),
        in_specs=[
            pl.BlockSpec((tm, D), lambda i, j: (i, 0)),
            pl.BlockSpec((1, D), lambda i, j: (0, 0)),
            pl.BlockSpec((tm, 1), lambda i, j: (i, 0)),
            pl.BlockSpec((D, tf), lambda i, j: (0, j)),
            pl.BlockSpec((D, tf), lambda i, j: (0, j)),
            pl.BlockSpec((tf, D), lambda i, j: (j, 0)),
        ],
        out_specs=pl.BlockSpec((tm, D), lambda i, j: (i, 0)),
        out_shape=jax.ShapeDtypeStruct((M, D), jnp.float32),
        scratch_shapes=[pltpu.VMEM((tm, D), jnp.bfloat16), pltpu.VMEM((tm, D), jnp.float32)],
        compiler_params=pltpu.CompilerParams(
            dimension_semantics=("parallel", "arbitrary"), vmem_limit_bytes=VMEM_LIMIT_BYTES),
        name="ffn",
    )(x, g.reshape(1, D), scale, wg, wu, wd)


def _rwkv_scan_body(r_ref, w_ref, k_ref, v_ref, kk_ref, b_ref, s0_ref, y_ref, st_ref, s_scr, *, sub):
    j = pl.program_id(1)
    n_b, t_blk, _ = r_ref.shape
    d = HEAD_DIM

    @pl.when(j == 0)
    def _():
        s_scr[...] = s0_ref[...]

    eye = lax.broadcasted_iota(jnp.int32, (d, d), 0) == lax.broadcasted_iota(jnp.int32, (d, d), 1)

    def sub_block(i, carry):
        t0 = pl.multiple_of(i * sub, sub)
        for bb in range(n_b):
            blk = [ref[bb, pl.ds(t0, sub), :] for ref in (r_ref, w_ref, k_ref, v_ref, kk_ref, b_ref)]
            for h in range(RWKV_HEADS):
                cols = slice(h * d, (h + 1) * d)
                S = s_scr[bb, h]
                y_rows = []
                for t in range(sub):
                    r_t, w_t, k_t, v_t, kk_t, b_t = [x[t:t + 1, cols] for x in blk]
                    sa = jnp.sum(S * kk_t, axis=1, keepdims=True)
                    v_col = jnp.sum(jnp.where(eye, v_t, 0.0), axis=1, keepdims=True)
                    S = S * w_t - sa * b_t + v_col * k_t
                    y_col = jnp.sum(S * r_t, axis=1, keepdims=True)
                    y_rows.append(jnp.sum(jnp.where(eye, y_col, 0.0), axis=0, keepdims=True))
                s_scr[bb, h] = S
                y_ref[bb, pl.ds(t0, sub), cols] = jnp.concatenate(y_rows, axis=0)
        return carry

    lax.fori_loop(0, t_blk // sub, sub_block, 0)

    @pl.when(j == pl.num_programs(1) - 1)
    def _():
        st_ref[...] = s_scr[...]


def _rwkv_scan(r, w, k, v, kk, b, s0, *, n_b=2, t_blk=256):
    B, T, W = r.shape
    n_b = _pick_tile(B, n_b)
    t_blk = _pick_tile(T, t_blk)
    sub = 8 if t_blk % 8 == 0 else t_blk
    seq = pl.BlockSpec((n_b, t_blk, W), lambda i, j: (i, j, 0))
    st = pl.BlockSpec((n_b, RWKV_HEADS, HEAD_DIM, HEAD_DIM), lambda i, j: (i, 0, 0, 0))
    return pl.pallas_call(
        functools.partial(_rwkv_scan_body, sub=sub),
        grid=(B // n_b, T // t_blk),
        in_specs=[seq] * 6 + [st],
        out_specs=[seq, st],
        out_shape=[jax.ShapeDtypeStruct((B, T, W), jnp.float32),
                   jax.ShapeDtypeStruct((B, RWKV_HEADS, HEAD_DIM, HEAD_DIM), jnp.float32)],
        scratch_shapes=[pltpu.VMEM((n_b, RWKV_HEADS, HEAD_DIM, HEAD_DIM), jnp.float32)],
        compiler_params=pltpu.CompilerParams(
            dimension_semantics=("parallel", "arbitrary"), vmem_limit_bytes=VMEM_LIMIT_BYTES),
        name="rwkv_scan",
    )(r, w, k, v, kk, b, s0)


RWKV_CHUNK = 64


_F32_DOT = dict(preferred_element_type=jnp.float32, precision=lax.Precision.HIGHEST)


def _dot(a, b):
    return jnp.dot(a, b, **_F32_DOT)


def _dot_t(a, b):
    return lax.dot_general(a, b, (((1,), (1,)), ((), ())), **_F32_DOT)


def _dot_0(a, b):
    return lax.dot_general(a, b, (((0,), (0,)), ((), ())), **_F32_DOT)


def _rwkv_chunk_body(r_ref, lw_ref, k_ref, v_ref, kk_ref, b_ref, qe_ref, y0_ref, pm_ref, z_ref):
    L, d = RWKV_CHUNK, HEAD_DIM
    row = lax.broadcasted_iota(jnp.int32, (L, L), 0)
    col = lax.broadcasted_iota(jnp.int32, (L, L), 1)
    lower = row >= col
    strict = row > col
    ones_lower = jnp.where(lower, 1.0, 0.0)
    eye = jnp.where(row == col, 1.0, 0.0)
    for h in range(RWKV_HEADS):
        cols = slice(h * d, (h + 1) * d)
        r, lw, k, v, kk, b = [ref[0, :, cols] for ref in (r_ref, lw_ref, k_ref, v_ref, kk_ref, b_ref)]
        G = _dot(ones_lower, lw)
        g_inv = jnp.exp(-G)
        kap, bt, kt, rt = kk * jnp.exp(G - lw), b * g_inv, k * g_inv, r * jnp.exp(G)
        N = jnp.where(strict, _dot_t(kap, bt), 0.0)
        Mk = jnp.where(strict, _dot_t(kap, kt), 0.0)
        RB = jnp.where(lower, _dot_t(rt, bt), 0.0)
        RK = jnp.where(lower, _dot_t(rt, kt), 0.0)
        X, P = eye - N, N
        for _ in range(L.bit_length() - 2):
            P = _dot(P, P)
            X = X + _dot(X, P)
        A = _dot(X, kap)
        C = _dot(X, _dot(Mk, v))
        qe_ref[0, h] = rt - _dot(RB, A)
        y0_ref[0, h] = _dot(RK, v) - _dot(RB, C)
        g_end = jnp.exp(G[L - 1:L, :])
        pm_ref[0, h, 0] = (eye - _dot_0(A, bt)) * g_end
        z_ref[0, h, 0] = (_dot_0(v, kt) - _dot_0(C, bt)) * g_end


def _rwkv_walk_body(qe_ref, y0_ref, pm_ref, z_ref, s0_ref, y_ref, st_ref, s_scr):
    j = pl.program_id(0)
    B, H, n_c = pm_ref.shape[:3]
    L = RWKV_CHUNK

    @pl.when(j == 0)
    def _():
        s_scr[...] = s0_ref[...]

    def one_chunk(c, carry):
        t0 = pl.multiple_of(c * L, L)
        for bb in range(B):
            for h in range(H):
                S = s_scr[bb, h]
                y_ref[bb, h, pl.ds(t0, L), :] = _dot_t(qe_ref[bb, h, pl.ds(t0, L), :], S) + y0_ref[bb, h, pl.ds(t0, L), :]
                s_scr[bb, h] = _dot(S, pm_ref[bb, h, c]) + z_ref[bb, h, c]
        return carry

    lax.fori_loop(0, n_c, one_chunk, 0)

    @pl.when(j == pl.num_programs(0) - 1)
    def _():
        st_ref[...] = s_scr[...]


def _rwkv_chunked(r, lw, k, v, kk, b, s0, *, chunks_per_step=16):
    B, T, W = r.shape
    H, d, L = RWKV_HEADS, HEAD_DIM, RWKV_CHUNK
    n_c = T // L
    seq = pl.BlockSpec((1, L, W), lambda i, c: (i, c, 0))
    per_tok = pl.BlockSpec((1, H, L, d), lambda i, c: (i, 0, c, 0))
    per_chunk = pl.BlockSpec((1, H, 1, d, d), lambda i, c: (i, 0, c, 0, 0))
    qe, y0, pm, z = pl.pallas_call(
        _rwkv_chunk_body,
        grid=(B, n_c),
        in_specs=[seq] * 6,
        out_specs=[per_tok, per_tok, per_chunk, per_chunk],
        out_shape=[jax.ShapeDtypeStruct((B, H, T, d), jnp.float32)] * 2
                  + [jax.ShapeDtypeStruct((B, H, n_c, d, d), jnp.float32)] * 2,
        compiler_params=pltpu.CompilerParams(
            dimension_semantics=("parallel", "parallel"), vmem_limit_bytes=VMEM_LIMIT_BYTES),
        name="rwkv_chunk",
    )(r, lw, k, v, kk, b)
    cs = _pick_tile(n_c, chunks_per_step)
    tok = pl.BlockSpec((B, H, cs * L, d), lambda j: (0, 0, j, 0))
    chk = pl.BlockSpec((B, H, cs, d, d), lambda j: (0, 0, j, 0, 0))
    st = pl.BlockSpec((B, H, d, d), lambda j: (0, 0, 0, 0))
    y, s_t = pl.pallas_call(
        _rwkv_walk_body,
        grid=(n_c // cs,),
        in_specs=[tok, tok, chk, chk, st],
        out_specs=[tok, st],
        out_shape=[jax.ShapeDtypeStruct((B, H, T, d), jnp.float32), jax.ShapeDtypeStruct((B, H, d, d), jnp.float32)],
        scratch_shapes=[pltpu.VMEM((B, H, d, d), jnp.float32)],
        compiler_params=pltpu.CompilerParams(
            dimension_semantics=("arbitrary",), vmem_limit_bytes=VMEM_LIMIT_BYTES),
        name="rwkv_walk",
    )(qe, y0, pm, z, s0)
    return jnp.transpose(y, (0, 2, 1, 3)).reshape(B, T, W), s_t


SEL_CHUNK = 512
WIN_SPAN = WINDOW + Q_BLOCK


def _softmax_cols(s, valid):
    s = jnp.where(valid, s, NEG)
    m = jnp.max(s, axis=0, keepdims=True)
    e = jnp.where(valid, jnp.exp(s - m), 0.0)
    return e / jnp.maximum(jnp.sum(e, axis=0, keepdims=True), TINY)


def _nsa_prompt_body(qT_ref, kc_ref, vcT_ref, ks_ref, vsT_ref, kw_ref, vwT_ref, covT_ref, o_ref,
                     sel_scr, m_scr, l_scr, acc_scr):
    g = pl.program_id(1)
    i = pl.program_id(2)
    R = NSA_GROUP
    n_cmp = kc_ref.shape[2]
    n_sel = covT_ref.shape[0]
    q0 = i * Q_BLOCK
    pos = q0 + lax.broadcasted_iota(jnp.int32, (1, Q_BLOCK), 1)
    posf = pos.astype(jnp.float32)
    slopes = [jnp.where(g == 0, 2.0 ** -(r + 1), 2.0 ** -(R + r + 1)).astype(jnp.float32) for r in range(R)]

    c_end = (lax.broadcasted_iota(jnp.int32, (n_cmp, Q_BLOCK), 0) * CMP_STRIDE + (CMP_LEN - 1)).astype(jnp.float32)
    d_cmp = posf - c_end
    valid_c = d_cmp >= 0.0
    kc = kc_ref[0, 0]
    vcT = vcT_ref[0, 0]
    p_sum = jnp.zeros((n_cmp, Q_BLOCK), jnp.float32)
    for r in range(R):
        s = jnp.dot(kc, qT_ref[0, 0, r], preferred_element_type=jnp.float32) - slopes[r] * d_cmp
        p = _softmax_cols(s, valid_c)
        p_sum = p_sum + p
        o_ref[0, 0, 0, r] = jnp.dot(vcT, p.astype(jnp.bfloat16), preferred_element_type=jnp.float32)
    imp = jnp.dot(covT_ref[...], p_sum, preferred_element_type=jnp.float32, precision=lax.Precision.HIGHEST)
    blk = lax.broadcasted_iota(jnp.int32, (n_sel, Q_BLOCK), 0)
    avail = blk * SEL_LEN <= pos
    forced = (blk == jnp.right_shift(pos, 6)) | (blk == 0)
    imp = jnp.where(avail, jnp.where(forced, FORCE, imp), -FORCE)

    sub8 = lax.broadcasted_iota(jnp.int32, (8, Q_BLOCK), 0)
    for jj in range(n_sel // 8):
        vj = imp[8 * jj:8 * jj + 8]
        cnt = jnp.zeros((8, Q_BLOCK), jnp.float32)
        for k in range(n_sel):
            row = imp[k:k + 1]
            if k < 8 * jj:
                cnt = cnt + jnp.where(row >= vj, 1.0, 0.0)
            elif k >= 8 * jj + 8:
                cnt = cnt + jnp.where(row > vj, 1.0, 0.0)
            else:
                cnt = cnt + jnp.where(sub8 > (k - 8 * jj), jnp.where(row >= vj, 1.0, 0.0),
                                      jnp.where(row > vj, 1.0, 0.0))
        sel_scr[8 * jj:8 * jj + 8, :] = jnp.where(cnt < float(SEL_TOP), 1.0, 0.0)

    m_scr[...] = jnp.full_like(m_scr, NEG)
    l_scr[...] = jnp.zeros_like(l_scr)
    acc_scr[...] = jnp.zeros_like(acc_scr)
    key_iota = lax.broadcasted_iota(jnp.int32, (SEL_CHUNK, Q_BLOCK), 0).astype(jnp.float32)
    blocks_per_chunk = SEL_CHUNK // SEL_LEN

    def chunk(c, carry):
        k0 = pl.multiple_of(c * SEL_CHUNK, SEL_CHUNK)
        dist = (posf - k0.astype(jnp.float32)) - key_iota
        picked = jnp.concatenate(
            [jnp.broadcast_to(sel_scr[pl.ds(c * blocks_per_chunk + b, 1), :], (SEL_LEN, Q_BLOCK))
             for b in range(blocks_per_chunk)], axis=0)
        neg = jnp.where((picked > 0.5) & (dist >= 0.0), 0.0, NEG)
        ks = ks_ref[0, 0, pl.ds(k0, SEL_CHUNK), :]
        vsT = vsT_ref[0, 0, :, pl.ds(k0, SEL_CHUNK)]
        for r in range(R):
            s = jnp.dot(ks, qT_ref[0, 0, r], preferred_element_type=jnp.float32) - slopes[r] * dist + neg
            m_old = m_scr[r]
            m_new = jnp.maximum(m_old, jnp.max(s, axis=0, keepdims=True))
            p = jnp.exp(s - m_new)
            alpha = jnp.exp(m_old - m_new)
            l_scr[r] = alpha * l_scr[r] + jnp.sum(p, axis=0, keepdims=True)
            acc_scr[r] = alpha * acc_scr[r] + jnp.dot(vsT, p.astype(jnp.bfloat16), preferred_element_type=jnp.float32)
            m_scr[r] = m_new
        return carry

    lax.fori_loop(0, q0 // SEL_CHUNK + 1, chunk, 0)
    for r in range(R):
        o_ref[1, 0, 0, r] = acc_scr[r] / jnp.maximum(l_scr[r], TINY)

    k0w = pl.multiple_of(jnp.maximum(q0 - WINDOW, 0), Q_BLOCK)
    d_win = (posf - k0w.astype(jnp.float32)) - lax.broadcasted_iota(jnp.int32, (WIN_SPAN, Q_BLOCK), 0).astype(jnp.float32)
    valid_w = (d_win >= 0.0) & (d_win < float(WINDOW))
    kw = kw_ref[0, 0, pl.ds(k0w, WIN_SPAN), :]
    vwT = vwT_ref[0, 0, :, pl.ds(k0w, WIN_SPAN)]
    for r in range(R):
        s = jnp.dot(kw, qT_ref[0, 0, r], preferred_element_type=jnp.float32) - slopes[r] * d_win
        p = _softmax_cols(s, valid_w)
        o_ref[2, 0, 0, r] = jnp.dot(vwT, p.astype(jnp.bfloat16), preferred_element_type=jnp.float32)


def _nsa_prompt_attn(q, kc, vc, ks, vs, kw, vw):
    B, T, H, d = q.shape
    G, R = NSA_KV_HEADS, NSA_GROUP
    n_c = kc.shape[1]
    n_cmp = -(-n_c // 128) * 128
    n_sel = T // SEL_LEN
    bf = jnp.bfloat16
    qT = jnp.transpose((q * (d ** -0.5)).reshape(B, T, G, R, d), (0, 2, 3, 4, 1)).astype(bf)
    rows = lambda t: jnp.transpose(t, (0, 2, 1, 3)).astype(bf)
    cols = lambda t: jnp.transpose(t, (0, 2, 3, 1)).astype(bf)
    pad_c = ((0, 0), (0, n_cmp - n_c), (0, 0), (0, 0))
    kc_r, vc_c = rows(jnp.pad(kc, pad_c)), cols(jnp.pad(vc, pad_c))
    c_start = jnp.arange(n_cmp, dtype=jnp.int32) * CMP_STRIDE
    s_start = jnp.arange(n_sel, dtype=jnp.int32) * SEL_LEN
    covT = jnp.maximum(jnp.minimum(c_start[None, :] + CMP_LEN, s_start[:, None] + SEL_LEN)
                       - jnp.maximum(c_start[None, :], s_start[:, None]), 0).astype(jnp.float32) / CMP_LEN
    full_r = lambda n: pl.BlockSpec((1, 1, n, d), lambda b, g, i: (b, g, 0, 0))
    full_c = lambda n: pl.BlockSpec((1, 1, d, n), lambda b, g, i: (b, g, 0, 0))
    oT = pl.pallas_call(
        _nsa_prompt_body,
        grid=(B, G, T // Q_BLOCK),
        in_specs=[pl.BlockSpec((1, 1, R, d, Q_BLOCK), lambda b, g, i: (b, g, 0, 0, i)),
                  full_r(n_cmp), full_c(n_cmp), full_r(T), full_c(T), full_r(T), full_c(T),
                  pl.BlockSpec((n_sel, n_cmp), lambda b, g, i: (0, 0))],
        out_specs=pl.BlockSpec((3, 1, 1, R, d, Q_BLOCK), lambda b, g, i: (0, b, g, 0, 0, i)),
        out_shape=jax.ShapeDtypeStruct((3, B, G, R, d, T), jnp.float32),
        scratch_shapes=[pltpu.VMEM((n_sel, Q_BLOCK), jnp.float32),
                        pltpu.VMEM((R, 1, Q_BLOCK), jnp.float32),
                        pltpu.VMEM((R, 1, Q_BLOCK), jnp.float32),
                        pltpu.VMEM((R, d, Q_BLOCK), jnp.float32)],
        compiler_params=pltpu.CompilerParams(
            dimension_semantics=("parallel", "parallel", "arbitrary"), vmem_limit_bytes=VMEM_LIMIT_BYTES),
        name="nsa_prompt",
    )(qT, kc_r, vc_c, rows(ks), cols(vs), rows(kw), cols(vw), covT)
    return jnp.transpose(oT, (0, 1, 5, 2, 3, 4)).reshape(3, B, T, H, d)


def _rms_norm(x, g):
    xf = x.astype(jnp.float32)
    y = xf * lax.rsqrt(jnp.mean(xf * xf, axis=-1, keepdims=True) + EPS)
    return (y * g.astype(jnp.float32)).astype(x.dtype)


def _group_norm(x, g, eps):
    xf = x.astype(jnp.float32)
    mu = jnp.mean(xf, axis=-1, keepdims=True)
    var = jnp.mean(jnp.square(xf - mu), axis=-1, keepdims=True)
    return (xf - mu) * lax.rsqrt(var + eps) * g.astype(jnp.float32)


def _masked_softmax(s, mask):
    s = jnp.where(mask, s, NEG)
    m = jnp.max(s, axis=-1, keepdims=True)
    e = jnp.where(mask, jnp.exp(s - m), 0.0)
    return e / jnp.maximum(jnp.sum(e, axis=-1, keepdims=True), TINY)


def _alibi_slopes(n):
    return 2.0 ** (-8.0 * jnp.arange(1, n + 1, dtype=jnp.float32) / n)


def _rotary(x, pos):
    half = x.shape[-1] // 2
    freqs = ROPE_BASE ** (-jnp.arange(half, dtype=jnp.float32) / half)
    ang = pos.astype(jnp.float32)[:, None] * freqs[None, :]
    cos, sin = jnp.cos(ang)[None, :, None, :], jnp.sin(ang)[None, :, None, :]
    x1, x2 = x[..., :half], x[..., half:]
    return jnp.concatenate([x1 * cos - x2 * sin, x1 * sin + x2 * cos], axis=-1)


def _nsa_compress(rows, pe, w1, w2):
    B, T, G, d = rows.shape
    n_cmp = (T - CMP_LEN) // CMP_STRIDE + 1
    idx = (jnp.arange(n_cmp, dtype=jnp.int32) * CMP_STRIDE)[:, None] + jnp.arange(CMP_LEN, dtype=jnp.int32)[None, :]
    blk = rows[:, idx] + pe[None, None, :, None, :]
    blk = jnp.transpose(blk, (0, 1, 3, 2, 4)).reshape(B, n_cmp, G, CMP_LEN * d)
    return jax.nn.gelu(blk @ w1) @ w2


def _to_sel_blocks(rows):
    B, T, G, d = rows.shape
    n_sel = -(-T // SEL_LEN)
    rows = jnp.pad(rows, ((0, 0), (0, n_sel * SEL_LEN - T), (0, 0), (0, 0)))
    return jnp.transpose(rows.reshape(B, n_sel, SEL_LEN, G, d), (0, 3, 1, 2, 4))


def _nsa_branches(q, pos, kc, vc, ks_blk, vs_blk, kw, vw, pos_w, slopes):
    B, Tq, H, d = q.shape
    G, R = NSA_KV_HEADS, NSA_GROUP
    scale = d ** -0.5
    qg = q.reshape(B, Tq, G, R, d)
    sl = slopes.reshape(G, R)
    posf = pos.astype(jnp.float32)
    n_cmp = kc.shape[1]
    c_start = jnp.arange(n_cmp, dtype=jnp.int32) * CMP_STRIDE
    d_cmp = posf[:, None] - (c_start + CMP_LEN - 1).astype(jnp.float32)[None, :]
    s = jnp.einsum('btgrd,bngd->bgrtn', qg, kc).astype(jnp.float32) * scale - sl[None, :, :, None, None] * d_cmp
    p_cmp = _masked_softmax(s, (d_cmp >= 0.0)[None, None, None])
    o_cmp = jnp.einsum('bgrtn,bngd->btgrd', p_cmp.astype(vc.dtype), vc)
    n_sel = ks_blk.shape[2]
    s_start = jnp.arange(n_sel, dtype=jnp.int32) * SEL_LEN
    cover = jnp.maximum(jnp.minimum(c_start[:, None] + CMP_LEN, s_start[None, :] + SEL_LEN)
                        - jnp.maximum(c_start[:, None], s_start[None, :]), 0).astype(jnp.float32) / CMP_LEN
    imp = jnp.einsum('bgrtn,nj->bgtj', p_cmp, cover)
    blk = jnp.arange(n_sel, dtype=jnp.int32)
    avail = s_start[None, :] <= pos[:, None]
    forced = (blk[None, :] == (pos // SEL_LEN)[:, None]) | (blk[None, :] == 0)
    imp = jnp.where(avail, jnp.where(forced, FORCE, imp), -FORCE)
    n_top = min(SEL_TOP, n_sel)
    _, idx = lax.top_k(imp, n_top)
    b_i = jnp.arange(B)[:, None, None, None]
    g_i = jnp.arange(G)[None, :, None, None]
    ks_g = ks_blk[b_i, g_i, idx]
    vs_g = vs_blk[b_i, g_i, idx]
    kpos = idx[..., None] * SEL_LEN + jnp.arange(SEL_LEN, dtype=jnp.int32)
    d_sel = (pos[None, None, :, None, None] - kpos).astype(jnp.float32)[:, :, None]
    s = jnp.einsum('btgrd,bgtnsd->bgrtns', qg, ks_g).astype(jnp.float32) * scale - sl[None, :, :, None, None, None] * d_sel
    mask = jnp.broadcast_to(d_sel >= 0.0, s.shape)
    p_sel = _masked_softmax(s.reshape(B, G, R, Tq, -1), mask.reshape(B, G, R, Tq, -1)).reshape(s.shape)
    o_sel = jnp.einsum('bgrtns,bgtnsd->btgrd', p_sel.astype(vs_g.dtype), vs_g)
    d_win = posf[:, None] - pos_w.astype(jnp.float32)[None, :]
    s = jnp.einsum('btgrd,bwgd->bgrtw', qg, kw).astype(jnp.float32) * scale - sl[None, :, :, None, None] * d_win
    win_mask = (d_win >= 0.0) & (d_win < WINDOW) & (pos_w >= 0)[None, :]
    p_win = _masked_softmax(s, win_mask[None, None, None])
    o_win = jnp.einsum('bgrtw,bwgd->btgrd', p_win.astype(vw.dtype), vw)
    return jnp.stack([o_cmp, o_sel, o_win]).reshape(3, B, Tq, H, d)


def _rwkv_group(u, shift0, S0, lp):
    B, T, _ = u.shape
    W = RWKV_WIDTH
    uf = u.astype(jnp.float32)
    prev = jnp.concatenate([shift0.astype(jnp.float32)[:, None], uf[:, :-1]], axis=1)
    um = uf + (prev - uf) * lp['rwkv_mu']
    r, k, v = um[..., :W], um[..., W:2 * W], um[..., 2 * W:3 * W]
    o = 3 * W
    wd = um[..., o:o + RWKV_W_RANK]
    ad = um[..., o + RWKV_W_RANK:o + RWKV_W_RANK + RWKV_A_RANK]
    gd = um[..., o + RWKV_W_RANK + RWKV_A_RANK:]
    w = lp['rwkv_w0'] + jnp.tanh(wd) @ lp['rwkv_w_up']
    log_decay = -jnp.exp(-jax.nn.softplus(-w) - 0.5)
    a = jax.nn.sigmoid(lp['rwkv_a0'] + ad @ lp['rwkv_a_up'])
    g = jax.nn.sigmoid(gd) @ lp['rwkv_g_up']
    kk = k * lp['rwkv_k_k']
    k = k * (1.0 + (a - 1.0) * lp['rwkv_k_a'])
    hd = lambda t: t.reshape(B, T, RWKV_HEADS, HEAD_DIM).astype(jnp.float32)
    r, k, v, a, g, kk = hd(r), hd(k), hd(v), hd(a), hd(g), hd(kk)
    kk = kk / jnp.maximum(jnp.sqrt(jnp.sum(kk * kk, axis=-1, keepdims=True)), 1e-12)
    flat = lambda t: t.reshape(B, T, W)
    if T % RWKV_CHUNK == 0:
        ys, S_T = _rwkv_chunked(flat(r), log_decay, flat(k), flat(v), flat(kk), flat(kk * a), S0.astype(jnp.float32))
    else:
        ys, S_T = _rwkv_scan(flat(r), jnp.exp(log_decay), flat(k), flat(v), flat(kk), flat(kk * a),
                             S0.astype(jnp.float32))
    y = _group_norm(ys.reshape(B, T, RWKV_HEADS, HEAD_DIM), lp['rwkv_ln_g'], RWKV_GN_EPS) + lp['rwkv_ln_b']
    y = y + jnp.sum(r * k * lp['rwkv_r_k'], axis=-1, keepdims=True) * v
    y = y * g
    return y.reshape(B, T, W).astype(u.dtype), u[:, -1], S_T


def _retention_group(u, pos, S0, ln_g):
    B, T, _ = u.shape
    uf = u.astype(jnp.float32)
    q, k, v, g = [t.reshape(B, T, RET_HEADS, HEAD_DIM) for t in jnp.split(uf, 4, axis=-1)]
    q = _rotary(q, pos)
    k = _rotary(k, pos) * HEAD_DIM ** -0.5
    lg = jnp.log(1.0 - 2.0 ** (-5.0 - jnp.arange(RET_HEADS, dtype=jnp.float32)))
    C = RET_CHUNK if T % RET_CHUNK == 0 else T
    nC = T // C
    n = jnp.arange(C, dtype=jnp.float32)
    diff = n[:, None] - n[None, :]
    dmask = jnp.where(diff[None] >= 0, jnp.exp(jnp.maximum(diff, 0.0)[None] * lg[:, None, None]), 0.0)
    q_dec = jnp.exp((n[:, None] + 1.0) * lg[None, :])
    k_dec = jnp.exp((C - 1.0 - n)[:, None] * lg[None, :])
    s_dec = jnp.exp(C * lg)

    def chunk(S, inp):
        qc, kc, vc = inp
        att = jnp.einsum('bnhd,bmhd->bhnm', qc, kc) * dmask
        out = jnp.einsum('bhnm,bmhe->bnhe', att, vc) + jnp.einsum('bnhd,bhde->bnhe', qc, S) * q_dec[None, :, :, None]
        S = S * s_dec[None, :, None, None] + jnp.einsum('bmhd,bmhe->bhde', kc * k_dec[None, :, :, None], vc)
        return S, out

    to_chunks = lambda t: jnp.moveaxis(t.reshape(B, nC, C, RET_HEADS, HEAD_DIM), 1, 0)
    S_T, o = lax.scan(chunk, S0.astype(jnp.float32), (to_chunks(q), to_chunks(k), to_chunks(v)))
    o = jnp.moveaxis(o, 0, 1).reshape(B, T, RET_HEADS, HEAD_DIM)
    y = jax.nn.silu(g) * _group_norm(o, ln_g, GN_EPS)
    return y.reshape(B, T, RET_WIDTH).astype(u.dtype), S_T


def _mixing_sublayer(x, lp, past_len, past_cmp, past_sel, win_buf, rwkv_S, rwkv_shift, ret_S):
    B, T, _ = x.shape
    h = _rms_norm(x, lp['norm_attn'])
    P = h @ lp['w_in']
    c = P[..., :NSA_IN]
    q = _rms_norm(c[..., :NSA_WIDTH].reshape(B, T, NSA_HEADS, HEAD_DIM), lp['nsa_q_norm'])
    kv = c[..., NSA_WIDTH:NSA_WIDTH + 6 * NSA_KV_COLS].reshape(B, T, 3, 2, NSA_KV_HEADS, HEAD_DIM)
    kv_cmp, kv_sel, kv_win = kv[:, :, 0], kv[:, :, 1], kv[:, :, 2]
    gates = jax.nn.sigmoid(c[..., NSA_WIDTH + 6 * NSA_KV_COLS:].astype(jnp.float32)).reshape(B, T, NSA_HEADS, 3)
    pos = past_len + jnp.arange(T, dtype=jnp.int32)
    slopes = _alibi_slopes(NSA_HEADS)
    prompt = past_cmp is None
    if prompt:
        rows_cmp, rows_sel, rows_win = kv_cmp, kv_sel, kv_win
        new_win = kv_win[:, T - min(WINDOW, T):]
    else:
        rows_cmp = jnp.concatenate([past_cmp, kv_cmp.astype(past_cmp.dtype)], axis=1)
        rows_sel = jnp.concatenate([past_sel, kv_sel.astype(past_sel.dtype)], axis=1)
        rows_win = jnp.concatenate([win_buf, kv_win.astype(win_buf.dtype)], axis=1)
        new_win = rows_win[:, T:]
    k_norm = lp['nsa_k_norm']
    kc = _rms_norm(_nsa_compress(rows_cmp[:, :, 0], lp['nsa_cmp_pe'][0], lp['nsa_cmp_w1'][0], lp['nsa_cmp_w2'][0]), k_norm[0])
    vc = _nsa_compress(rows_cmp[:, :, 1], lp['nsa_cmp_pe'][1], lp['nsa_cmp_w1'][1], lp['nsa_cmp_w2'][1])
    kw = _rms_norm(rows_win[:, :, 0], k_norm[2])
    vw = rows_win[:, :, 1]
    if prompt:
        o3 = _nsa_prompt_attn(q, kc, vc, _rms_norm(rows_sel[:, :, 0], k_norm[1]), rows_sel[:, :, 1], kw, vw)
    else:
        ks_blk = _to_sel_blocks(_rms_norm(rows_sel[:, :, 0], k_norm[1]))
        vs_blk = _to_sel_blocks(rows_sel[:, :, 1])
        wb = win_buf.shape[1]
        pos_w = past_len - wb + jnp.arange(wb + T, dtype=jnp.int32)
        o3 = _nsa_branches(q, pos, kc, vc, ks_blk, vs_blk, kw, vw, pos_w, slopes)
    o_nsa = jnp.einsum('btha,abthd->bthd', gates.astype(o3.dtype), o3)
    o_nsa = _rms_norm(o_nsa, lp['nsa_out_norm']).reshape(B, T, NSA_WIDTH)
    y_rwkv, new_shift, new_rwkv = _rwkv_group(P[..., NSA_IN:NSA_IN + RWKV_IN], rwkv_shift, rwkv_S, lp)
    y_ret, new_ret = _retention_group(P[..., NSA_IN + RWKV_IN:], pos, ret_S, lp['ret_ln_g'])
    mix = jnp.concatenate([o_nsa, y_rwkv.astype(o_nsa.dtype), y_ret.astype(o_nsa.dtype)], axis=-1)
    x = x + mix @ lp['w_out']
    return x, (kv_cmp, kv_sel, new_win, new_rwkv, new_shift, new_ret)


def _dense_ffn(x, g, wg, wu, wd):
    shp = x.shape
    x2 = x.reshape(-1, shp[-1])
    ones = jnp.ones((x2.shape[0], 1), jnp.float32)
    return (x2 + _ffn(x2, g, ones, wg, wu, wd)).reshape(shp)


def _moe_ffn(x, g, router, wg, wu, wd):
    shp = x.shape
    x2 = x.reshape(-1, shp[-1])
    h = _rms_norm(x2, g)
    logits = (h @ router).astype(jnp.float32)
    top_val, top_idx = lax.top_k(logits, TOP_K)
    gate = jax.nn.softmax(top_val, axis=-1)
    combine = jnp.einsum('...k,...ke->...e', gate, jax.nn.one_hot(top_idx, N_EXPERTS, dtype=jnp.float32))
    y = x2
    for e in range(N_EXPERTS):
        y = y + _ffn(x2, g, combine[:, e:e + 1], wg[e], wu[e], wd[e])
    return y.reshape(shp)


def kernel(x_prompt, x_sample, cache_nsa_cmp, cache_nsa_sel, cache_nsa_win, state_rwkv, state_rwkv_shift,
           state_ret, page_table, norm_attn, norm_ffn, w_in, w_out, nsa_q_norm, nsa_k_norm, nsa_cmp_pe,
           nsa_cmp_w1, nsa_cmp_w2, nsa_out_norm, rwkv_mu, rwkv_w0, rwkv_w_up, rwkv_a0, rwkv_a_up, rwkv_g_up,
           rwkv_k_k, rwkv_k_a, rwkv_r_k, rwkv_ln_g, rwkv_ln_b, ret_ln_g, ffn_w_gate, ffn_w_up, ffn_w_down,
           moe_router, moe_w_gate, moe_w_up, moe_w_down):
    Bp, Bs = x_prompt.shape[0], x_sample.shape[0]
    past_len = page_table.shape[1] * cache_nsa_cmp.shape[2]
    xp, xs = x_prompt, x_sample
    outs_p, outs_s = [], []
    for l in range(DEPTH):
        lp = {'norm_attn': norm_attn[l], 'w_in': w_in[l], 'w_out': w_out[l], 'nsa_q_norm': nsa_q_norm[l],
              'nsa_k_norm': nsa_k_norm[l], 'nsa_cmp_pe': nsa_cmp_pe[l], 'nsa_cmp_w1': nsa_cmp_w1[l],
              'nsa_cmp_w2': nsa_cmp_w2[l], 'nsa_out_norm': nsa_out_norm[l], 'rwkv_mu': rwkv_mu[l],
              'rwkv_w0': rwkv_w0[l], 'rwkv_w_up': rwkv_w_up[l], 'rwkv_a0': rwkv_a0[l], 'rwkv_a_up': rwkv_a_up[l],
              'rwkv_g_up': rwkv_g_up[l], 'rwkv_k_k': rwkv_k_k[l], 'rwkv_k_a': rwkv_k_a[l], 'rwkv_r_k': rwkv_r_k[l],
              'rwkv_ln_g': rwkv_ln_g[l], 'rwkv_ln_b': rwkv_ln_b[l], 'ret_ln_g': ret_ln_g[l]}
        xp, st = _mixing_sublayer(xp, lp, 0, None, None, None,
                                  jnp.zeros((Bp, RWKV_HEADS, HEAD_DIM, HEAD_DIM), jnp.float32),
                                  jnp.zeros((Bp, RWKV_IN), xp.dtype),
                                  jnp.zeros((Bp, RET_HEADS, HEAD_DIM, HEAD_DIM), jnp.float32))
        outs_p.append(st)
        past_cmp = cache_nsa_cmp[l][page_table].reshape(Bs, past_len, 2, NSA_KV_HEADS, HEAD_DIM)
        past_sel = cache_nsa_sel[l][page_table].reshape(Bs, past_len, 2, NSA_KV_HEADS, HEAD_DIM)
        xs, st = _mixing_sublayer(xs, lp, past_len, past_cmp, past_sel, cache_nsa_win[l], state_rwkv[l],
                                  state_rwkv_shift[l], state_ret[l])
        outs_s.append(st)
        i = l // 2
        if l % 2 == 0:
            xp = _dense_ffn(xp, norm_ffn[l], ffn_w_gate[i], ffn_w_up[i], ffn_w_down[i])
            xs = _dense_ffn(xs, norm_ffn[l], ffn_w_gate[i], ffn_w_up[i], ffn_w_down[i])
        else:
            xp = _moe_ffn(xp, norm_ffn[l], moe_router[i], moe_w_gate[i], moe_w_up[i], moe_w_down[i])
            xs = _moe_ffn(xs, norm_ffn[l], moe_router[i], moe_w_gate[i], moe_w_up[i], moe_w_down[i])
    kv_cmp_p, kv_sel_p, win_p, rwkv_p, shift_p, ret_p = [jnp.stack([o[j] for o in outs_p]) for j in range(6)]
    kv_cmp_s, kv_sel_s, win_s, rwkv_s, shift_s, ret_s = [jnp.stack([o[j] for o in outs_s]) for j in range(6)]
    return (xp, xs, kv_cmp_p, kv_sel_p, win_p, rwkv_p, shift_p, ret_p,
            kv_cmp_s, kv_sel_s, win_s, rwkv_s, shift_s, ret_s)
```

```python
import functools

import jax
import jax.numpy as jnp
from jax import lax
from jax.experimental import pallas as pl
from jax.experimental.pallas import tpu as pltpu

D_MODEL = 1024
DEPTH = 2
HEAD_DIM = 64
NSA_WIDTH = D_MODEL // 2
RWKV_WIDTH = D_MODEL // 4
RET_WIDTH = D_MODEL - NSA_WIDTH - RWKV_WIDTH
NSA_HEADS = NSA_WIDTH // HEAD_DIM
NSA_KV_HEADS = 2
NSA_GROUP = NSA_HEADS // NSA_KV_HEADS
CMP_LEN = 32
CMP_STRIDE = 16
SEL_LEN = 64
SEL_TOP = 16
WINDOW = 512
Q_BLOCK = 128
RWKV_HEADS = RWKV_WIDTH // HEAD_DIM
RWKV_W_RANK = 64
RWKV_A_RANK = 64
RWKV_G_RANK = 128
RWKV_GN_EPS = 64e-5
RET_HEADS = RET_WIDTH // HEAD_DIM
RET_CHUNK = 128
ROPE_BASE = 10000.0
N_EXPERTS = 8
TOP_K = 2
NSA_KV_COLS = NSA_KV_HEADS * HEAD_DIM
NSA_IN = NSA_WIDTH + 6 * NSA_KV_COLS + 3 * NSA_HEADS
RWKV_IN = 3 * RWKV_WIDTH + RWKV_W_RANK + RWKV_A_RANK + RWKV_G_RANK
RET_IN = 4 * RET_WIDTH
EPS = 1e-6
GN_EPS = 1e-5
NEG = -1e30
TINY = 1e-30
FORCE = 1e9

VMEM_LIMIT_BYTES = 56 * 1024 * 1024


def _pick_tile(n, target):
    t = min(n, target)
    while n % t:
        t //= 2
    return t


def _ffn_body(expert_ref, used_ref, x_ref, g_ref, s_ref, wg_ref, wu_ref, wd_ref, y_ref, h_scr, acc_scr):
    i = pl.program_id(0)
    j = pl.program_id(1)
    last = pl.num_programs(1) - 1
    used = used_ref[i] > 0

    @pl.when(used & (j == 0))
    def _():
        x = x_ref[...]
        h = x * lax.rsqrt(jnp.mean(x * x, axis=-1, keepdims=True) + EPS) * g_ref[...]
        h_scr[...] = h.astype(jnp.bfloat16)
        acc_scr[...] = jnp.zeros_like(acc_scr)

    @pl.when(used)
    def _():
        h = h_scr[...]
        a = jnp.dot(h, wg_ref[...].astype(jnp.bfloat16), preferred_element_type=jnp.float32)
        b = jnp.dot(h, wu_ref[...].astype(jnp.bfloat16), preferred_element_type=jnp.float32)
        z = (a * jax.nn.sigmoid(a)) * b
        acc_scr[...] += jnp.dot(z.astype(jnp.bfloat16), wd_ref[...].astype(jnp.bfloat16),
                                preferred_element_type=jnp.float32)

    @pl.when(used & (j == last))
    def _():
        y_ref[...] = acc_scr[...] * s_ref[...]

    @pl.when(jnp.logical_not(used) & (j == last))
    def _():
        y_ref[...] = jnp.zeros_like(y_ref)


def _ffn(x, g, scale, wg, wu, wd, tile_expert, tile_used, *, tm, tf=512):
    M, D = x.shape
    F = wg.shape[2]
    tf = _pick_tile(F, tf)
    grid_spec = pltpu.PrefetchScalarGridSpec(
        num_scalar_prefetch=2,
        grid=(M // tm, F // tf),
        in_specs=[
            pl.BlockSpec((tm, D), lambda i, j, e, u: (i, 0)),
            pl.BlockSpec((1, D), lambda i, j, e, u: (0, 0)),
            pl.BlockSpec((tm, 1), lambda i, j, e, u: (i, 0)),
            pl.BlockSpec((None, D, tf), lambda i, j, e, u: (e[i], 0, j)),
            pl.BlockSpec((None, D, tf), lambda i, j, e, u: (e[i], 0, j)),
            pl.BlockSpec((None, tf, D), lambda i, j, e, u: (e[i], j, 0)),
        ],
        out_specs=pl.BlockSpec((tm, D), lambda i, j, e, u: (i, 0)),
        scratch_shapes=[pltpu.VMEM((tm, D), jnp.bfloat16), pltpu.VMEM((tm, D), jnp.float32)],
    )
    return pl.pallas_call(
        _ffn_body,
        grid_spec=grid_spec,
        out_shape=jax.ShapeDtypeStruct((M, D), jnp.float32),
        compiler_params=pltpu.CompilerParams(
            dimension_semantics=("parallel", "arbitrary"), vmem_limit_bytes=VMEM_LIMIT_BYTES),
        name="ffn",
    )(tile_expert, tile_used, x, g.reshape(1, D), scale, wg, wu, wd)


def _router_body(x_ref, g_ref, w_ref, o_ref):
    x = x_ref[...]
    h = x * lax.rsqrt(jnp.mean(x * x, axis=-1, keepdims=True) + EPS) * g_ref[...]
    o_ref[...] = jnp.dot(h, w_ref[...], preferred_element_type=jnp.float32, precision=lax.Precision.HIGHEST)


def _router_logits(x, g, router, *, tm=512):
    M, D = x.shape
    E = router.shape[1]
    tm = _pick_tile(M, tm)
    lanes = 128
    w = jnp.pad(router, ((0, 0), (0, lanes - E)))
    out = pl.pallas_call(
        _router_body,
        grid=(M // tm,),
        in_specs=[pl.BlockSpec((tm, D), lambda i: (i, 0)), pl.BlockSpec((1, D), lambda i: (0, 0)),
                  pl.BlockSpec((D, lanes), lambda i: (0, 0))],
        out_specs=pl.BlockSpec((tm, lanes), lambda i: (i, 0)),
        out_shape=jax.ShapeDtypeStruct((M, lanes), jnp.float32),
        compiler_params=pltpu.CompilerParams(dimension_semantics=("parallel",), vmem_limit_bytes=VMEM_LIMIT_BYTES),
        name="router",
    )(x, g.reshape(1, D), w)
    return out[:, :E]


def _rwkv_scan_body(r_ref, w_ref, k_ref, v_ref, kk_ref, b_ref, s0_ref, y_ref, st_ref, s_scr, *, sub):
    j = pl.program_id(1)
    n_b, t_blk, _ = r_ref.shape
    d = HEAD_DIM

    @pl.when(j == 0)
    def _():
        s_scr[...] = s0_ref[...]

    eye = lax.broadcasted_iota(jnp.int32, (d, d), 0) == lax.broadcasted_iota(jnp.int32, (d, d), 1)

    def sub_block(i, carry):
        t0 = pl.multiple_of(i * sub, sub)
        for bb in range(n_b):
            blk = [ref[bb, pl.ds(t0, sub), :] for ref in (r_ref, w_ref, k_ref, v_ref, kk_ref, b_ref)]
            for h in range(RWKV_HEADS):
                cols = slice(h * d, (h + 1) * d)
                S = s_scr[bb, h]
                y_rows = []
                for t in range(sub):
                    r_t, w_t, k_t, v_t, kk_t, b_t = [x[t:t + 1, cols] for x in blk]
                    sa = jnp.sum(S * kk_t, axis=1, keepdims=True)
                    v_col = jnp.sum(jnp.where(eye, v_t, 0.0), axis=1, keepdims=True)
                    S = S * w_t - sa * b_t + v_col * k_t
                    y_col = jnp.sum(S * r_t, axis=1, keepdims=True)
                    y_rows.append(jnp.sum(jnp.where(eye, y_col, 0.0), axis=0, keepdims=True))
                s_scr[bb, h] = S
                y_ref[bb, pl.ds(t0, sub), cols] = jnp.concatenate(y_rows, axis=0)
        return carry

    lax.fori_loop(0, t_blk // sub, sub_block, 0)

    @pl.when(j == pl.num_programs(1) - 1)
    def _():
        st_ref[...] = s_scr[...]


def _rwkv_scan(r, w, k, v, kk, b, s0, *, n_b=2, t_blk=256):
    B, T, W = r.shape
    n_b = _pick_tile(B, n_b)
    t_blk = _pick_tile(T, t_blk)
    sub = 8 if t_blk % 8 == 0 else t_blk
    seq = pl.BlockSpec((n_b, t_blk, W), lambda i, j: (i, j, 0))
    st = pl.BlockSpec((n_b, RWKV_HEADS, HEAD_DIM, HEAD_DIM), lambda i, j: (i, 0, 0, 0))
    return pl.pallas_call(
        functools.partial(_rwkv_scan_body, sub=sub),
        grid=(B // n_b, T // t_blk),
        in_specs=[seq] * 6 + [st],
        out_specs=[seq, st],
        out_shape=[jax.ShapeDtypeStruct((B, T, W), jnp.float32),
                   jax.ShapeDtypeStruct((B, RWKV_HEADS, HEAD_DIM, HEAD_DIM), jnp.float32)],
        scratch_shapes=[pltpu.VMEM((n_b, RWKV_HEADS, HEAD_DIM, HEAD_DIM), jnp.float32)],
        compiler_params=pltpu.CompilerParams(
            dimension_semantics=("parallel", "arbitrary"), vmem_limit_bytes=VMEM_LIMIT_BYTES),
        name="rwkv_scan",
    )(r, w, k, v, kk, b, s0)


RWKV_CHUNK = 64


_F32_DOT = dict(preferred_element_type=jnp.float32, precision=lax.Precision.HIGHEST)


def _dot(a, b):
    return jnp.dot(a, b, **_F32_DOT)


def _dot_t(a, b):
    return lax.dot_general(a, b, (((1,), (1,)), ((), ())), **_F32_DOT)


def _dot_0(a, b):
    return lax.dot_general(a, b, (((0,), (0,)), ((), ())), **_F32_DOT)


def _rwkv_chunk_body(r_ref, lw_ref, k_ref, v_ref, kk_ref, b_ref, qe_ref, y0_ref, pm_ref, z_ref):
    L, d = RWKV_CHUNK, HEAD_DIM
    row = lax.broadcasted_iota(jnp.int32, (L, L), 0)
    col = lax.broadcasted_iota(jnp.int32, (L, L), 1)
    lower = row >= col
    strict = row > col
    ones_lower = jnp.where(lower, 1.0, 0.0)
    eye = jnp.where(row == col, 1.0, 0.0)
    for h in range(RWKV_HEADS):
        cols = slice(h * d, (h + 1) * d)
        r, lw, k, v, kk, b = [ref[0, :, cols] for ref in (r_ref, lw_ref, k_ref, v_ref, kk_ref, b_ref)]
        G = _dot(ones_lower, lw)
        g_inv = jnp.exp(-G)
        kap, bt, kt, rt = kk * jnp.exp(G - lw), b * g_inv, k * g_inv, r * jnp.exp(G)
        N = jnp.where(strict, _dot_t(kap, bt), 0.0)
        Mk = jnp.where(strict, _dot_t(kap, kt), 0.0)
        RB = jnp.where(lower, _dot_t(rt, bt), 0.0)
        RK = jnp.where(lower, _dot_t(rt, kt), 0.0)
        X, P = eye - N, N
        for _ in range(L.bit_length() - 2):
            P = _dot(P, P)
            X = X + _dot(X, P)
        A = _dot(X, kap)
        C = _dot(X, _dot(Mk, v))
        qe_ref[0, h] = rt - _dot(RB, A)
        y0_ref[0, h] = _dot(RK, v) - _dot(RB, C)
        g_end = jnp.exp(G[L - 1:L, :])
        pm_ref[0, h, 0] = (eye - _dot_0(A, bt)) * g_end
        z_ref[0, h, 0] = (_dot_0(v, kt) - _dot_0(C, bt)) * g_end


def _rwkv_walk_body(qe_ref, y0_ref, pm_ref, z_ref, s0_ref, y_ref, st_ref, s_scr):
    j = pl.program_id(0)
    B, H, n_c = pm_ref.shape[:3]
    L = RWKV_CHUNK

    @pl.when(j == 0)
    def _():
        s_scr[...] = s0_ref[...]

    def one_chunk(c, carry):
        t0 = pl.multiple_of(c * L, L)
        for bb in range(B):
            for h in range(H):
                S = s_scr[bb, h]
                y_ref[bb, h, pl.ds(t0, L), :] = _dot_t(qe_ref[bb, h, pl.ds(t0, L), :], S) + y0_ref[bb, h, pl.ds(t0, L), :]
                s_scr[bb, h] = _dot(S, pm_ref[bb, h, c]) + z_ref[bb, h, c]
        return carry

    lax.fori_loop(0, n_c, one_chunk, 0)

    @pl.when(j == pl.num_programs(0) - 1)
    def _():
        st_ref[...] = s_scr[...]


def _rwkv_chunked(r, lw, k, v, kk, b, s0, *, chunks_per_step=16):
    B, T, W = r.shape
    H, d, L = RWKV_HEADS, HEAD_DIM, RWKV_CHUNK
    n_c = T // L
    seq = pl.BlockSpec((1, L, W), lambda i, c: (i, c, 0))
    per_tok = pl.BlockSpec((1, H, L, d), lambda i, c: (i, 0, c, 0))
    per_chunk = pl.BlockSpec((1, H, 1, d, d), lambda i, c: (i, 0, c, 0, 0))
    qe, y0, pm, z = pl.pallas_call(
        _rwkv_chunk_body,
        grid=(B, n_c),
        in_specs=[seq] * 6,
        out_specs=[per_tok, per_tok, per_chunk, per_chunk],
        out_shape=[jax.ShapeDtypeStruct((B, H, T, d), jnp.float32)] * 2
                  + [jax.ShapeDtypeStruct((B, H, n_c, d, d), jnp.float32)] * 2,
        compiler_params=pltpu.CompilerParams(
            dimension_semantics=("parallel", "parallel"), vmem_limit_bytes=VMEM_LIMIT_BYTES),
        name="rwkv_chunk",
    )(r, lw, k, v, kk, b)
    cs = _pick_tile(n_c, chunks_per_step)
    tok = pl.BlockSpec((B, H, cs * L, d), lambda j: (0, 0, j, 0))
    chk = pl.BlockSpec((B, H, cs, d, d), lambda j: (0, 0, j, 0, 0))
    st = pl.BlockSpec((B, H, d, d), lambda j: (0, 0, 0, 0))
    y, s_t = pl.pallas_call(
        _rwkv_walk_body,
        grid=(n_c // cs,),
        in_specs=[tok, tok, chk, chk, st],
        out_specs=[tok, st],
        out_shape=[jax.ShapeDtypeStruct((B, H, T, d), jnp.float32), jax.ShapeDtypeStruct((B, H, d, d), jnp.float32)],
        scratch_shapes=[pltpu.VMEM((B, H, d, d), jnp.float32)],
        compiler_params=pltpu.CompilerParams(
            dimension_semantics=("arbitrary",), vmem_limit_bytes=VMEM_LIMIT_BYTES),
        name="rwkv_walk",
    )(qe, y0, pm, z, s0)
    return jnp.transpose(y, (0, 2, 1, 3)).reshape(B, T, W), s_t


SEL_CHUNK = 512
WIN_SPAN = WINDOW + Q_BLOCK


def _softmax_cols(s, valid):
    s = jnp.where(valid, s, NEG)
    m = jnp.max(s, axis=0, keepdims=True)
    e = jnp.where(valid, jnp.exp(s - m), 0.0)
    return e / jnp.maximum(jnp.sum(e, axis=0, keepdims=True), TINY)


def _nsa_prompt_body(qT_ref, kc_ref, vcT_ref, ks_ref, vsT_ref, kw_ref, vwT_ref, covT_ref, o_ref,
                     sel_scr, m_scr, l_scr, acc_scr):
    g = pl.program_id(1)
    i = pl.program_id(2)
    R = NSA_GROUP
    n_cmp = kc_ref.shape[2]
    n_sel = covT_ref.shape[0]
    q0 = i * Q_BLOCK
    pos = q0 + lax.broadcasted_iota(jnp.int32, (1, Q_BLOCK), 1)
    posf = pos.astype(jnp.float32)
    slopes = [jnp.where(g == 0, 2.0 ** -(r + 1), 2.0 ** -(R + r + 1)).astype(jnp.float32) for r in range(R)]

    c_end = (lax.broadcasted_iota(jnp.int32, (n_cmp, Q_BLOCK), 0) * CMP_STRIDE + (CMP_LEN - 1)).astype(jnp.float32)
    d_cmp = posf - c_end
    valid_c = d_cmp >= 0.0
    kc = kc_ref[0, 0]
    vcT = vcT_ref[0, 0]
    p_sum = jnp.zeros((n_cmp, Q_BLOCK), jnp.float32)
    for r in range(R):
        s = jnp.dot(kc, qT_ref[0, 0, r], preferred_element_type=jnp.float32) - slopes[r] * d_cmp
        p = _softmax_cols(s, valid_c)
        p_sum = p_sum + p
        o_ref[0, 0, 0, r] = jnp.dot(vcT, p.astype(jnp.bfloat16), preferred_element_type=jnp.float32)
    imp = jnp.dot(covT_ref[...], p_sum, preferred_element_type=jnp.float32, precision=lax.Precision.HIGHEST)
    blk = lax.broadcasted_iota(jnp.int32, (n_sel, Q_BLOCK), 0)
    avail = blk * SEL_LEN <= pos
    forced = (blk == jnp.right_shift(pos, 6)) | (blk == 0)
    imp = jnp.where(avail, jnp.where(forced, FORCE, imp), -FORCE)

    sub8 = lax.broadcasted_iota(jnp.int32, (8, Q_BLOCK), 0)
    for jj in range(n_sel // 8):
        vj = imp[8 * jj:8 * jj + 8]
        cnt = jnp.zeros((8, Q_BLOCK), jnp.float32)
        for k in range(n_sel):
            row = imp[k:k + 1]
            if k < 8 * jj:
                cnt = cnt + jnp.where(row >= vj, 1.0, 0.0)
            elif k >= 8 * jj + 8:
                cnt = cnt + jnp.where(row > vj, 1.0, 0.0)
            else:
                cnt = cnt + jnp.where(sub8 > (k - 8 * jj), jnp.where(row >= vj, 1.0, 0.0),
                                      jnp.where(row > vj, 1.0, 0.0))
        sel_scr[8 * jj:8 * jj + 8, :] = jnp.where(cnt < float(SEL_TOP), 1.0, 0.0)

    m_scr[...] = jnp.full_like(m_scr, NEG)
    l_scr[...] = jnp.zeros_like(l_scr)
    acc_scr[...] = jnp.zeros_like(acc_scr)
    key_iota = lax.broadcasted_iota(jnp.int32, (SEL_CHUNK, Q_BLOCK), 0).astype(jnp.float32)
    blocks_per_chunk = SEL_CHUNK // SEL_LEN

    def chunk(c, carry):
        k0 = pl.multiple_of(c * SEL_CHUNK, SEL_CHUNK)
        dist = (posf - k0.astype(jnp.float32)) - key_iota
        picked = jnp.concatenate(
            [jnp.broadcast_to(sel_scr[pl.ds(c * blocks_per_chunk + b, 1), :], (SEL_LEN, Q_BLOCK))
             for b in range(blocks_per_chunk)], axis=0)
        neg = jnp.where((picked > 0.5) & (dist >= 0.0), 0.0, NEG)
        ks = ks_ref[0, 0, pl.ds(k0, SEL_CHUNK), :]
        vsT = vsT_ref[0, 0, :, pl.ds(k0, SEL_CHUNK)]
        for r in range(R):
            s = jnp.dot(ks, qT_ref[0, 0, r], preferred_element_type=jnp.float32) - slopes[r] * dist + neg
            m_old = m_scr[r]
            m_new = jnp.maximum(m_old, jnp.max(s, axis=0, keepdims=True))
            p = jnp.exp(s - m_new)
            alpha = jnp.exp(m_old - m_new)
            l_scr[r] = alpha * l_scr[r] + jnp.sum(p, axis=0, keepdims=True)
            acc_scr[r] = alpha * acc_scr[r] + jnp.dot(vsT, p.astype(jnp.bfloat16), preferred_element_type=jnp.float32)
            m_scr[r] = m_new
        return carry

    lax.fori_loop(0, q0 // SEL_CHUNK + 1, chunk, 0)
    for r in range(R):
        o_ref[1, 0, 0, r] = acc_scr[r] / jnp.maximum(l_scr[r], TINY)

    k0w = pl.multiple_of(jnp.maximum(q0 - WINDOW, 0), Q_BLOCK)
    d_win = (posf - k0w.astype(jnp.float32)) - lax.broadcasted_iota(jnp.int32, (WIN_SPAN, Q_BLOCK), 0).astype(jnp.float32)
    valid_w = (d_win >= 0.0) & (d_win < float(WINDOW))
    kw = kw_ref[0, 0, pl.ds(k0w, WIN_SPAN), :]
    vwT = vwT_ref[0, 0, :, pl.ds(k0w, WIN_SPAN)]
    for r in range(R):
        s = jnp.dot(kw, qT_ref[0, 0, r], preferred_element_type=jnp.float32) - slopes[r] * d_win
        p = _softmax_cols(s, valid_w)
        o_ref[2, 0, 0, r] = jnp.dot(vwT, p.astype(jnp.bfloat16), preferred_element_type=jnp.float32)


def _nsa_prompt_attn(q, kc, vc, ks, vs, kw, vw):
    B, T, H, d = q.shape
    G, R = NSA_KV_HEADS, NSA_GROUP
    n_c = kc.shape[1]
    n_cmp = -(-n_c // 128) * 128
    n_sel = T // SEL_LEN
    bf = jnp.bfloat16
    qT = jnp.transpose((q * (d ** -0.5)).reshape(B, T, G, R, d), (0, 2, 3, 4, 1)).astype(bf)
    rows = lambda t: jnp.transpose(t, (0, 2, 1, 3)).astype(bf)
    cols = lambda t: jnp.transpose(t, (0, 2, 3, 1)).astype(bf)
    pad_c = ((0, 0), (0, n_cmp - n_c), (0, 0), (0, 0))
    kc_r, vc_c = rows(jnp.pad(kc, pad_c)), cols(jnp.pad(vc, pad_c))
    c_start = jnp.arange(n_cmp, dtype=jnp.int32) * CMP_STRIDE
    s_start = jnp.arange(n_sel, dtype=jnp.int32) * SEL_LEN
    covT = jnp.maximum(jnp.minimum(c_start[None, :] + CMP_LEN, s_start[:, None] + SEL_LEN)
                       - jnp.maximum(c_start[None, :], s_start[:, None]), 0).astype(jnp.float32) / CMP_LEN
    full_r = lambda n: pl.BlockSpec((1, 1, n, d), lambda b, g, i: (b, g, 0, 0))
    full_c = lambda n: pl.BlockSpec((1, 1, d, n), lambda b, g, i: (b, g, 0, 0))
    oT = pl.pallas_call(
        _nsa_prompt_body,
        grid=(B, G, T // Q_BLOCK),
        in_specs=[pl.BlockSpec((1, 1, R, d, Q_BLOCK), lambda b, g, i: (b, g, 0, 0, i)),
                  full_r(n_cmp), full_c(n_cmp), full_r(T), full_c(T), full_r(T), full_c(T),
                  pl.BlockSpec((n_sel, n_cmp), lambda b, g, i: (0, 0))],
        out_specs=pl.BlockSpec((3, 1, 1, R, d, Q_BLOCK), lambda b, g, i: (0, b, g, 0, 0, i)),
        out_shape=jax.ShapeDtypeStruct((3, B, G, R, d, T), jnp.float32),
        scratch_shapes=[pltpu.VMEM((n_sel, Q_BLOCK), jnp.float32),
                        pltpu.VMEM((R, 1, Q_BLOCK), jnp.float32),
                        pltpu.VMEM((R, 1, Q_BLOCK), jnp.float32),
                        pltpu.VMEM((R, d, Q_BLOCK), jnp.float32)],
        compiler_params=pltpu.CompilerParams(
            dimension_semantics=("parallel", "parallel", "arbitrary"), vmem_limit_bytes=VMEM_LIMIT_BYTES),
        name="nsa_prompt",
    )(qT, kc_r, vc_c, rows(ks), cols(vs), rows(kw), cols(vw), covT)
    return jnp.transpose(oT, (0, 1, 5, 2, 3, 4)).reshape(3, B, T, H, d)


def _rms_norm(x, g):
    xf = x.astype(jnp.float32)
    y = xf * lax.rsqrt(jnp.mean(xf * xf, axis=-1, keepdims=True) + EPS)
    return (y * g.astype(jnp.float32)).astype(x.dtype)


def _group_norm(x, g, eps):
    xf = x.astype(jnp.float32)
    mu = jnp.mean(xf, axis=-1, keepdims=True)
    var = jnp.mean(jnp.square(xf - mu), axis=-1, keepdims=True)
    return (xf - mu) * lax.rsqrt(var + eps) * g.astype(jnp.float32)


def _masked_softmax(s, mask):
    s = jnp.where(mask, s, NEG)
    m = jnp.max(s, axis=-1, keepdims=True)
    e = jnp.where(mask, jnp.exp(s - m), 0.0)
    return e / jnp.maximum(jnp.sum(e, axis=-1, keepdims=True), TINY)


def _alibi_slopes(n):
    return 2.0 ** (-8.0 * jnp.arange(1, n + 1, dtype=jnp.float32) / n)


def _rotary(x, pos):
    half = x.shape[-1] // 2
    freqs = ROPE_BASE ** (-jnp.arange(half, dtype=jnp.float32) / half)
    ang = pos.astype(jnp.float32)[:, None] * freqs[None, :]
    cos, sin = jnp.cos(ang)[None, :, None, :], jnp.sin(ang)[None, :, None, :]
    x1, x2 = x[..., :half], x[..., half:]
    return jnp.concatenate([x1 * cos - x2 * sin, x1 * sin + x2 * cos], axis=-1)


def _nsa_compress(rows, pe, w1, w2):
    B, T, G, d = rows.shape
    n_cmp = (T - CMP_LEN) // CMP_STRIDE + 1
    idx = (jnp.arange(n_cmp, dtype=jnp.int32) * CMP_STRIDE)[:, None] + jnp.arange(CMP_LEN, dtype=jnp.int32)[None, :]
    blk = rows[:, idx] + pe[None, None, :, None, :]
    blk = jnp.transpose(blk, (0, 1, 3, 2, 4)).reshape(B, n_cmp, G, CMP_LEN * d)
    return jax.nn.gelu(blk @ w1) @ w2


def _to_sel_blocks(rows):
    B, T, G, d = rows.shape
    n_sel = -(-T // SEL_LEN)
    rows = jnp.pad(rows, ((0, 0), (0, n_sel * SEL_LEN - T), (0, 0), (0, 0)))
    return jnp.transpose(rows.reshape(B, n_sel, SEL_LEN, G, d), (0, 3, 1, 2, 4))


def _nsa_branches(q, pos, kc, vc, ks_blk, vs_blk, kw, vw, pos_w, slopes):
    B, Tq, H, d = q.shape
    G, R = NSA_KV_HEADS, NSA_GROUP
    scale = d ** -0.5
    qg = q.reshape(B, Tq, G, R, d)
    sl = slopes.reshape(G, R)
    posf = pos.astype(jnp.float32)
    n_cmp = kc.shape[1]
    c_start = jnp.arange(n_cmp, dtype=jnp.int32) * CMP_STRIDE
    d_cmp = posf[:, None] - (c_start + CMP_LEN - 1).astype(jnp.float32)[None, :]
    s = jnp.einsum('btgrd,bngd->bgrtn', qg, kc).astype(jnp.float32) * scale - sl[None, :, :, None, None] * d_cmp
    p_cmp = _masked_softmax(s, (d_cmp >= 0.0)[None, None, None])
    o_cmp = jnp.einsum('bgrtn,bngd->btgrd', p_cmp.astype(vc.dtype), vc)
    n_sel = ks_blk.shape[2]
    s_start = jnp.arange(n_sel, dtype=jnp.int32) * SEL_LEN
    cover = jnp.maximum(jnp.minimum(c_start[:, None] + CMP_LEN, s_start[None, :] + SEL_LEN)
                        - jnp.maximum(c_start[:, None], s_start[None, :]), 0).astype(jnp.float32) / CMP_LEN
    imp = jnp.einsum('bgrtn,nj->bgtj', p_cmp, cover)
    blk = jnp.arange(n_sel, dtype=jnp.int32)
    avail = s_start[None, :] <= pos[:, None]
    forced = (blk[None, :] == (pos // SEL_LEN)[:, None]) | (blk[None, :] == 0)
    imp = jnp.where(avail, jnp.where(forced, FORCE, imp), -FORCE)
    n_top = min(SEL_TOP, n_sel)
    _, idx = lax.top_k(imp, n_top)
    b_i = jnp.arange(B)[:, None, None, None]
    g_i = jnp.arange(G)[None, :, None, None]
    ks_g = ks_blk[b_i, g_i, idx]
    vs_g = vs_blk[b_i, g_i, idx]
    kpos = idx[..., None] * SEL_LEN + jnp.arange(SEL_LEN, dtype=jnp.int32)
    d_sel = (pos[None, None, :, None, None] - kpos).astype(jnp.float32)[:, :, None]
    s = jnp.einsum('btgrd,bgtnsd->bgrtns', qg, ks_g).astype(jnp.float32) * scale - sl[None, :, :, None, None, None] * d_sel
    mask = jnp.broadcast_to(d_sel >= 0.0, s.shape)
    p_sel = _masked_softmax(s.reshape(B, G, R, Tq, -1), mask.reshape(B, G, R, Tq, -1)).reshape(s.shape)
    o_sel = jnp.einsum('bgrtns,bgtnsd->btgrd', p_sel.astype(vs_g.dtype), vs_g)
    d_win = posf[:, None] - pos_w.astype(jnp.float32)[None, :]
    s = jnp.einsum('btgrd,bwgd->bgrtw', qg, kw).astype(jnp.float32) * scale - sl[None, :, :, None, None] * d_win
    win_mask = (d_win >= 0.0) & (d_win < WINDOW) & (pos_w >= 0)[None, :]
    p_win = _masked_softmax(s, win_mask[None, None, None])
    o_win = jnp.einsum('bgrtw,bwgd->btgrd', p_win.astype(vw.dtype), vw)
    return jnp.stack([o_cmp, o_sel, o_win]).reshape(3, B, Tq, H, d)


def _rwkv_group(u, shift0, S0, lp):
    B, T, _ = u.shape
    W = RWKV_WIDTH
    uf = u.astype(jnp.float32)
    prev = jnp.concatenate([shift0.astype(jnp.float32)[:, None], uf[:, :-1]], axis=1)
    um = uf + (prev - uf) * lp['rwkv_mu']
    r, k, v = um[..., :W], um[..., W:2 * W], um[..., 2 * W:3 * W]
    o = 3 * W
    wd = um[..., o:o + RWKV_W_RANK]
    ad = um[..., o + RWKV_W_RANK:o + RWKV_W_RANK + RWKV_A_RANK]
    gd = um[..., o + RWKV_W_RANK + RWKV_A_RANK:]
    w = lp['rwkv_w0'] + jnp.tanh(wd) @ lp['rwkv_w_up']
    log_decay = -jnp.exp(-jax.nn.softplus(-w) - 0.5)
    a = jax.nn.sigmoid(lp['rwkv_a0'] + ad @ lp['rwkv_a_up'])
    g = jax.nn.sigmoid(gd) @ lp['rwkv_g_up']
    kk = k * lp['rwkv_k_k']
    k = k * (1.0 + (a - 1.0) * lp['rwkv_k_a'])
    hd = lambda t: t.reshape(B, T, RWKV_HEADS, HEAD_DIM).astype(jnp.float32)
    r, k, v, a, g, kk = hd(r), hd(k), hd(v), hd(a), hd(g), hd(kk)
    kk = kk / jnp.maximum(jnp.sqrt(jnp.sum(kk * kk, axis=-1, keepdims=True)), 1e-12)
    flat = lambda t: t.reshape(B, T, W)
    if T % RWKV_CHUNK == 0:
        ys, S_T = _rwkv_chunked(flat(r), log_decay, flat(k), flat(v), flat(kk), flat(kk * a), S0.astype(jnp.float32))
    else:
        ys, S_T = _rwkv_scan(flat(r), jnp.exp(log_decay), flat(k), flat(v), flat(kk), flat(kk * a),
                             S0.astype(jnp.float32))
    y = _group_norm(ys.reshape(B, T, RWKV_HEADS, HEAD_DIM), lp['rwkv_ln_g'], RWKV_GN_EPS) + lp['rwkv_ln_b']
    y = y + jnp.sum(r * k * lp['rwkv_r_k'], axis=-1, keepdims=True) * v
    y = y * g
    return y.reshape(B, T, W).astype(u.dtype), u[:, -1], S_T


def _retention_group(u, pos, S0, ln_g):
    B, T, _ = u.shape
    uf = u.astype(jnp.float32)
    q, k, v, g = [t.reshape(B, T, RET_HEADS, HEAD_DIM) for t in jnp.split(uf, 4, axis=-1)]
    q = _rotary(q, pos)
    k = _rotary(k, pos) * HEAD_DIM ** -0.5
    lg = jnp.log(1.0 - 2.0 ** (-5.0 - jnp.arange(RET_HEADS, dtype=jnp.float32)))
    C = RET_CHUNK if T % RET_CHUNK == 0 else T
    nC = T // C
    n = jnp.arange(C, dtype=jnp.float32)
    diff = n[:, None] - n[None, :]
    dmask = jnp.where(diff[None] >= 0, jnp.exp(jnp.maximum(diff, 0.0)[None] * lg[:, None, None]), 0.0)
    q_dec = jnp.exp((n[:, None] + 1.0) * lg[None, :])
    k_dec = jnp.exp((C - 1.0 - n)[:, None] * lg[None, :])
    s_dec = jnp.exp(C * lg)

    def chunk(S, inp):
        qc, kc, vc = inp
        att = jnp.einsum('bnhd,bmhd->bhnm', qc, kc) * dmask
        out = jnp.einsum('bhnm,bmhe->bnhe', att, vc) + jnp.einsum('bnhd,bhde->bnhe', qc, S) * q_dec[None, :, :, None]
        S = S * s_dec[None, :, None, None] + jnp.einsum('bmhd,bmhe->bhde', kc * k_dec[None, :, :, None], vc)
        return S, out

    to_chunks = lambda t: jnp.moveaxis(t.reshape(B, nC, C, RET_HEADS, HEAD_DIM), 1, 0)
    S_T, o = lax.scan(chunk, S0.astype(jnp.float32), (to_chunks(q), to_chunks(k), to_chunks(v)))
    o = jnp.moveaxis(o, 0, 1).reshape(B, T, RET_HEADS, HEAD_DIM)
    y = jax.nn.silu(g) * _group_norm(o, ln_g, GN_EPS)
    return y.reshape(B, T, RET_WIDTH).astype(u.dtype), S_T


def _mixing_sublayer(x, lp, past_len, past_cmp, past_sel, win_buf, rwkv_S, rwkv_shift, ret_S):
    B, T, _ = x.shape
    h = _rms_norm(x, lp['norm_attn'])
    P = h @ lp['w_in']
    c = P[..., :NSA_IN]
    q = _rms_norm(c[..., :NSA_WIDTH].reshape(B, T, NSA_HEADS, HEAD_DIM), lp['nsa_q_norm'])
    kv = c[..., NSA_WIDTH:NSA_WIDTH + 6 * NSA_KV_COLS].reshape(B, T, 3, 2, NSA_KV_HEADS, HEAD_DIM)
    kv_cmp, kv_sel, kv_win = kv[:, :, 0], kv[:, :, 1], kv[:, :, 2]
    gates = jax.nn.sigmoid(c[..., NSA_WIDTH + 6 * NSA_KV_COLS:].astype(jnp.float32)).reshape(B, T, NSA_HEADS, 3)
    pos = past_len + jnp.arange(T, dtype=jnp.int32)
    slopes = _alibi_slopes(NSA_HEADS)
    prompt = past_cmp is None
    if prompt:
        rows_cmp, rows_sel, rows_win = kv_cmp, kv_sel, kv_win
        new_win = kv_win[:, T - min(WINDOW, T):]
    else:
        rows_cmp = jnp.concatenate([past_cmp, kv_cmp.astype(past_cmp.dtype)], axis=1)
        rows_sel = jnp.concatenate([past_sel, kv_sel.astype(past_sel.dtype)], axis=1)
        rows_win = jnp.concatenate([win_buf, kv_win.astype(win_buf.dtype)], axis=1)
        new_win = rows_win[:, T:]
    k_norm = lp['nsa_k_norm']
    kc = _rms_norm(_nsa_compress(rows_cmp[:, :, 0], lp['nsa_cmp_pe'][0], lp['nsa_cmp_w1'][0], lp['nsa_cmp_w2'][0]), k_norm[0])
    vc = _nsa_compress(rows_cmp[:, :, 1], lp['nsa_cmp_pe'][1], lp['nsa_cmp_w1'][1], lp['nsa_cmp_w2'][1])
    kw = _rms_norm(rows_win[:, :, 0], k_norm[2])
    vw = rows_win[:, :, 1]
    if prompt:
        o3 = _nsa_prompt_attn(q, kc, vc, _rms_norm(rows_sel[:, :, 0], k_norm[1]), rows_sel[:, :, 1], kw, vw)
    else:
        ks_blk = _to_sel_blocks(_rms_norm(rows_sel[:, :, 0], k_norm[1]))
        vs_blk = _to_sel_blocks(rows_sel[:, :, 1])
        wb = win_buf.shape[1]
        pos_w = past_len - wb + jnp.arange(wb + T, dtype=jnp.int32)
        o3 = _nsa_branches(q, pos, kc, vc, ks_blk, vs_blk, kw, vw, pos_w, slopes)
    o_nsa = jnp.einsum('btha,abthd->bthd', gates.astype(o3.dtype), o3)
    o_nsa = _rms_norm(o_nsa, lp['nsa_out_norm']).reshape(B, T, NSA_WIDTH)
    y_rwkv, new_shift, new_rwkv = _rwkv_group(P[..., NSA_IN:NSA_IN + RWKV_IN], rwkv_shift, rwkv_S, lp)
    y_ret, new_ret = _retention_group(P[..., NSA_IN + RWKV_IN:], pos, ret_S, lp['ret_ln_g'])
    mix = jnp.concatenate([o_nsa, y_rwkv.astype(o_nsa.dtype), y_ret.astype(o_nsa.dtype)], axis=-1)
    x = x + mix @ lp['w_out']
    return x, (kv_cmp, kv_sel, new_win, new_rwkv, new_shift, new_ret)


FFN_ROW_TILE = 512


def _dense_ffn(x, g, wg, wu, wd):
    shp = x.shape
    x2 = x.reshape(-1, shp[-1])
    tm = _pick_tile(x2.shape[0], FFN_ROW_TILE)
    n_tiles = x2.shape[0] // tm
    ones = jnp.ones((x2.shape[0], 1), jnp.float32)
    y = _ffn(x2, g, ones, wg[None], wu[None], wd[None], jnp.zeros((n_tiles,), jnp.int32),
             jnp.ones((n_tiles,), jnp.int32), tm=tm)
    return (x2 + y).reshape(shp)


def _moe_ffn(xs, g, router, wg, wu, wd):
    D = xs[0].shape[-1]
    E, tm = router.shape[1], FFN_ROW_TILE
    flat = [x.reshape(-1, D) for x in xs]
    x2 = jnp.concatenate(flat, axis=0)
    logits = jnp.concatenate([_router_logits(x, g, router) for x in flat], axis=0)
    top_val, top_idx = lax.top_k(logits, TOP_K)
    gate = jax.nn.softmax(top_val, axis=-1)
    N = x2.shape[0]
    A = N * TOP_K
    flat_e, flat_w = top_idx.reshape(A).astype(jnp.int32), gate.reshape(A)
    order = jnp.argsort(flat_e, stable=True).astype(jnp.int32)
    counts = jnp.sum(jax.nn.one_hot(flat_e, E, dtype=jnp.int32), axis=0)
    padded = (counts + tm - 1) // tm * tm
    start, p_end = jnp.cumsum(counts) - counts, jnp.cumsum(padded)
    p_start = p_end - padded
    sorted_e = flat_e[order]
    dest = p_start[sorted_e] + (jnp.arange(A, dtype=jnp.int32) - start[sorted_e])
    P = -(-(A + E * (tm - 1)) // tm) * tm
    row_tok = jnp.zeros((P,), jnp.int32).at[dest].set(order // TOP_K)
    row_w = jnp.zeros((P,), jnp.float32).at[dest].set(flat_w[order])
    tile_start = jnp.arange(P // tm, dtype=jnp.int32) * tm
    tile_e = jnp.minimum(jnp.searchsorted(p_end, tile_start, side='right'), E - 1).astype(jnp.int32)
    tile_used = (tile_start < p_end[-1]).astype(jnp.int32)
    ys = _ffn(x2[row_tok], g, row_w[:, None], wg, wu, wd, tile_e, tile_used, tm=tm)
    slot = jnp.zeros((A,), jnp.int32).at[order].set(dest).reshape(N, TOP_K)
    y = x2 + ys[slot[:, 0]] + ys[slot[:, 1]]
    outs, off = [], 0
    for x in xs:
        n = x.size // D
        outs.append(y[off:off + n].reshape(x.shape))
        off += n
    return outs


def kernel(x_prompt, x_sample, cache_nsa_cmp, cache_nsa_sel, cache_nsa_win, state_rwkv, state_rwkv_shift,
           state_ret, page_table, norm_attn, norm_ffn, w_in, w_out, nsa_q_norm, nsa_k_norm, nsa_cmp_pe,
           nsa_cmp_w1, nsa_cmp_w2, nsa_out_norm, rwkv_mu, rwkv_w0, rwkv_w_up, rwkv_a0, rwkv_a_up, rwkv_g_up,
           rwkv_k_k, rwkv_k_a, rwkv_r_k, rwkv_ln_g, rwkv_ln_b, ret_ln_g, ffn_w_gate, ffn_w_up, ffn_w_down,
           moe_router, moe_w_gate, moe_w_up, moe_w_down):
    Bp, Bs = x_prompt.shape[0], x_sample.shape[0]
    past_len = page_table.shape[1] * cache_nsa_cmp.shape[2]
    xp, xs = x_prompt, x_sample
    outs_p, outs_s = [], []
    for l in range(DEPTH):
        lp = {'norm_attn': norm_attn[l], 'w_in': w_in[l], 'w_out': w_out[l], 'nsa_q_norm': nsa_q_norm[l],
              'nsa_k_norm': nsa_k_norm[l], 'nsa_cmp_pe': nsa_cmp_pe[l], 'nsa_cmp_w1': nsa_cmp_w1[l],
              'nsa_cmp_w2': nsa_cmp_w2[l], 'nsa_out_norm': nsa_out_norm[l], 'rwkv_mu': rwkv_mu[l],
              'rwkv_w0': rwkv_w0[l], 'rwkv_w_up': rwkv_w_up[l], 'rwkv_a0': rwkv_a0[l], 'rwkv_a_up': rwkv_a_up[l],
              'rwkv_g_up': rwkv_g_up[l], 'rwkv_k_k': rwkv_k_k[l], 'rwkv_k_a': rwkv_k_a[l], 'rwkv_r_k': rwkv_r_k[l],
              'rwkv_ln_g': rwkv_ln_g[l], 'rwkv_ln_b': rwkv_ln_b[l], 'ret_ln_g': ret_ln_g[l]}
        xp, st = _mixing_sublayer(xp, lp, 0, None, None, None,
                                  jnp.zeros((Bp, RWKV_HEADS, HEAD_DIM, HEAD_DIM), jnp.float32),
                                  jnp.zeros((Bp, RWKV_IN), xp.dtype),
                                  jnp.zeros((Bp, RET_HEADS, HEAD_DIM, HEAD_DIM), jnp.float32))
        outs_p.append(st)
        past_cmp = cache_nsa_cmp[l][page_table].reshape(Bs, past_len, 2, NSA_KV_HEADS, HEAD_DIM)
        past_sel = cache_nsa_sel[l][page_table].reshape(Bs, past_len, 2, NSA_KV_HEADS, HEAD_DIM)
        xs, st = _mixing_sublayer(xs, lp, past_len, past_cmp, past_sel, cache_nsa_win[l], state_rwkv[l],
                                  state_rwkv_shift[l], state_ret[l])
        outs_s.append(st)
        i = l // 2
        if l % 2 == 0:
            xp = _dense_ffn(xp, norm_ffn[l], ffn_w_gate[i], ffn_w_up[i], ffn_w_down[i])
            xs = _dense_ffn(xs, norm_ffn[l], ffn_w_gate[i], ffn_w_up[i], ffn_w_down[i])
        else:
            xp, xs = _moe_ffn([xp, xs], norm_ffn[l], moe_router[i], moe_w_gate[i], moe_w_up[i], moe_w_down[i])
    kv_cmp_p, kv_sel_p, win_p, rwkv_p, shift_p, ret_p = [jnp.stack([o[j] for o in outs_p]) for j in range(6)]
    kv_cmp_s, kv_sel_s, win_s, rwkv_s, shift_s, ret_s = [jnp.stack([o[j] for o in outs_s]) for j in range(6)]
    return (xp, xs, kv_cmp_p, kv_sel_p, win_p, rwkv_p, shift_p, ret_p,
            kv_cmp_s, kv_sel_s, win_s, rwkv_s, shift_s, ret_s)
```

```python
import functools

import jax
import jax.numpy as jnp
from jax import lax
from jax.experimental import pallas as pl
from jax.experimental.pallas import tpu as pltpu

D_MODEL = 1024
DEPTH = 2
HEAD_DIM = 64
NSA_WIDTH = D_MODEL // 2
RWKV_WIDTH = D_MODEL // 4
RET_WIDTH = D_MODEL - NSA_WIDTH - RWKV_WIDTH
NSA_HEADS = NSA_WIDTH // HEAD_DIM
NSA_KV_HEADS = 2
NSA_GROUP = NSA_HEADS // NSA_KV_HEADS
CMP_LEN = 32
CMP_STRIDE = 16
SEL_LEN = 64
SEL_TOP = 16
WINDOW = 512
Q_BLOCK = 128
RWKV_HEADS = RWKV_WIDTH // HEAD_DIM
RWKV_W_RANK = 64
RWKV_A_RANK = 64
RWKV_G_RANK = 128
RWKV_GN_EPS = 64e-5
RET_HEADS = RET_WIDTH // HEAD_DIM
RET_CHUNK = 128
ROPE_BASE = 10000.0
N_EXPERTS = 8
TOP_K = 2
NSA_KV_COLS = NSA_KV_HEADS * HEAD_DIM
NSA_IN = NSA_WIDTH + 6 * NSA_KV_COLS + 3 * NSA_HEADS
RWKV_IN = 3 * RWKV_WIDTH + RWKV_W_RANK + RWKV_A_RANK + RWKV_G_RANK
RET_IN = 4 * RET_WIDTH
EPS = 1e-6
GN_EPS = 1e-5
NEG = -1e30
TINY = 1e-30
FORCE = 1e9

VMEM_LIMIT_BYTES = 56 * 1024 * 1024


def _pick_tile(n, target):
    t = min(n, target)
    while n % t:
        t //= 2
    return t


def _ffn_body(expert_ref, used_ref, x_ref, g_ref, s_ref, wg_ref, wu_ref, wd_ref, y_ref, h_scr, acc_scr):
    i = pl.program_id(0)
    j = pl.program_id(1)
    last = pl.num_programs(1) - 1
    used = used_ref[i] > 0

    @pl.when(used & (j == 0))
    def _():
        x = x_ref[...]
        h = x * lax.rsqrt(jnp.mean(x * x, axis=-1, keepdims=True) + EPS) * g_ref[...]
        h_scr[...] = h.astype(jnp.bfloat16)
        acc_scr[...] = jnp.zeros_like(acc_scr)

    @pl.when(used)
    def _():
        h = h_scr[...]
        a = jnp.dot(h, wg_ref[...].astype(jnp.bfloat16), preferred_element_type=jnp.float32)
        b = jnp.dot(h, wu_ref[...].astype(jnp.bfloat16), preferred_element_type=jnp.float32)
        z = (a * jax.nn.sigmoid(a)) * b
        acc_scr[...] += jnp.dot(z.astype(jnp.bfloat16), wd_ref[...].astype(jnp.bfloat16),
                                preferred_element_type=jnp.float32)

    @pl.when(used & (j == last))
    def _():
        y_ref[...] = acc_scr[...] * s_ref[...]

    @pl.when(jnp.logical_not(used) & (j == last))
    def _():
        y_ref[...] = jnp.zeros_like(y_ref)


def _ffn(x, g, scale, wg, wu, wd, tile_expert, tile_used, *, tm, tf=512):
    M, D = x.shape
    F = wg.shape[2]
    tf = _pick_tile(F, tf)
    grid_spec = pltpu.PrefetchScalarGridSpec(
        num_scalar_prefetch=2,
        grid=(M // tm, F // tf),
        in_specs=[
            pl.BlockSpec((tm, D), lambda i, j, e, u: (i, 0)),
            pl.BlockSpec((1, D), lambda i, j, e, u: (0, 0)),
            pl.BlockSpec((tm, 1), lambda i, j, e, u: (i, 0)),
            pl.BlockSpec((None, D, tf), lambda i, j, e, u: (e[i], 0, j)),
            pl.BlockSpec((None, D, tf), lambda i, j, e, u: (e[i], 0, j)),
            pl.BlockSpec((None, tf, D), lambda i, j, e, u: (e[i], j, 0)),
        ],
        out_specs=pl.BlockSpec((tm, D), lambda i, j, e, u: (i, 0)),
        scratch_shapes=[pltpu.VMEM((tm, D), jnp.bfloat16), pltpu.VMEM((tm, D), jnp.float32)],
    )
    return pl.pallas_call(
        _ffn_body,
        grid_spec=grid_spec,
        out_shape=jax.ShapeDtypeStruct((M, D), jnp.float32),
        compiler_params=pltpu.CompilerParams(
            dimension_semantics=("parallel", "arbitrary"), vmem_limit_bytes=VMEM_LIMIT_BYTES),
        name="ffn",
    )(tile_expert, tile_used, x, g.reshape(1, D), scale, wg, wu, wd)


def _router_body(x_ref, g_ref, w_ref, o_ref):
    x = x_ref[...]
    h = x * lax.rsqrt(jnp.mean(x * x, axis=-1, keepdims=True) + EPS) * g_ref[...]
    o_ref[...] = jnp.dot(h, w_ref[...], preferred_element_type=jnp.float32, precision=lax.Precision.HIGHEST)


def _router_logits(x, g, router, *, tm=512):
    M, D = x.shape
    E = router.shape[1]
    tm = _pick_tile(M, tm)
    lanes = 128
    w = jnp.pad(router, ((0, 0), (0, lanes - E)))
    out = pl.pallas_call(
        _router_body,
        grid=(M // tm,),
        in_specs=[pl.BlockSpec((tm, D), lambda i: (i, 0)), pl.BlockSpec((1, D), lambda i: (0, 0)),
                  pl.BlockSpec((D, lanes), lambda i: (0, 0))],
        out_specs=pl.BlockSpec((tm, lanes), lambda i: (i, 0)),
        out_shape=jax.ShapeDtypeStruct((M, lanes), jnp.float32),
        compiler_params=pltpu.CompilerParams(dimension_semantics=("parallel",), vmem_limit_bytes=VMEM_LIMIT_BYTES),
        name="router",
    )(x, g.reshape(1, D), w)
    return out[:, :E]


def _rwkv_scan_body(r_ref, w_ref, k_ref, v_ref, kk_ref, b_ref, s0_ref, y_ref, st_ref, s_scr, *, sub):
    j = pl.program_id(1)
    n_b, t_blk, _ = r_ref.shape
    d = HEAD_DIM

    @pl.when(j == 0)
    def _():
        s_scr[...] = s0_ref[...]

    eye = lax.broadcasted_iota(jnp.int32, (d, d), 0) == lax.broadcasted_iota(jnp.int32, (d, d), 1)

    def sub_block(i, carry):
        t0 = pl.multiple_of(i * sub, sub)
        for bb in range(n_b):
            blk = [ref[bb, pl.ds(t0, sub), :] for ref in (r_ref, w_ref, k_ref, v_ref, kk_ref, b_ref)]
            for h in range(RWKV_HEADS):
                cols = slice(h * d, (h + 1) * d)
                S = s_scr[bb, h]
                y_rows = []
                for t in range(sub):
                    r_t, w_t, k_t, v_t, kk_t, b_t = [x[t:t + 1, cols] for x in blk]
                    sa = jnp.sum(S * kk_t, axis=1, keepdims=True)
                    v_col = jnp.sum(jnp.where(eye, v_t, 0.0), axis=1, keepdims=True)
                    S = S * w_t - sa * b_t + v_col * k_t
                    y_col = jnp.sum(S * r_t, axis=1, keepdims=True)
                    y_rows.append(jnp.sum(jnp.where(eye, y_col, 0.0), axis=0, keepdims=True))
                s_scr[bb, h] = S
                y_ref[bb, pl.ds(t0, sub), cols] = jnp.concatenate(y_rows, axis=0)
        return carry

    lax.fori_loop(0, t_blk // sub, sub_block, 0)

    @pl.when(j == pl.num_programs(1) - 1)
    def _():
        st_ref[...] = s_scr[...]


def _rwkv_scan(r, w, k, v, kk, b, s0, *, n_b=2, t_blk=256):
    B, T, W = r.shape
    n_b = _pick_tile(B, n_b)
    t_blk = _pick_tile(T, t_blk)
    sub = 8 if t_blk % 8 == 0 else t_blk
    seq = pl.BlockSpec((n_b, t_blk, W), lambda i, j: (i, j, 0))
    st = pl.BlockSpec((n_b, RWKV_HEADS, HEAD_DIM, HEAD_DIM), lambda i, j: (i, 0, 0, 0))
    return pl.pallas_call(
        functools.partial(_rwkv_scan_body, sub=sub),
        grid=(B // n_b, T // t_blk),
        in_specs=[seq] * 6 + [st],
        out_specs=[seq, st],
        out_shape=[jax.ShapeDtypeStruct((B, T, W), jnp.float32),
                   jax.ShapeDtypeStruct((B, RWKV_HEADS, HEAD_DIM, HEAD_DIM), jnp.float32)],
        scratch_shapes=[pltpu.VMEM((n_b, RWKV_HEADS, HEAD_DIM, HEAD_DIM), jnp.float32)],
        compiler_params=pltpu.CompilerParams(
            dimension_semantics=("parallel", "arbitrary"), vmem_limit_bytes=VMEM_LIMIT_BYTES),
        name="rwkv_scan",
    )(r, w, k, v, kk, b, s0)


RWKV_CHUNK = 64
RWKV_CHUNKS_PER_STEP = 4


_F32_DOT = dict(preferred_element_type=jnp.float32, precision=lax.Precision.HIGHEST)


def _dot(a, b):
    return jnp.dot(a, b, **_F32_DOT)


def _dot_t(a, b):
    return lax.dot_general(a, b, (((1,), (1,)), ((), ())), **_F32_DOT)


def _dot_0(a, b):
    return lax.dot_general(a, b, (((0,), (0,)), ((), ())), **_F32_DOT)


_NN = (((1,), (0,)), ((), ()))
_NT = (((1,), (1,)), ((), ()))


def _split(x):
    hi = x.astype(jnp.bfloat16)
    return hi, (x - hi.astype(jnp.float32)).astype(jnp.bfloat16)


def _mm3(a, b, dims):
    dot = lambda x, y: lax.dot_general(x, y, dims, preferred_element_type=jnp.float32)
    return dot(a[0], b[0]) + (dot(a[0], b[1]) + dot(a[1], b[0]))


def _rwkv_chunk_body(r_ref, lw_ref, k_ref, v_ref, kk_ref, b_ref, qe_ref, y0_ref, pm_ref, z_ref):
    L, d = RWKV_CHUNK, HEAD_DIM
    n = r_ref.shape[1]
    row = lax.broadcasted_iota(jnp.int32, (n, n), 0)
    col = lax.broadcasted_iota(jnp.int32, (n, n), 1)
    same = (row // L) == (col // L)
    lower = same & (row >= col)
    strict = same & (row > col)
    ones_lower = jnp.where(lower, 1.0, 0.0)
    eye = jnp.where(row == col, 1.0, 0.0)
    eye_d = eye[:d, :d]
    ones_bf = ones_lower.astype(jnp.bfloat16)
    for h in range(RWKV_HEADS):
        cols = slice(h * d, (h + 1) * d)
        r, lw, k, v, kk, b = [ref[0, :, cols] for ref in (r_ref, lw_ref, k_ref, v_ref, kk_ref, b_ref)]
        lw_hi, lw_lo = _split(lw)
        lw_rest = (lw - lw_hi.astype(jnp.float32) - lw_lo.astype(jnp.float32)).astype(jnp.bfloat16)
        G = sum(jnp.dot(ones_bf, t, preferred_element_type=jnp.float32) for t in (lw_hi, lw_lo, lw_rest))
        g_inv = jnp.exp(-G)
        kap, bt, kt, rt = kk * jnp.exp(G - lw), b * g_inv, k * g_inv, r * jnp.exp(G)
        kap2, bt2, kt2, rt2, v2 = _split(kap), _split(bt), _split(kt), _split(rt), _split(v)
        N = jnp.where(strict, _mm3(kap2, bt2, _NT), 0.0)
        Mk = jnp.where(strict, _mm3(kap2, kt2, _NT), 0.0)
        RB = jnp.where(lower, _mm3(rt2, bt2, _NT), 0.0)
        RK = jnp.where(lower, _mm3(rt2, kt2, _NT), 0.0)
        X, P2 = eye - N, _split(N)
        for _ in range(L.bit_length() - 2):
            P2 = _split(_mm3(P2, P2, _NN))
            X = X + _mm3(_split(X), P2, _NN)
        X2, RB2 = _split(X), _split(RB)
        A = _mm3(X2, kap2, _NN)
        C = _mm3(X2, _split(_mm3(_split(Mk), v2, _NN)), _NN)
        qe_ref[0, h] = rt - _mm3(RB2, _split(A), _NN)
        y0_ref[0, h] = _mm3(_split(RK), v2, _NN) - _mm3(RB2, _split(C), _NN)
        for c in range(n // L):
            rows = slice(c * L, (c + 1) * L)
            g_end = jnp.exp(G[(c + 1) * L - 1:(c + 1) * L, :])
            pm_ref[0, h, c] = (eye_d - _dot_0(A[rows], bt[rows])) * g_end
            z_ref[0, h, c] = (_dot_0(v[rows], kt[rows]) - _dot_0(C[rows], bt[rows])) * g_end


def _rwkv_walk_body(qe_ref, y0_ref, pm_ref, z_ref, s0_ref, y_ref, st_ref, s_scr):
    j = pl.program_id(0)
    B, H, n_c = pm_ref.shape[:3]
    L = RWKV_CHUNK

    @pl.when(j == 0)
    def _():
        s_scr[...] = s0_ref[...]

    def one_chunk(c, carry):
        t0 = pl.multiple_of(c * L, L)
        for bb in range(B):
            for h in range(H):
                S = s_scr[bb, h]
                y_ref[bb, h, pl.ds(t0, L), :] = _dot_t(qe_ref[bb, h, pl.ds(t0, L), :], S) + y0_ref[bb, h, pl.ds(t0, L), :]
                s_scr[bb, h] = _dot(S, pm_ref[bb, h, c]) + z_ref[bb, h, c]
        return carry

    lax.fori_loop(0, n_c, one_chunk, 0)

    @pl.when(j == pl.num_programs(0) - 1)
    def _():
        st_ref[...] = s_scr[...]


def _rwkv_chunked(r, lw, k, v, kk, b, s0, *, chunks_per_step=16):
    B, T, W = r.shape
    H, d, L = RWKV_HEADS, HEAD_DIM, RWKV_CHUNK
    n_c = T // L
    group = _pick_tile(n_c, RWKV_CHUNKS_PER_STEP)
    seq = pl.BlockSpec((1, group * L, W), lambda i, c: (i, c, 0))
    per_tok = pl.BlockSpec((1, H, group * L, d), lambda i, c: (i, 0, c, 0))
    per_chunk = pl.BlockSpec((1, H, group, d, d), lambda i, c: (i, 0, c, 0, 0))
    qe, y0, pm, z = pl.pallas_call(
        _rwkv_chunk_body,
        grid=(B, n_c // group),
        in_specs=[seq] * 6,
        out_specs=[per_tok, per_tok, per_chunk, per_chunk],
        out_shape=[jax.ShapeDtypeStruct((B, H, T, d), jnp.float32)] * 2
                  + [jax.ShapeDtypeStruct((B, H, n_c, d, d), jnp.float32)] * 2,
        compiler_params=pltpu.CompilerParams(
            dimension_semantics=("parallel", "parallel"), vmem_limit_bytes=VMEM_LIMIT_BYTES),
        name="rwkv_chunk",
    )(r, lw, k, v, kk, b)
    cs = _pick_tile(n_c, chunks_per_step)
    tok = pl.BlockSpec((B, H, cs * L, d), lambda j: (0, 0, j, 0))
    chk = pl.BlockSpec((B, H, cs, d, d), lambda j: (0, 0, j, 0, 0))
    st = pl.BlockSpec((B, H, d, d), lambda j: (0, 0, 0, 0))
    y, s_t = pl.pallas_call(
        _rwkv_walk_body,
        grid=(n_c // cs,),
        in_specs=[tok, tok, chk, chk, st],
        out_specs=[tok, st],
        out_shape=[jax.ShapeDtypeStruct((B, H, T, d), jnp.float32), jax.ShapeDtypeStruct((B, H, d, d), jnp.float32)],
        scratch_shapes=[pltpu.VMEM((B, H, d, d), jnp.float32)],
        compiler_params=pltpu.CompilerParams(
            dimension_semantics=("arbitrary",), vmem_limit_bytes=VMEM_LIMIT_BYTES),
        name="rwkv_walk",
    )(qe, y0, pm, z, s0)
    return jnp.transpose(y, (0, 2, 1, 3)).reshape(B, T, W), s_t


SEL_CHUNK = 512
WIN_SPAN = WINDOW + Q_BLOCK


def _nsa_scores(k_aug, qT_ref):
    return [jnp.dot(k_aug, qT_ref[0, 0, r], preferred_element_type=jnp.float32) for r in range(NSA_GROUP)]


def _nsa_softmax_cols(s_list, neg):
    out = []
    for s in s_list:
        s = s + neg
        m = jnp.max(s, axis=0, keepdims=True)
        e = jnp.exp(s - m)
        inv = jnp.where(m > 0.5 * NEG, 1.0, 0.0) / jnp.maximum(jnp.sum(e, axis=0, keepdims=True), TINY)
        out.append(e * inv)
    return out


def _nsa_prompt_body(qT_ref, kc_ref, vcT_ref, ks_ref, vsT_ref, kw_ref, vwT_ref, covT_ref, o_ref,
                     sel_scr, m_scr, l_scr, acc_scr):
    i = pl.program_id(2)
    R = NSA_GROUP
    n_cmp = kc_ref.shape[2]
    n_sel = covT_ref.shape[0]
    q0 = i * Q_BLOCK
    pos = q0 + lax.broadcasted_iota(jnp.int32, (1, Q_BLOCK), 1)
    posf = pos.astype(jnp.float32)
    bf = jnp.bfloat16

    c_end = (lax.broadcasted_iota(jnp.int32, (n_cmp, Q_BLOCK), 0) * CMP_STRIDE + (CMP_LEN - 1)).astype(jnp.float32)
    neg_c = jnp.where(posf - c_end >= 0.0, 0.0, NEG)
    p_c = _nsa_softmax_cols(_nsa_scores(kc_ref[0, 0], qT_ref), neg_c)
    vcT = vcT_ref[0, 0]
    o_c = [jnp.dot(vcT, p.astype(bf), preferred_element_type=jnp.float32) for p in p_c]
    for r in range(R):
        o_ref[0, 0, 0, r] = o_c[r]
    p_sum = (p_c[0] + p_c[1]) + (p_c[2] + p_c[3])
    imp = jnp.dot(covT_ref[...], p_sum, preferred_element_type=jnp.float32, precision=lax.Precision.HIGHEST)
    blk = lax.broadcasted_iota(jnp.int32, (n_sel, Q_BLOCK), 0)
    avail = blk * SEL_LEN <= pos
    forced = (blk == jnp.right_shift(pos, 6)) | (blk == 0)
    imp = jnp.where(avail, jnp.where(forced, FORCE, imp), -FORCE)

    sub8 = lax.broadcasted_iota(jnp.int32, (8, Q_BLOCK), 0)
    for jj in range(n_sel // 8):
        vj = imp[8 * jj:8 * jj + 8]
        cnt = jnp.zeros((8, Q_BLOCK), jnp.float32)
        for k in range(n_sel):
            row = imp[k:k + 1]
            if k < 8 * jj:
                cnt = cnt + jnp.where(row >= vj, 1.0, 0.0)
            elif k >= 8 * jj + 8:
                cnt = cnt + jnp.where(row > vj, 1.0, 0.0)
            else:
                cnt = cnt + jnp.where(sub8 > (k - 8 * jj), jnp.where(row >= vj, 1.0, 0.0),
                                      jnp.where(row > vj, 1.0, 0.0))
        sel_scr[8 * jj:8 * jj + 8, :] = jnp.where(cnt < float(SEL_TOP), 1.0, 0.0)

    m_scr[...] = jnp.full_like(m_scr, NEG)
    l_scr[...] = jnp.zeros_like(l_scr)
    acc_scr[...] = jnp.zeros_like(acc_scr)
    key_iota = lax.broadcasted_iota(jnp.int32, (SEL_CHUNK, Q_BLOCK), 0).astype(jnp.float32)
    blocks_per_chunk = SEL_CHUNK // SEL_LEN

    def chunk(c, carry):
        k0 = pl.multiple_of(c * SEL_CHUNK, SEL_CHUNK)
        s_all = _nsa_scores(ks_ref[0, 0, pl.ds(k0, SEL_CHUNK), :], qT_ref)
        dist = (posf - k0.astype(jnp.float32)) - key_iota
        picked = jnp.concatenate(
            [jnp.broadcast_to(sel_scr[pl.ds(c * blocks_per_chunk + b, 1), :], (SEL_LEN, Q_BLOCK))
             for b in range(blocks_per_chunk)], axis=0)
        neg = jnp.where((picked > 0.5) & (dist >= 0.0), 0.0, NEG)
        m_old = [m_scr[r] for r in range(R)]
        m_new, p_all, l_add = [], [], []
        for r in range(R):
            s = s_all[r] + neg
            m = jnp.maximum(m_old[r], jnp.max(s, axis=0, keepdims=True))
            p = jnp.exp(s - m)
            m_new.append(m)
            l_add.append(jnp.sum(p, axis=0, keepdims=True))
            p_all.append(p.astype(bf))
        vsT = vsT_ref[0, 0, :, pl.ds(k0, SEL_CHUNK)]
        pv = [jnp.dot(vsT, p, preferred_element_type=jnp.float32) for p in p_all]
        for r in range(R):
            alpha = jnp.exp(m_old[r] - m_new[r])
            l_scr[r] = alpha * l_scr[r] + l_add[r]
            acc_scr[r] = alpha * acc_scr[r] + pv[r]
            m_scr[r] = m_new[r]
        return carry

    lax.fori_loop(0, q0 // SEL_CHUNK + 1, chunk, 0)
    for r in range(R):
        o_ref[1, 0, 0, r] = acc_scr[r] / jnp.maximum(l_scr[r], TINY)

    k0w = pl.multiple_of(jnp.maximum(q0 - WINDOW, 0), Q_BLOCK)
    d_win = (posf - k0w.astype(jnp.float32)) - lax.broadcasted_iota(jnp.int32, (WIN_SPAN, Q_BLOCK), 0).astype(jnp.float32)
    neg_w = jnp.where((d_win >= 0.0) & (d_win < float(WINDOW)), 0.0, NEG)
    p_w = _nsa_softmax_cols(_nsa_scores(kw_ref[0, 0, pl.ds(k0w, WIN_SPAN), :], qT_ref), neg_w)
    vwT = vwT_ref[0, 0, :, pl.ds(k0w, WIN_SPAN)]
    o_w = [jnp.dot(vwT, p.astype(bf), preferred_element_type=jnp.float32) for p in p_w]
    for r in range(R):
        o_ref[2, 0, 0, r] = o_w[r]


def _nsa_prompt_attn(q, kc, vc, ks, vs, kw, vw):
    B, T, H, d = q.shape
    G, R = NSA_KV_HEADS, NSA_GROUP
    n_c = kc.shape[1]
    n_cmp = -(-n_c // 128) * 128
    n_sel = T // SEL_LEN
    bf = jnp.bfloat16
    lanes = 128

    def key_rows(k, key_pos):
        n = k.shape[1]
        extra = jnp.zeros((n, lanes - d), jnp.float32).at[:, 0].set((key_pos // SEL_LEN).astype(jnp.float32))
        extra = extra.at[:, 1].set((key_pos % SEL_LEN).astype(jnp.float32))
        extra = jnp.broadcast_to(extra[None, None], (B, G, n, lanes - d))
        return jnp.concatenate([jnp.transpose(k, (0, 2, 1, 3)), extra], axis=-1).astype(bf)

    cols = lambda t: jnp.transpose(t, (0, 2, 3, 1)).astype(bf)
    slopes = (2.0 ** -jnp.arange(1, H + 1, dtype=jnp.float32)).reshape(G, R)
    q_extra = jnp.zeros((G, R, lanes - d), jnp.float32).at[:, :, 0].set(SEL_LEN * slopes).at[:, :, 1].set(slopes)
    qT = jnp.transpose((q * (d ** -0.5)).reshape(B, T, G, R, d), (0, 2, 3, 4, 1))
    qT = jnp.concatenate([qT, jnp.broadcast_to(q_extra[None, :, :, :, None], (B, G, R, lanes - d, T))],
                         axis=3).astype(bf)
    pad_c = ((0, 0), (0, n_cmp - n_c), (0, 0), (0, 0))
    kc_r = key_rows(jnp.pad(kc, pad_c), jnp.arange(n_cmp, dtype=jnp.int32) * CMP_STRIDE + (CMP_LEN - 1))
    vc_c = cols(jnp.pad(vc, pad_c))
    tok = jnp.arange(T, dtype=jnp.int32)
    rows = lambda t: key_rows(t, tok)
    c_start = jnp.arange(n_cmp, dtype=jnp.int32) * CMP_STRIDE
    s_start = jnp.arange(n_sel, dtype=jnp.int32) * SEL_LEN
    covT = jnp.maximum(jnp.minimum(c_start[None, :] + CMP_LEN, s_start[:, None] + SEL_LEN)
                       - jnp.maximum(c_start[None, :], s_start[:, None]), 0).astype(jnp.float32) / CMP_LEN
    full_r = lambda n: pl.BlockSpec((1, 1, n, lanes), lambda b, g, i: (b, g, 0, 0))
    full_c = lambda n: pl.BlockSpec((1, 1, d, n), lambda b, g, i: (b, g, 0, 0))
    oT = pl.pallas_call(
        _nsa_prompt_body,
        grid=(B, G, T // Q_BLOCK),
        in_specs=[pl.BlockSpec((1, 1, R, lanes, Q_BLOCK), lambda b, g, i: (b, g, 0, 0, i)),
                  full_r(n_cmp), full_c(n_cmp), full_r(T), full_c(T), full_r(T), full_c(T),
                  pl.BlockSpec((n_sel, n_cmp), lambda b, g, i: (0, 0))],
        out_specs=pl.BlockSpec((3, 1, 1, R, d, Q_BLOCK), lambda b, g, i: (0, b, g, 0, 0, i)),
        out_shape=jax.ShapeDtypeStruct((3, B, G, R, d, T), jnp.float32),
        scratch_shapes=[pltpu.VMEM((n_sel, Q_BLOCK), jnp.float32),
                        pltpu.VMEM((R, 1, Q_BLOCK), jnp.float32),
                        pltpu.VMEM((R, 1, Q_BLOCK), jnp.float32),
                        pltpu.VMEM((R, d, Q_BLOCK), jnp.float32)],
        compiler_params=pltpu.CompilerParams(
            dimension_semantics=("parallel", "parallel", "arbitrary"), vmem_limit_bytes=VMEM_LIMIT_BYTES),
        name="nsa_prompt",
    )(qT, kc_r, vc_c, rows(ks), cols(vs), rows(kw), cols(vw), covT)
    return jnp.transpose(oT, (0, 1, 5, 2, 3, 4)).reshape(3, B, T, H, d)


def _rms_norm(x, g):
    xf = x.astype(jnp.float32)
    y = xf * lax.rsqrt(jnp.mean(xf * xf, axis=-1, keepdims=True) + EPS)
    return (y * g.astype(jnp.float32)).astype(x.dtype)


def _group_norm(x, g, eps):
    xf = x.astype(jnp.float32)
    mu = jnp.mean(xf, axis=-1, keepdims=True)
    var = jnp.mean(jnp.square(xf - mu), axis=-1, keepdims=True)
    return (xf - mu) * lax.rsqrt(var + eps) * g.astype(jnp.float32)


def _masked_softmax(s, mask):
    s = jnp.where(mask, s, NEG)
    m = jnp.max(s, axis=-1, keepdims=True)
    e = jnp.where(mask, jnp.exp(s - m), 0.0)
    return e / jnp.maximum(jnp.sum(e, axis=-1, keepdims=True), TINY)


def _alibi_slopes(n):
    return 2.0 ** (-8.0 * jnp.arange(1, n + 1, dtype=jnp.float32) / n)


def _rotary(x, pos):
    half = x.shape[-1] // 2
    freqs = ROPE_BASE ** (-jnp.arange(half, dtype=jnp.float32) / half)
    ang = pos.astype(jnp.float32)[:, None] * freqs[None, :]
    cos, sin = jnp.cos(ang)[None, :, None, :], jnp.sin(ang)[None, :, None, :]
    x1, x2 = x[..., :half], x[..., half:]
    return jnp.concatenate([x1 * cos - x2 * sin, x1 * sin + x2 * cos], axis=-1)


def _nsa_compress(rows, pe, w1, w2):
    B, T, G, d = rows.shape
    n_cmp = (T - CMP_LEN) // CMP_STRIDE + 1
    idx = (jnp.arange(n_cmp, dtype=jnp.int32) * CMP_STRIDE)[:, None] + jnp.arange(CMP_LEN, dtype=jnp.int32)[None, :]
    blk = rows[:, idx] + pe[None, None, :, None, :]
    blk = jnp.transpose(blk, (0, 1, 3, 2, 4)).reshape(B, n_cmp, G, CMP_LEN * d)
    return jax.nn.gelu(blk @ w1) @ w2


def _to_sel_blocks(rows):
    B, T, G, d = rows.shape
    n_sel = -(-T // SEL_LEN)
    rows = jnp.pad(rows, ((0, 0), (0, n_sel * SEL_LEN - T), (0, 0), (0, 0)))
    return jnp.transpose(rows.reshape(B, n_sel, SEL_LEN, G, d), (0, 3, 1, 2, 4))


def _nsa_branches(q, pos, kc, vc, ks_blk, vs_blk, kw, vw, pos_w, slopes):
    B, Tq, H, d = q.shape
    G, R = NSA_KV_HEADS, NSA_GROUP
    scale = d ** -0.5
    qg = q.reshape(B, Tq, G, R, d)
    sl = slopes.reshape(G, R)
    posf = pos.astype(jnp.float32)
    n_cmp = kc.shape[1]
    c_start = jnp.arange(n_cmp, dtype=jnp.int32) * CMP_STRIDE
    d_cmp = posf[:, None] - (c_start + CMP_LEN - 1).astype(jnp.float32)[None, :]
    s = jnp.einsum('btgrd,bngd->bgrtn', qg, kc).astype(jnp.float32) * scale - sl[None, :, :, None, None] * d_cmp
    p_cmp = _masked_softmax(s, (d_cmp >= 0.0)[None, None, None])
    o_cmp = jnp.einsum('bgrtn,bngd->btgrd', p_cmp.astype(vc.dtype), vc)
    n_sel = ks_blk.shape[2]
    s_start = jnp.arange(n_sel, dtype=jnp.int32) * SEL_LEN
    cover = jnp.maximum(jnp.minimum(c_start[:, None] + CMP_LEN, s_start[None, :] + SEL_LEN)
                        - jnp.maximum(c_start[:, None], s_start[None, :]), 0).astype(jnp.float32) / CMP_LEN
    imp = jnp.einsum('bgrtn,nj->bgtj', p_cmp, cover)
    blk = jnp.arange(n_sel, dtype=jnp.int32)
    avail = s_start[None, :] <= pos[:, None]
    forced = (blk[None, :] == (pos // SEL_LEN)[:, None]) | (blk[None, :] == 0)
    imp = jnp.where(avail, jnp.where(forced, FORCE, imp), -FORCE)
    n_top = min(SEL_TOP, n_sel)
    _, idx = lax.top_k(imp, n_top)
    b_i = jnp.arange(B)[:, None, None, None]
    g_i = jnp.arange(G)[None, :, None, None]
    ks_g = ks_blk[b_i, g_i, idx]
    vs_g = vs_blk[b_i, g_i, idx]
    kpos = idx[..., None] * SEL_LEN + jnp.arange(SEL_LEN, dtype=jnp.int32)
    d_sel = (pos[None, None, :, None, None] - kpos).astype(jnp.float32)[:, :, None]
    s = jnp.einsum('btgrd,bgtnsd->bgrtns', qg, ks_g).astype(jnp.float32) * scale - sl[None, :, :, None, None, None] * d_sel
    mask = jnp.broadcast_to(d_sel >= 0.0, s.shape)
    p_sel = _masked_softmax(s.reshape(B, G, R, Tq, -1), mask.reshape(B, G, R, Tq, -1)).reshape(s.shape)
    o_sel = jnp.einsum('bgrtns,bgtnsd->btgrd', p_sel.astype(vs_g.dtype), vs_g)
    d_win = posf[:, None] - pos_w.astype(jnp.float32)[None, :]
    s = jnp.einsum('btgrd,bwgd->bgrtw', qg, kw).astype(jnp.float32) * scale - sl[None, :, :, None, None] * d_win
    win_mask = (d_win >= 0.0) & (d_win < WINDOW) & (pos_w >= 0)[None, :]
    p_win = _masked_softmax(s, win_mask[None, None, None])
    o_win = jnp.einsum('bgrtw,bwgd->btgrd', p_win.astype(vw.dtype), vw)
    return jnp.stack([o_cmp, o_sel, o_win]).reshape(3, B, Tq, H, d)


def _rwkv_group(u, shift0, S0, lp):
    B, T, _ = u.shape
    W = RWKV_WIDTH
    uf = u.astype(jnp.float32)
    prev = jnp.concatenate([shift0.astype(jnp.float32)[:, None], uf[:, :-1]], axis=1)
    um = uf + (prev - uf) * lp['rwkv_mu']
    r, k, v = um[..., :W], um[..., W:2 * W], um[..., 2 * W:3 * W]
    o = 3 * W
    wd = um[..., o:o + RWKV_W_RANK]
    ad = um[..., o + RWKV_W_RANK:o + RWKV_W_RANK + RWKV_A_RANK]
    gd = um[..., o + RWKV_W_RANK + RWKV_A_RANK:]
    w = lp['rwkv_w0'] + jnp.tanh(wd) @ lp['rwkv_w_up']
    log_decay = -jnp.exp(-jax.nn.softplus(-w) - 0.5)
    a = jax.nn.sigmoid(lp['rwkv_a0'] + ad @ lp['rwkv_a_up'])
    g = jax.nn.sigmoid(gd) @ lp['rwkv_g_up']
    kk = k * lp['rwkv_k_k']
    k = k * (1.0 + (a - 1.0) * lp['rwkv_k_a'])
    hd = lambda t: t.reshape(B, T, RWKV_HEADS, HEAD_DIM).astype(jnp.float32)
    r, k, v, a, g, kk = hd(r), hd(k), hd(v), hd(a), hd(g), hd(kk)
    kk = kk / jnp.maximum(jnp.sqrt(jnp.sum(kk * kk, axis=-1, keepdims=True)), 1e-12)
    flat = lambda t: t.reshape(B, T, W)
    if T % RWKV_CHUNK == 0:
        ys, S_T = _rwkv_chunked(flat(r), log_decay, flat(k), flat(v), flat(kk), flat(kk * a), S0.astype(jnp.float32))
    else:
        ys, S_T = _rwkv_scan(flat(r), jnp.exp(log_decay), flat(k), flat(v), flat(kk), flat(kk * a),
                             S0.astype(jnp.float32))
    y = _group_norm(ys.reshape(B, T, RWKV_HEADS, HEAD_DIM), lp['rwkv_ln_g'], RWKV_GN_EPS) + lp['rwkv_ln_b']
    y = y + jnp.sum(r * k * lp['rwkv_r_k'], axis=-1, keepdims=True) * v
    y = y * g
    return y.reshape(B, T, W).astype(u.dtype), u[:, -1], S_T


def _retention_group(u, pos, S0, ln_g):
    B, T, _ = u.shape
    uf = u.astype(jnp.float32)
    q, k, v, g = [t.reshape(B, T, RET_HEADS, HEAD_DIM) for t in jnp.split(uf, 4, axis=-1)]
    q = _rotary(q, pos)
    k = _rotary(k, pos) * HEAD_DIM ** -0.5
    lg = jnp.log(1.0 - 2.0 ** (-5.0 - jnp.arange(RET_HEADS, dtype=jnp.float32)))
    C = RET_CHUNK if T % RET_CHUNK == 0 else T
    nC = T // C
    n = jnp.arange(C, dtype=jnp.float32)
    diff = n[:, None] - n[None, :]
    dmask = jnp.where(diff[None] >= 0, jnp.exp(jnp.maximum(diff, 0.0)[None] * lg[:, None, None]), 0.0)
    q_dec = jnp.exp((n[:, None] + 1.0) * lg[None, :])
    k_dec = jnp.exp((C - 1.0 - n)[:, None] * lg[None, :])
    s_dec = jnp.exp(C * lg)

    def chunk(S, inp):
        qc, kc, vc = inp
        att = jnp.einsum('bnhd,bmhd->bhnm', qc, kc) * dmask
        out = jnp.einsum('bhnm,bmhe->bnhe', att, vc) + jnp.einsum('bnhd,bhde->bnhe', qc, S) * q_dec[None, :, :, None]
        S = S * s_dec[None, :, None, None] + jnp.einsum('bmhd,bmhe->bhde', kc * k_dec[None, :, :, None], vc)
        return S, out

    to_chunks = lambda t: jnp.moveaxis(t.reshape(B, nC, C, RET_HEADS, HEAD_DIM), 1, 0)
    S_T, o = lax.scan(chunk, S0.astype(jnp.float32), (to_chunks(q), to_chunks(k), to_chunks(v)))
    o = jnp.moveaxis(o, 0, 1).reshape(B, T, RET_HEADS, HEAD_DIM)
    y = jax.nn.silu(g) * _group_norm(o, ln_g, GN_EPS)
    return y.reshape(B, T, RET_WIDTH).astype(u.dtype), S_T


def _mixing_sublayer(x, lp, past_len, past_cmp, past_sel, win_buf, rwkv_S, rwkv_shift, ret_S):
    B, T, _ = x.shape
    h = _rms_norm(x, lp['norm_attn'])
    P = h @ lp['w_in']
    c = P[..., :NSA_IN]
    q = _rms_norm(c[..., :NSA_WIDTH].reshape(B, T, NSA_HEADS, HEAD_DIM), lp['nsa_q_norm'])
    kv = c[..., NSA_WIDTH:NSA_WIDTH + 6 * NSA_KV_COLS].reshape(B, T, 3, 2, NSA_KV_HEADS, HEAD_DIM)
    kv_cmp, kv_sel, kv_win = kv[:, :, 0], kv[:, :, 1], kv[:, :, 2]
    gates = jax.nn.sigmoid(c[..., NSA_WIDTH + 6 * NSA_KV_COLS:].astype(jnp.float32)).reshape(B, T, NSA_HEADS, 3)
    pos = past_len + jnp.arange(T, dtype=jnp.int32)
    slopes = _alibi_slopes(NSA_HEADS)
    prompt = past_cmp is None
    if prompt:
        rows_cmp, rows_sel, rows_win = kv_cmp, kv_sel, kv_win
        new_win = kv_win[:, T - min(WINDOW, T):]
    else:
        rows_cmp = jnp.concatenate([past_cmp, kv_cmp.astype(past_cmp.dtype)], axis=1)
        rows_sel = jnp.concatenate([past_sel, kv_sel.astype(past_sel.dtype)], axis=1)
        rows_win = jnp.concatenate([win_buf, kv_win.astype(win_buf.dtype)], axis=1)
        new_win = rows_win[:, T:]
    k_norm = lp['nsa_k_norm']
    kc = _rms_norm(_nsa_compress(rows_cmp[:, :, 0], lp['nsa_cmp_pe'][0], lp['nsa_cmp_w1'][0], lp['nsa_cmp_w2'][0]), k_norm[0])
    vc = _nsa_compress(rows_cmp[:, :, 1], lp['nsa_cmp_pe'][1], lp['nsa_cmp_w1'][1], lp['nsa_cmp_w2'][1])
    kw = _rms_norm(rows_win[:, :, 0], k_norm[2])
    vw = rows_win[:, :, 1]
    if prompt:
        o3 = _nsa_prompt_attn(q, kc, vc, _rms_norm(rows_sel[:, :, 0], k_norm[1]), rows_sel[:, :, 1], kw, vw)
    else:
        ks_blk = _to_sel_blocks(_rms_norm(rows_sel[:, :, 0], k_norm[1]))
        vs_blk = _to_sel_blocks(rows_sel[:, :, 1])
        wb = win_buf.shape[1]
        pos_w = past_len - wb + jnp.arange(wb + T, dtype=jnp.int32)
        o3 = _nsa_branches(q, pos, kc, vc, ks_blk, vs_blk, kw, vw, pos_w, slopes)
    o_nsa = jnp.einsum('btha,abthd->bthd', gates.astype(o3.dtype), o3)
    o_nsa = _rms_norm(o_nsa, lp['nsa_out_norm']).reshape(B, T, NSA_WIDTH)
    y_rwkv, new_shift, new_rwkv = _rwkv_group(P[..., NSA_IN:NSA_IN + RWKV_IN], rwkv_shift, rwkv_S, lp)
    y_ret, new_ret = _retention_group(P[..., NSA_IN + RWKV_IN:], pos, ret_S, lp['ret_ln_g'])
    mix = jnp.concatenate([o_nsa, y_rwkv.astype(o_nsa.dtype), y_ret.astype(o_nsa.dtype)], axis=-1)
    x = x + mix @ lp['w_out']
    return x, (kv_cmp, kv_sel, new_win, new_rwkv, new_shift, new_ret)


FFN_ROW_TILE = 512


def _dense_ffn(x, g, wg, wu, wd):
    shp = x.shape
    x2 = x.reshape(-1, shp[-1])
    tm = _pick_tile(x2.shape[0], FFN_ROW_TILE)
    n_tiles = x2.shape[0] // tm
    ones = jnp.ones((x2.shape[0], 1), jnp.float32)
    y = _ffn(x2, g, ones, wg[None], wu[None], wd[None], jnp.zeros((n_tiles,), jnp.int32),
             jnp.ones((n_tiles,), jnp.int32), tm=tm)
    return (x2 + y).reshape(shp)


def _moe_ffn(xs, g, router, wg, wu, wd):
    D = xs[0].shape[-1]
    E, tm = router.shape[1], FFN_ROW_TILE
    flat = [x.reshape(-1, D) for x in xs]
    x2 = jnp.concatenate(flat, axis=0)
    logits = jnp.concatenate([_router_logits(x, g, router) for x in flat], axis=0)
    top_val, top_idx = lax.top_k(logits, TOP_K)
    gate = jax.nn.softmax(top_val, axis=-1)
    N = x2.shape[0]
    A = N * TOP_K
    flat_e, flat_w = top_idx.reshape(A).astype(jnp.int32), gate.reshape(A)
    order = jnp.argsort(flat_e, stable=True).astype(jnp.int32)
    counts = jnp.sum(jax.nn.one_hot(flat_e, E, dtype=jnp.int32), axis=0)
    padded = (counts + tm - 1) // tm * tm
    start, p_end = jnp.cumsum(counts) - counts, jnp.cumsum(padded)
    p_start = p_end - padded
    sorted_e = flat_e[order]
    dest = p_start[sorted_e] + (jnp.arange(A, dtype=jnp.int32) - start[sorted_e])
    P = -(-(A + E * (tm - 1)) // tm) * tm
    row_tok = jnp.zeros((P,), jnp.int32).at[dest].set(order // TOP_K)
    row_w = jnp.zeros((P,), jnp.float32).at[dest].set(flat_w[order])
    tile_start = jnp.arange(P // tm, dtype=jnp.int32) * tm
    tile_e = jnp.minimum(jnp.searchsorted(p_end, tile_start, side='right'), E - 1).astype(jnp.int32)
    tile_used = (tile_start < p_end[-1]).astype(jnp.int32)
    ys = _ffn(x2[row_tok], g, row_w[:, None], wg, wu, wd, tile_e, tile_used, tm=tm)
    slot = jnp.zeros((A,), jnp.int32).at[order].set(dest).reshape(N, TOP_K)
    y = x2 + ys[slot[:, 0]] + ys[slot[:, 1]]
    outs, off = [], 0
    for x in xs:
        n = x.size // D
        outs.append(y[off:off + n].reshape(x.shape))
        off += n
    return outs


def kernel(x_prompt, x_sample, cache_nsa_cmp, cache_nsa_sel, cache_nsa_win, state_rwkv, state_rwkv_shift,
           state_ret, page_table, norm_attn, norm_ffn, w_in, w_out, nsa_q_norm, nsa_k_norm, nsa_cmp_pe,
           nsa_cmp_w1, nsa_cmp_w2, nsa_out_norm, rwkv_mu, rwkv_w0, rwkv_w_up, rwkv_a0, rwkv_a_up, rwkv_g_up,
           rwkv_k_k, rwkv_k_a, rwkv_r_k, rwkv_ln_g, rwkv_ln_b, ret_ln_g, ffn_w_gate, ffn_w_up, ffn_w_down,
           moe_router, moe_w_gate, moe_w_up, moe_w_down):
    Bp, Bs = x_prompt.shape[0], x_sample.shape[0]
    past_len = page_table.shape[1] * cache_nsa_cmp.shape[2]
    xp, xs = x_prompt, x_sample
    outs_p, outs_s = [], []
    for l in range(DEPTH):
        lp = {'norm_attn': norm_attn[l], 'w_in': w_in[l], 'w_out': w_out[l], 'nsa_q_norm': nsa_q_norm[l],
              'nsa_k_norm': nsa_k_norm[l], 'nsa_cmp_pe': nsa_cmp_pe[l], 'nsa_cmp_w1': nsa_cmp_w1[l],
              'nsa_cmp_w2': nsa_cmp_w2[l], 'nsa_out_norm': nsa_out_norm[l], 'rwkv_mu': rwkv_mu[l],
              'rwkv_w0': rwkv_w0[l], 'rwkv_w_up': rwkv_w_up[l], 'rwkv_a0': rwkv_a0[l], 'rwkv_a_up': rwkv_a_up[l],
              'rwkv_g_up': rwkv_g_up[l], 'rwkv_k_k': rwkv_k_k[l], 'rwkv_k_a': rwkv_k_a[l], 'rwkv_r_k': rwkv_r_k[l],
              'rwkv_ln_g': rwkv_ln_g[l], 'rwkv_ln_b': rwkv_ln_b[l], 'ret_ln_g': ret_ln_g[l]}
        xp, st = _mixing_sublayer(xp, lp, 0, None, None, None,
                                  jnp.zeros((Bp, RWKV_HEADS, HEAD_DIM, HEAD_DIM), jnp.float32),
                                  jnp.zeros((Bp, RWKV_IN), xp.dtype),
                                  jnp.zeros((Bp, RET_HEADS, HEAD_DIM, HEAD_DIM), jnp.float32))
        outs_p.append(st)
        past_cmp = cache_nsa_cmp[l][page_table].reshape(Bs, past_len, 2, NSA_KV_HEADS, HEAD_DIM)
        past_sel = cache_nsa_sel[l][page_table].reshape(Bs, past_len, 2, NSA_KV_HEADS, HEAD_DIM)
        xs, st = _mixing_sublayer(xs, lp, past_len, past_cmp, past_sel, cache_nsa_win[l], state_rwkv[l],
                                  state_rwkv_shift[l], state_ret[l])
        outs_s.append(st)
        i = l // 2
        if l % 2 == 0:
            xp = _dense_ffn(xp, norm_ffn[l], ffn_w_gate[i], ffn_w_up[i], ffn_w_down[i])
            xs = _dense_ffn(xs, norm_ffn[l], ffn_w_gate[i], ffn_w_up[i], ffn_w_down[i])
        else:
            xp, xs = _moe_ffn([xp, xs], norm_ffn[l], moe_router[i], moe_w_gate[i], moe_w_up[i], moe_w_down[i])
    kv_cmp_p, kv_sel_p, win_p, rwkv_p, shift_p, ret_p = [jnp.stack([o[j] for o in outs_p]) for j in range(6)]
    kv_cmp_s, kv_sel_s, win_s, rwkv_s, shift_s, ret_s = [jnp.stack([o[j] for o in outs_s]) for j in range(6)]
    return (xp, xs, kv_cmp_p, kv_sel_p, win_p, rwkv_p, shift_p, ret_p,
            kv_cmp_s, kv_sel_s, win_s, rwkv_s, shift_s, ret_s)
```

```python
import functools

import jax
import jax.numpy as jnp
from jax import lax
from jax.experimental import pallas as pl
from jax.experimental.pallas import tpu as pltpu

D_MODEL = 1024
DEPTH = 2
HEAD_DIM = 64
NSA_WIDTH = D_MODEL // 2
RWKV_WIDTH = D_MODEL // 4
RET_WIDTH = D_MODEL - NSA_WIDTH - RWKV_WIDTH
NSA_HEADS = NSA_WIDTH // HEAD_DIM
NSA_KV_HEADS = 2
NSA_GROUP = NSA_HEADS // NSA_KV_HEADS
CMP_LEN = 32
CMP_STRIDE = 16
SEL_LEN = 64
SEL_TOP = 16
WINDOW = 512
Q_BLOCK = 128
RWKV_HEADS = RWKV_WIDTH // HEAD_DIM
RWKV_W_RANK = 64
RWKV_A_RANK = 64
RWKV_G_RANK = 128
RWKV_GN_EPS = 64e-5
RET_HEADS = RET_WIDTH // HEAD_DIM
RET_CHUNK = 128
ROPE_BASE = 10000.0
N_EXPERTS = 8
TOP_K = 2
NSA_KV_COLS = NSA_KV_HEADS * HEAD_DIM
NSA_IN = NSA_WIDTH + 6 * NSA_KV_COLS + 3 * NSA_HEADS
RWKV_IN = 3 * RWKV_WIDTH + RWKV_W_RANK + RWKV_A_RANK + RWKV_G_RANK
RET_IN = 4 * RET_WIDTH
EPS = 1e-6
GN_EPS = 1e-5
NEG = -1e30
TINY = 1e-30
FORCE = 1e9

VMEM_LIMIT_BYTES = 56 * 1024 * 1024


def _pick_tile(n, target):
    t = min(n, target)
    while n % t:
        t //= 2
    return t


def _ffn_body(expert_ref, used_ref, x_ref, g_ref, s_ref, wg_ref, wu_ref, wd_ref, y_ref, h_scr, acc_scr):
    i = pl.program_id(0)
    j = pl.program_id(1)
    last = pl.num_programs(1) - 1
    used = used_ref[i] > 0

    @pl.when(used & (j == 0))
    def _():
        x = x_ref[...]
        h = x * lax.rsqrt(jnp.mean(x * x, axis=-1, keepdims=True) + EPS) * g_ref[...]
        h_scr[...] = h.astype(jnp.bfloat16)
        acc_scr[...] = jnp.zeros_like(acc_scr)

    @pl.when(used)
    def _():
        h = h_scr[...]
        a = jnp.dot(h, wg_ref[...].astype(jnp.bfloat16), preferred_element_type=jnp.float32)
        b = jnp.dot(h, wu_ref[...].astype(jnp.bfloat16), preferred_element_type=jnp.float32)
        z = (a * jax.nn.sigmoid(a)) * b
        acc_scr[...] += jnp.dot(z.astype(jnp.bfloat16), wd_ref[...].astype(jnp.bfloat16),
                                preferred_element_type=jnp.float32)

    @pl.when(used & (j == last))
    def _():
        y_ref[...] = acc_scr[...] * s_ref[...]

    @pl.when(jnp.logical_not(used) & (j == last))
    def _():
        y_ref[...] = jnp.zeros_like(y_ref)


def _ffn(x, g, scale, wg, wu, wd, tile_expert, tile_used, *, tm, tf=512):
    M, D = x.shape
    F = wg.shape[2]
    tf = _pick_tile(F, tf)
    grid_spec = pltpu.PrefetchScalarGridSpec(
        num_scalar_prefetch=2,
        grid=(M // tm, F // tf),
        in_specs=[
            pl.BlockSpec((tm, D), lambda i, j, e, u: (i, 0)),
            pl.BlockSpec((1, D), lambda i, j, e, u: (0, 0)),
            pl.BlockSpec((tm, 1), lambda i, j, e, u: (i, 0)),
            pl.BlockSpec((None, D, tf), lambda i, j, e, u: (e[i], 0, j)),
            pl.BlockSpec((None, D, tf), lambda i, j, e, u: (e[i], 0, j)),
            pl.BlockSpec((None, tf, D), lambda i, j, e, u: (e[i], j, 0)),
        ],
        out_specs=pl.BlockSpec((tm, D), lambda i, j, e, u: (i, 0)),
        scratch_shapes=[pltpu.VMEM((tm, D), jnp.bfloat16), pltpu.VMEM((tm, D), jnp.float32)],
    )
    return pl.pallas_call(
        _ffn_body,
        grid_spec=grid_spec,
        out_shape=jax.ShapeDtypeStruct((M, D), jnp.float32),
        compiler_params=pltpu.CompilerParams(
            dimension_semantics=("parallel", "arbitrary"), vmem_limit_bytes=VMEM_LIMIT_BYTES),
        name="ffn",
    )(tile_expert, tile_used, x, g.reshape(1, D), scale, wg, wu, wd)


def _router_body(x_ref, g_ref, w_ref, o_ref):
    x = x_ref[...]
    h = x * lax.rsqrt(jnp.mean(x * x, axis=-1, keepdims=True) + EPS) * g_ref[...]
    o_ref[...] = jnp.dot(h, w_ref[...], preferred_element_type=jnp.float32, precision=lax.Precision.HIGHEST)


def _router_logits(x, g, router, *, tm=512):
    M, D = x.shape
    E = router.shape[1]
    tm = _pick_tile(M, tm)
    lanes = 128
    w = jnp.pad(router, ((0, 0), (0, lanes - E)))
    out = pl.pallas_call(
        _router_body,
        grid=(M // tm,),
        in_specs=[pl.BlockSpec((tm, D), lambda i: (i, 0)), pl.BlockSpec((1, D), lambda i: (0, 0)),
                  pl.BlockSpec((D, lanes), lambda i: (0, 0))],
        out_specs=pl.BlockSpec((tm, lanes), lambda i: (i, 0)),
        out_shape=jax.ShapeDtypeStruct((M, lanes), jnp.float32),
        compiler_params=pltpu.CompilerParams(dimension_semantics=("parallel",), vmem_limit_bytes=VMEM_LIMIT_BYTES),
        name="router",
    )(x, g.reshape(1, D), w)
    return out[:, :E]


def _rwkv_scan_body(r_ref, w_ref, k_ref, v_ref, kk_ref, b_ref, s0_ref, y_ref, st_ref, s_scr, *, sub):
    j = pl.program_id(1)
    n_b, t_blk, _ = r_ref.shape
    d = HEAD_DIM

    @pl.when(j == 0)
    def _():
        s_scr[...] = s0_ref[...]

    eye = lax.broadcasted_iota(jnp.int32, (d, d), 0) == lax.broadcasted_iota(jnp.int32, (d, d), 1)

    def sub_block(i, carry):
        t0 = pl.multiple_of(i * sub, sub)
        for bb in range(n_b):
            blk = [ref[bb, pl.ds(t0, sub), :] for ref in (r_ref, w_ref, k_ref, v_ref, kk_ref, b_ref)]
            for h in range(RWKV_HEADS):
                cols = slice(h * d, (h + 1) * d)
                S = s_scr[bb, h]
                y_rows = []
                for t in range(sub):
                    r_t, w_t, k_t, v_t, kk_t, b_t = [x[t:t + 1, cols] for x in blk]
                    sa = jnp.sum(S * kk_t, axis=1, keepdims=True)
                    v_col = jnp.sum(jnp.where(eye, v_t, 0.0), axis=1, keepdims=True)
                    S = S * w_t - sa * b_t + v_col * k_t
                    y_col = jnp.sum(S * r_t, axis=1, keepdims=True)
                    y_rows.append(jnp.sum(jnp.where(eye, y_col, 0.0), axis=0, keepdims=True))
                s_scr[bb, h] = S
                y_ref[bb, pl.ds(t0, sub), cols] = jnp.concatenate(y_rows, axis=0)
        return carry

    lax.fori_loop(0, t_blk // sub, sub_block, 0)

    @pl.when(j == pl.num_programs(1) - 1)
    def _():
        st_ref[...] = s_scr[...]


def _rwkv_scan(r, w, k, v, kk, b, s0, *, n_b=2, t_blk=256):
    B, T, W = r.shape
    n_b = _pick_tile(B, n_b)
    t_blk = _pick_tile(T, t_blk)
    sub = 8 if t_blk % 8 == 0 else t_blk
    seq = pl.BlockSpec((n_b, t_blk, W), lambda i, j: (i, j, 0))
    st = pl.BlockSpec((n_b, RWKV_HEADS, HEAD_DIM, HEAD_DIM), lambda i, j: (i, 0, 0, 0))
    return pl.pallas_call(
        functools.partial(_rwkv_scan_body, sub=sub),
        grid=(B // n_b, T // t_blk),
        in_specs=[seq] * 6 + [st],
        out_specs=[seq, st],
        out_shape=[jax.ShapeDtypeStruct((B, T, W), jnp.float32),
                   jax.ShapeDtypeStruct((B, RWKV_HEADS, HEAD_DIM, HEAD_DIM), jnp.float32)],
        scratch_shapes=[pltpu.VMEM((n_b, RWKV_HEADS, HEAD_DIM, HEAD_DIM), jnp.float32)],
        compiler_params=pltpu.CompilerParams(
            dimension_semantics=("parallel", "arbitrary"), vmem_limit_bytes=VMEM_LIMIT_BYTES),
        name="rwkv_scan",
    )(r, w, k, v, kk, b, s0)


RWKV_CHUNK = 64
RWKV_CHUNKS_PER_STEP = 4


_F32_DOT = dict(preferred_element_type=jnp.float32, precision=lax.Precision.HIGHEST)


def _dot(a, b):
    return jnp.dot(a, b, **_F32_DOT)


def _dot_t(a, b):
    return lax.dot_general(a, b, (((1,), (1,)), ((), ())), **_F32_DOT)


def _dot_0(a, b):
    return lax.dot_general(a, b, (((0,), (0,)), ((), ())), **_F32_DOT)


_NN = (((1,), (0,)), ((), ()))
_NT = (((1,), (1,)), ((), ()))


def _split(x):
    hi = x.astype(jnp.bfloat16)
    return hi, (x - hi.astype(jnp.float32)).astype(jnp.bfloat16)


def _mm3(a, b, dims):
    dot = lambda x, y: lax.dot_general(x, y, dims, preferred_element_type=jnp.float32)
    return dot(a[0], b[0]) + (dot(a[0], b[1]) + dot(a[1], b[0]))


def _rwkv_chunk_body(r_ref, lw_ref, k_ref, v_ref, kk_ref, b_ref, qe_ref, y0_ref, pm_ref, z_ref):
    L, d = RWKV_CHUNK, HEAD_DIM
    n = r_ref.shape[1]
    row = lax.broadcasted_iota(jnp.int32, (n, n), 0)
    col = lax.broadcasted_iota(jnp.int32, (n, n), 1)
    same = (row // L) == (col // L)
    lower = same & (row >= col)
    strict = same & (row > col)
    ones_lower = jnp.where(lower, 1.0, 0.0)
    eye = jnp.where(row == col, 1.0, 0.0)
    eye_d = eye[:d, :d]
    ones_bf = ones_lower.astype(jnp.bfloat16)
    for h in range(RWKV_HEADS):
        cols = slice(h * d, (h + 1) * d)
        r, lw, k, v, kk, b = [ref[0, :, cols] for ref in (r_ref, lw_ref, k_ref, v_ref, kk_ref, b_ref)]
        lw_hi, lw_lo = _split(lw)
        lw_rest = (lw - lw_hi.astype(jnp.float32) - lw_lo.astype(jnp.float32)).astype(jnp.bfloat16)
        G = sum(jnp.dot(ones_bf, t, preferred_element_type=jnp.float32) for t in (lw_hi, lw_lo, lw_rest))
        g_inv = jnp.exp(-G)
        kap, bt, kt, rt = kk * jnp.exp(G - lw), b * g_inv, k * g_inv, r * jnp.exp(G)
        kap2, bt2, kt2, rt2, v2 = _split(kap), _split(bt), _split(kt), _split(rt), _split(v)
        N = jnp.where(strict, _mm3(kap2, bt2, _NT), 0.0)
        Mk = jnp.where(strict, _mm3(kap2, kt2, _NT), 0.0)
        RB = jnp.where(lower, _mm3(rt2, bt2, _NT), 0.0)
        RK = jnp.where(lower, _mm3(rt2, kt2, _NT), 0.0)
        X, P2 = eye - N, _split(N)
        for _ in range(L.bit_length() - 2):
            P2 = _split(_mm3(P2, P2, _NN))
            X = X + _mm3(_split(X), P2, _NN)
        X2, RB2 = _split(X), _split(RB)
        A = _mm3(X2, kap2, _NN)
        C = _mm3(X2, _split(_mm3(_split(Mk), v2, _NN)), _NN)
        qe_ref[0, h] = rt - _mm3(RB2, _split(A), _NN)
        y0_ref[0, h] = _mm3(_split(RK), v2, _NN) - _mm3(RB2, _split(C), _NN)
        for c in range(n // L):
            rows = slice(c * L, (c + 1) * L)
            g_end = jnp.exp(G[(c + 1) * L - 1:(c + 1) * L, :])
            pm_ref[0, h, c] = (eye_d - _dot_0(A[rows], bt[rows])) * g_end
            z_ref[0, h, c] = (_dot_0(v[rows], kt[rows]) - _dot_0(C[rows], bt[rows])) * g_end


def _rwkv_walk_body(qe_ref, y0_ref, pm_ref, z_ref, s0_ref, y_ref, st_ref, s_scr):
    j = pl.program_id(0)
    B, H, n_c = pm_ref.shape[:3]
    L = RWKV_CHUNK

    @pl.when(j == 0)
    def _():
        s_scr[...] = s0_ref[...]

    def one_chunk(c, carry):
        t0 = pl.multiple_of(c * L, L)
        for bb in range(B):
            for h in range(H):
                S = s_scr[bb, h]
                y_ref[bb, h, pl.ds(t0, L), :] = _dot_t(qe_ref[bb, h, pl.ds(t0, L), :], S) + y0_ref[bb, h, pl.ds(t0, L), :]
                s_scr[bb, h] = _dot(S, pm_ref[bb, h, c]) + z_ref[bb, h, c]
        return carry

    lax.fori_loop(0, n_c, one_chunk, 0)

    @pl.when(j == pl.num_programs(0) - 1)
    def _():
        st_ref[...] = s_scr[...]


def _rwkv_chunked(r, lw, k, v, kk, b, s0, *, chunks_per_step=16):
    B, T, W = r.shape
    H, d, L = RWKV_HEADS, HEAD_DIM, RWKV_CHUNK
    n_c = T // L
    group = _pick_tile(n_c, RWKV_CHUNKS_PER_STEP)
    seq = pl.BlockSpec((1, group * L, W), lambda i, c: (i, c, 0))
    per_tok = pl.BlockSpec((1, H, group * L, d), lambda i, c: (i, 0, c, 0))
    per_chunk = pl.BlockSpec((1, H, group, d, d), lambda i, c: (i, 0, c, 0, 0))
    qe, y0, pm, z = pl.pallas_call(
        _rwkv_chunk_body,
        grid=(B, n_c // group),
        in_specs=[seq] * 6,
        out_specs=[per_tok, per_tok, per_chunk, per_chunk],
        out_shape=[jax.ShapeDtypeStruct((B, H, T, d), jnp.float32)] * 2
                  + [jax.ShapeDtypeStruct((B, H, n_c, d, d), jnp.float32)] * 2,
        compiler_params=pltpu.CompilerParams(
            dimension_semantics=("parallel", "parallel"), vmem_limit_bytes=VMEM_LIMIT_BYTES),
        name="rwkv_chunk",
    )(r, lw, k, v, kk, b)
    cs = _pick_tile(n_c, chunks_per_step)
    tok = pl.BlockSpec((B, H, cs * L, d), lambda j: (0, 0, j, 0))
    chk = pl.BlockSpec((B, H, cs, d, d), lambda j: (0, 0, j, 0, 0))
    st = pl.BlockSpec((B, H, d, d), lambda j: (0, 0, 0, 0))
    y, s_t = pl.pallas_call(
        _rwkv_walk_body,
        grid=(n_c // cs,),
        in_specs=[tok, tok, chk, chk, st],
        out_specs=[tok, st],
        out_shape=[jax.ShapeDtypeStruct((B, H, T, d), jnp.float32), jax.ShapeDtypeStruct((B, H, d, d), jnp.float32)],
        scratch_shapes=[pltpu.VMEM((B, H, d, d), jnp.float32)],
        compiler_params=pltpu.CompilerParams(
            dimension_semantics=("arbitrary",), vmem_limit_bytes=VMEM_LIMIT_BYTES),
        name="rwkv_walk",
    )(qe, y0, pm, z, s0)
    return jnp.transpose(y, (0, 2, 1, 3)).reshape(B, T, W), s_t


SEL_CHUNK = 512
WIN_SPAN = WINDOW + Q_BLOCK


def _nsa_scores(k_aug, qT_ref):
    return [jnp.dot(k_aug, qT_ref[0, 0, r], preferred_element_type=jnp.float32) for r in range(NSA_GROUP)]


def _nsa_softmax_cols(s_list, neg):
    out = []
    for s in s_list:
        s = s + neg
        m = jnp.max(s, axis=0, keepdims=True)
        e = jnp.exp(s - m)
        inv = jnp.where(m > 0.5 * NEG, 1.0, 0.0) / jnp.maximum(jnp.sum(e, axis=0, keepdims=True), TINY)
        out.append(e * inv)
    return out


def _nsa_prompt_body(qT_ref, kc_ref, vcT_ref, ks_ref, vsT_ref, kw_ref, vwT_ref, covT_ref, o_ref,
                     sel_scr, m_scr, l_scr, acc_scr):
    i = pl.program_id(2)
    R = NSA_GROUP
    n_cmp = kc_ref.shape[2]
    n_sel = covT_ref.shape[0]
    q0 = i * Q_BLOCK
    pos = q0 + lax.broadcasted_iota(jnp.int32, (1, Q_BLOCK), 1)
    posf = pos.astype(jnp.float32)
    bf = jnp.bfloat16

    c_end = (lax.broadcasted_iota(jnp.int32, (n_cmp, Q_BLOCK), 0) * CMP_STRIDE + (CMP_LEN - 1)).astype(jnp.float32)
    neg_c = jnp.where(posf - c_end >= 0.0, 0.0, NEG)
    p_c = _nsa_softmax_cols(_nsa_scores(kc_ref[0, 0], qT_ref), neg_c)
    vcT = vcT_ref[0, 0]
    o_c = [jnp.dot(vcT, p.astype(bf), preferred_element_type=jnp.float32) for p in p_c]
    for r in range(R):
        o_ref[0, 0, 0, r] = o_c[r]
    p_sum = (p_c[0] + p_c[1]) + (p_c[2] + p_c[3])
    imp = jnp.dot(covT_ref[...], p_sum, preferred_element_type=jnp.float32, precision=lax.Precision.HIGHEST)
    blk = lax.broadcasted_iota(jnp.int32, (n_sel, Q_BLOCK), 0)
    avail = blk * SEL_LEN <= pos
    forced = (blk == jnp.right_shift(pos, 6)) | (blk == 0)
    imp = jnp.where(avail, jnp.where(forced, FORCE, imp), -FORCE)

    sub8 = lax.broadcasted_iota(jnp.int32, (8, Q_BLOCK), 0)
    for jj in range(n_sel // 8):
        vj = imp[8 * jj:8 * jj + 8]
        cnt = jnp.zeros((8, Q_BLOCK), jnp.float32)
        for k in range(n_sel):
            row = imp[k:k + 1]
            if k < 8 * jj:
                cnt = cnt + jnp.where(row >= vj, 1.0, 0.0)
            elif k >= 8 * jj + 8:
                cnt = cnt + jnp.where(row > vj, 1.0, 0.0)
            else:
                cnt = cnt + jnp.where(sub8 > (k - 8 * jj), jnp.where(row >= vj, 1.0, 0.0),
                                      jnp.where(row > vj, 1.0, 0.0))
        sel_scr[8 * jj:8 * jj + 8, :] = jnp.where(cnt < float(SEL_TOP), 1.0, 0.0)

    m_scr[...] = jnp.full_like(m_scr, NEG)
    l_scr[...] = jnp.zeros_like(l_scr)
    acc_scr[...] = jnp.zeros_like(acc_scr)
    key_iota = lax.broadcasted_iota(jnp.int32, (SEL_CHUNK, Q_BLOCK), 0).astype(jnp.float32)
    blocks_per_chunk = SEL_CHUNK // SEL_LEN

    def chunk(c, carry):
        k0 = pl.multiple_of(c * SEL_CHUNK, SEL_CHUNK)
        s_all = _nsa_scores(ks_ref[0, 0, pl.ds(k0, SEL_CHUNK), :], qT_ref)
        dist = (posf - k0.astype(jnp.float32)) - key_iota
        picked = jnp.concatenate(
            [jnp.broadcast_to(sel_scr[pl.ds(c * blocks_per_chunk + b, 1), :], (SEL_LEN, Q_BLOCK))
             for b in range(blocks_per_chunk)], axis=0)
        neg = jnp.where((picked > 0.5) & (dist >= 0.0), 0.0, NEG)
        m_old = [m_scr[r] for r in range(R)]
        m_new, p_all, l_add = [], [], []
        for r in range(R):
            s = s_all[r] + neg
            m = jnp.maximum(m_old[r], jnp.max(s, axis=0, keepdims=True))
            p = jnp.exp(s - m)
            m_new.append(m)
            l_add.append(jnp.sum(p, axis=0, keepdims=True))
            p_all.append(p.astype(bf))
        vsT = vsT_ref[0, 0, :, pl.ds(k0, SEL_CHUNK)]
        pv = [jnp.dot(vsT, p, preferred_element_type=jnp.float32) for p in p_all]
        for r in range(R):
            alpha = jnp.exp(m_old[r] - m_new[r])
            l_scr[r] = alpha * l_scr[r] + l_add[r]
            acc_scr[r] = alpha * acc_scr[r] + pv[r]
            m_scr[r] = m_new[r]
        return carry

    lax.fori_loop(0, q0 // SEL_CHUNK + 1, chunk, 0)
    for r in range(R):
        o_ref[1, 0, 0, r] = acc_scr[r] / jnp.maximum(l_scr[r], TINY)

    k0w = pl.multiple_of(jnp.maximum(q0 - WINDOW, 0), Q_BLOCK)
    d_win = (posf - k0w.astype(jnp.float32)) - lax.broadcasted_iota(jnp.int32, (WIN_SPAN, Q_BLOCK), 0).astype(jnp.float32)
    neg_w = jnp.where((d_win >= 0.0) & (d_win < float(WINDOW)), 0.0, NEG)
    p_w = _nsa_softmax_cols(_nsa_scores(kw_ref[0, 0, pl.ds(k0w, WIN_SPAN), :], qT_ref), neg_w)
    vwT = vwT_ref[0, 0, :, pl.ds(k0w, WIN_SPAN)]
    o_w = [jnp.dot(vwT, p.astype(bf), preferred_element_type=jnp.float32) for p in p_w]
    for r in range(R):
        o_ref[2, 0, 0, r] = o_w[r]


def _nsa_prompt_attn(q, kc, vc, ks, vs, kw, vw):
    B, T, H, d = q.shape
    G, R = NSA_KV_HEADS, NSA_GROUP
    n_c = kc.shape[1]
    n_cmp = -(-n_c // 128) * 128
    n_sel = T // SEL_LEN
    bf = jnp.bfloat16
    lanes = 128

    def key_rows(k, key_pos):
        n = k.shape[1]
        extra = jnp.zeros((n, lanes - d), jnp.float32).at[:, 0].set((key_pos // SEL_LEN).astype(jnp.float32))
        extra = extra.at[:, 1].set((key_pos % SEL_LEN).astype(jnp.float32))
        extra = jnp.broadcast_to(extra[None, None], (B, G, n, lanes - d))
        return jnp.concatenate([jnp.transpose(k, (0, 2, 1, 3)), extra], axis=-1).astype(bf)

    cols = lambda t: jnp.transpose(t, (0, 2, 3, 1)).astype(bf)
    slopes = (2.0 ** -jnp.arange(1, H + 1, dtype=jnp.float32)).reshape(G, R)
    q_extra = jnp.zeros((G, R, lanes - d), jnp.float32).at[:, :, 0].set(SEL_LEN * slopes).at[:, :, 1].set(slopes)
    qT = jnp.transpose((q * (d ** -0.5)).reshape(B, T, G, R, d), (0, 2, 3, 4, 1))
    qT = jnp.concatenate([qT, jnp.broadcast_to(q_extra[None, :, :, :, None], (B, G, R, lanes - d, T))],
                         axis=3).astype(bf)
    pad_c = ((0, 0), (0, n_cmp - n_c), (0, 0), (0, 0))
    kc_r = key_rows(jnp.pad(kc, pad_c), jnp.arange(n_cmp, dtype=jnp.int32) * CMP_STRIDE + (CMP_LEN - 1))
    vc_c = cols(jnp.pad(vc, pad_c))
    tok = jnp.arange(T, dtype=jnp.int32)
    rows = lambda t: key_rows(t, tok)
    c_start = jnp.arange(n_cmp, dtype=jnp.int32) * CMP_STRIDE
    s_start = jnp.arange(n_sel, dtype=jnp.int32) * SEL_LEN
    covT = jnp.maximum(jnp.minimum(c_start[None, :] + CMP_LEN, s_start[:, None] + SEL_LEN)
                       - jnp.maximum(c_start[None, :], s_start[:, None]), 0).astype(jnp.float32) / CMP_LEN
    full_r = lambda n: pl.BlockSpec((1, 1, n, lanes), lambda b, g, i: (b, g, 0, 0))
    full_c = lambda n: pl.BlockSpec((1, 1, d, n), lambda b, g, i: (b, g, 0, 0))
    oT = pl.pallas_call(
        _nsa_prompt_body,
        grid=(B, G, T // Q_BLOCK),
        in_specs=[pl.BlockSpec((1, 1, R, lanes, Q_BLOCK), lambda b, g, i: (b, g, 0, 0, i)),
                  full_r(n_cmp), full_c(n_cmp), full_r(T), full_c(T), full_r(T), full_c(T),
                  pl.BlockSpec((n_sel, n_cmp), lambda b, g, i: (0, 0))],
        out_specs=pl.BlockSpec((3, 1, 1, R, d, Q_BLOCK), lambda b, g, i: (0, b, g, 0, 0, i)),
        out_shape=jax.ShapeDtypeStruct((3, B, G, R, d, T), jnp.float32),
        scratch_shapes=[pltpu.VMEM((n_sel, Q_BLOCK), jnp.float32),
                        pltpu.VMEM((R, 1, Q_BLOCK), jnp.float32),
                        pltpu.VMEM((R, 1, Q_BLOCK), jnp.float32),
                        pltpu.VMEM((R, d, Q_BLOCK), jnp.float32)],
        compiler_params=pltpu.CompilerParams(
            dimension_semantics=("parallel", "parallel", "arbitrary"), vmem_limit_bytes=VMEM_LIMIT_BYTES),
        name="nsa_prompt",
    )(qT, kc_r, vc_c, rows(ks), cols(vs), rows(kw), cols(vw), covT)
    return jnp.transpose(oT, (0, 1, 5, 2, 3, 4)).reshape(3, B, T, H, d)


PAGES_PER_STEP = 8


def _sel_pages_body(pt_ref, q_ref, bias_ref, bias_new_ref, *rest):
    page_refs = rest[:PAGES_PER_STEP]
    new_ref, o_ref, m_scr, l_scr, acc_scr = rest[PAGES_PER_STEP:]
    j = pl.program_id(1)
    G, d = NSA_KV_HEADS, HEAD_DIM
    P = new_ref.shape[1]
    lane = lax.broadcasted_iota(jnp.int32, (8, 2 * d), 1)
    group_cols = [jnp.where((lane >= g * d) & (lane < (g + 1) * d), 1.0, 0.0) for g in range(G)]

    @pl.when(j == 0)
    def _():
        m_scr[...] = jnp.full_like(m_scr, NEG)
        l_scr[...] = jnp.zeros_like(l_scr)
        acc_scr[...] = jnp.zeros_like(acc_scr)

    def update(pages, biases):
        keys = [p[:, :2 * d] for p in pages]
        k_bf = [k.astype(jnp.bfloat16) for k in keys]
        v_bf = [p[:, 2 * d:].astype(jnp.bfloat16) for p in pages]
        sq = [k * k for k in keys]
        for g in range(G):
            q = q_ref[0, g]
            s = []
            for i in range(len(pages)):
                ssq = _dot_t(group_cols[g], sq[i])[0:1]
                raw = lax.dot_general(q, k_bf[i], _NT, preferred_element_type=jnp.float32)
                s.append(raw * lax.rsqrt(ssq * (1.0 / d) + EPS) + biases[g][i])
            m_old = m_scr[g]
            m_new = m_old
            for x in s:
                m_new = jnp.maximum(m_new, jnp.max(x, axis=1, keepdims=True))
            alpha = jnp.exp(m_old - m_new)
            l_new, acc = alpha * l_scr[g], alpha * acc_scr[g]
            for i, x in enumerate(s):
                p = jnp.exp(x - m_new)
                l_new = l_new + jnp.sum(p, axis=1, keepdims=True)
                acc = acc + jnp.dot(p.astype(jnp.bfloat16), v_bf[i], preferred_element_type=jnp.float32)
            m_scr[g], l_scr[g], acc_scr[g] = m_new, l_new, acc

    update([r[...] for r in page_refs],
           [[bias_ref[0, g, :, i * P:(i + 1) * P] for i in range(PAGES_PER_STEP)] for g in range(G)])

    @pl.when(j == pl.num_programs(1) - 1)
    def _():
        update([new_ref[0]], [[bias_new_ref[0, g]] for g in range(G)])
        for g in range(G):
            o_ref[0, g] = acc_scr[g] / jnp.maximum(l_scr[g], TINY)


def _sel_pages_attn(q_rows, bias, bias_new, pool, pages, new_rows):
    B, G, Q, _ = q_rows.shape
    P, W = pool.shape[1], pool.shape[2]
    n_pages = pages.shape[0] // B
    steps = n_pages // PAGES_PER_STEP
    page_spec = lambda k: pl.BlockSpec((None, P, W), lambda b, j, pt: (pt[b * n_pages + j * PAGES_PER_STEP + k], 0, 0))
    grid_spec = pltpu.PrefetchScalarGridSpec(
        num_scalar_prefetch=1,
        grid=(B, steps),
        in_specs=[pl.BlockSpec((1, G, Q, 2 * HEAD_DIM), lambda b, j, pt: (b, 0, 0, 0)),
                  pl.BlockSpec((1, G, Q, PAGES_PER_STEP * P), lambda b, j, pt: (b, 0, 0, j)),
                  pl.BlockSpec((1, G, Q, P), lambda b, j, pt: (b, 0, 0, 0))]
                 + [page_spec(k) for k in range(PAGES_PER_STEP)]
                 + [pl.BlockSpec((1, P, W), lambda b, j, pt: (b, 0, 0))],
        out_specs=pl.BlockSpec((1, G, Q, 2 * HEAD_DIM), lambda b, j, pt: (b, 0, 0, 0)),
        scratch_shapes=[pltpu.VMEM((G, Q, 1), jnp.float32), pltpu.VMEM((G, Q, 1), jnp.float32),
                        pltpu.VMEM((G, Q, 2 * HEAD_DIM), jnp.float32)],
    )
    return pl.pallas_call(
        _sel_pages_body,
        grid_spec=grid_spec,
        out_shape=jax.ShapeDtypeStruct((B, G, Q, 2 * HEAD_DIM), jnp.float32),
        compiler_params=pltpu.CompilerParams(
            dimension_semantics=("parallel", "arbitrary"), vmem_limit_bytes=VMEM_LIMIT_BYTES),
        name="sel_pages",
    )(pages, q_rows, bias, bias_new, *([pool] * PAGES_PER_STEP), new_rows)


def _compress_pages_body(pt_ref, *rest):
    page_refs = rest[:PAGES_PER_STEP]
    w1_ref, c0_ref, w2_ref, ones_ref, gain_ref, o_ref, x_scr = rest[PAGES_PER_STEP:]
    j = pl.program_id(1)
    rows = page_refs[0].shape[0]
    for i, ref in enumerate(page_refs):
        x_scr[pl.ds(pl.multiple_of((j * PAGES_PER_STEP + i) * rows, rows), rows), :] = ref[...]

    @pl.when(j == pl.num_programs(1) - 1)
    def _():
        n, half = x_scr.shape[0], w2_ref.shape[0]
        ab = jnp.dot(x_scr[...].astype(jnp.bfloat16), w1_ref[...], preferred_element_type=jnp.float32)
        top, bottom = ab[:, :half], ab[:, half:]
        nxt = jnp.concatenate([bottom[1:], jnp.zeros((1, half), jnp.float32)], axis=0)
        hid = jax.nn.gelu(top + nxt + c0_ref[...])
        y = jnp.dot(hid.astype(jnp.bfloat16), w2_ref[...], preferred_element_type=jnp.float32)
        mean_sq = jnp.dot(y * y, ones_ref[...], preferred_element_type=jnp.float32, precision=lax.Precision.HIGHEST)
        is_key = lax.broadcasted_iota(jnp.int32, (n, half), 1) < half // 2
        o_ref[0] = jnp.where(is_key, y * lax.rsqrt(mean_sq + EPS) * gain_ref[...], y)


def _compress_pages(pool, pages, n_batch, pe, w1, w2, k_gain):
    d, G = HEAD_DIM, NSA_KV_HEADS
    N, page_rows, W = pool.shape
    cpp = page_rows // CMP_STRIDE
    n_pages = pages.shape[0] // n_batch
    halves = CMP_LEN // CMP_STRIDE
    w1r = w1.reshape(2, halves, CMP_STRIDE, d, d)[jnp.array([0] * G + [1] * G)]
    eye = jnp.eye(2 * G, dtype=jnp.float32)
    w_big = jnp.transpose(w1r, (2, 0, 3, 1, 4))[:, :, :, :, None, :] * eye[None, :, None, None, :, None]
    w_big = w_big.reshape(CMP_STRIDE * W, halves * W).astype(jnp.bfloat16)
    c0 = jnp.einsum('kn,kne->ke', pe.reshape(2, CMP_LEN * d), w1, precision=lax.Precision.HIGHEST)
    c0 = jnp.repeat(c0, G, axis=0).reshape(1, W)
    w2_big = (w2[jnp.array([0] * G + [1] * G)][:, :, None, :] * eye[:, None, :, None]).reshape(W, W).astype(jnp.bfloat16)
    ones_blk = jnp.kron(eye, jnp.full((d, d), 1.0 / d, jnp.float32))
    gain = jnp.concatenate([jnp.tile(k_gain, G), jnp.ones((G * d,), jnp.float32)]).reshape(1, W)
    pool_chunks = pool.reshape(N, cpp, CMP_STRIDE * W)
    page_spec = lambda k: pl.BlockSpec((None, cpp, CMP_STRIDE * W),
                                       lambda b, j, pt: (pt[b * n_pages + j * PAGES_PER_STEP + k], 0, 0))
    const = lambda shape: pl.BlockSpec(shape, lambda b, j, pt: (0, 0))
    grid_spec = pltpu.PrefetchScalarGridSpec(
        num_scalar_prefetch=1,
        grid=(n_batch, n_pages // PAGES_PER_STEP),
        in_specs=[page_spec(k) for k in range(PAGES_PER_STEP)]
                 + [const(w_big.shape), const((1, W)), const((W, W)), const((W, W)), const((1, W))],
        out_specs=pl.BlockSpec((1, n_pages * cpp, W), lambda b, j, pt: (b, 0, 0)),
        scratch_shapes=[pltpu.VMEM((n_pages * cpp, CMP_STRIDE * W), jnp.float32)],
    )
    return pl.pallas_call(
        _compress_pages_body,
        grid_spec=grid_spec,
        out_shape=jax.ShapeDtypeStruct((n_batch, n_pages * cpp, W), jnp.float32),
        compiler_params=pltpu.CompilerParams(
            dimension_semantics=("parallel", "arbitrary"), vmem_limit_bytes=VMEM_LIMIT_BYTES),
        name="compress_pages",
    )(pages, *([pool_chunks] * PAGES_PER_STEP), w_big, c0, w2_big, ones_blk, gain)


def _rms_norm(x, g):
    xf = x.astype(jnp.float32)
    y = xf * lax.rsqrt(jnp.mean(xf * xf, axis=-1, keepdims=True) + EPS)
    return (y * g.astype(jnp.float32)).astype(x.dtype)


def _group_norm(x, g, eps):
    xf = x.astype(jnp.float32)
    mu = jnp.mean(xf, axis=-1, keepdims=True)
    var = jnp.mean(jnp.square(xf - mu), axis=-1, keepdims=True)
    return (xf - mu) * lax.rsqrt(var + eps) * g.astype(jnp.float32)


def _masked_softmax(s, mask):
    s = jnp.where(mask, s, NEG)
    m = jnp.max(s, axis=-1, keepdims=True)
    e = jnp.where(mask, jnp.exp(s - m), 0.0)
    return e / jnp.maximum(jnp.sum(e, axis=-1, keepdims=True), TINY)


def _alibi_slopes(n):
    return 2.0 ** (-8.0 * jnp.arange(1, n + 1, dtype=jnp.float32) / n)


def _rotary(x, pos):
    half = x.shape[-1] // 2
    freqs = ROPE_BASE ** (-jnp.arange(half, dtype=jnp.float32) / half)
    ang = pos.astype(jnp.float32)[:, None] * freqs[None, :]
    cos, sin = jnp.cos(ang)[None, :, None, :], jnp.sin(ang)[None, :, None, :]
    x1, x2 = x[..., :half], x[..., half:]
    return jnp.concatenate([x1 * cos - x2 * sin, x1 * sin + x2 * cos], axis=-1)


def _nsa_compress(rows, pe, w1, w2):
    B, T, G, d = rows.shape
    n_cmp = (T - CMP_LEN) // CMP_STRIDE + 1
    idx = (jnp.arange(n_cmp, dtype=jnp.int32) * CMP_STRIDE)[:, None] + jnp.arange(CMP_LEN, dtype=jnp.int32)[None, :]
    blk = rows[:, idx] + pe[None, None, :, None, :]
    blk = jnp.transpose(blk, (0, 1, 3, 2, 4)).reshape(B, n_cmp, G, CMP_LEN * d)
    return jax.nn.gelu(blk @ w1) @ w2


def _to_sel_blocks(rows):
    B, T, G, d = rows.shape
    n_sel = -(-T // SEL_LEN)
    rows = jnp.pad(rows, ((0, 0), (0, n_sel * SEL_LEN - T), (0, 0), (0, 0)))
    return jnp.transpose(rows.reshape(B, n_sel, SEL_LEN, G, d), (0, 3, 1, 2, 4))


def _nsa_branches(q, pos, kc, vc, n_sel, sel_branch, kw, vw, pos_w, slopes):
    B, Tq, H, d = q.shape
    G, R = NSA_KV_HEADS, NSA_GROUP
    scale = d ** -0.5
    qg = q.reshape(B, Tq, G, R, d)
    sl = slopes.reshape(G, R)
    posf = pos.astype(jnp.float32)
    n_cmp = kc.shape[1]
    c_start = jnp.arange(n_cmp, dtype=jnp.int32) * CMP_STRIDE
    d_cmp = posf[:, None] - (c_start + CMP_LEN - 1).astype(jnp.float32)[None, :]
    s = jnp.einsum('btgrd,bngd->bgrtn', qg, kc).astype(jnp.float32) * scale - sl[None, :, :, None, None] * d_cmp
    p_cmp = _masked_softmax(s, (d_cmp >= 0.0)[None, None, None])
    o_cmp = jnp.einsum('bgrtn,bngd->btgrd', p_cmp.astype(vc.dtype), vc)
    s_start = jnp.arange(n_sel, dtype=jnp.int32) * SEL_LEN
    cover = jnp.maximum(jnp.minimum(c_start[:, None] + CMP_LEN, s_start[None, :] + SEL_LEN)
                        - jnp.maximum(c_start[:, None], s_start[None, :]), 0).astype(jnp.float32) / CMP_LEN
    imp = jnp.einsum('bgrtn,nj->bgtj', p_cmp, cover)
    blk = jnp.arange(n_sel, dtype=jnp.int32)
    avail = s_start[None, :] <= pos[:, None]
    forced = (blk[None, :] == (pos // SEL_LEN)[:, None]) | (blk[None, :] == 0)
    imp = jnp.where(avail, jnp.where(forced, FORCE, imp), -FORCE)
    n_top = min(SEL_TOP, n_sel)
    _, idx = lax.top_k(imp, n_top)
    o_sel = sel_branch(qg, idx)
    d_win = posf[:, None] - pos_w.astype(jnp.float32)[None, :]
    s = jnp.einsum('btgrd,bwgd->bgrtw', qg, kw).astype(jnp.float32) * scale - sl[None, :, :, None, None] * d_win
    win_mask = (d_win >= 0.0) & (d_win < WINDOW) & (pos_w >= 0)[None, :]
    p_win = _masked_softmax(s, win_mask[None, None, None])
    o_win = jnp.einsum('bgrtw,bwgd->btgrd', p_win.astype(vw.dtype), vw)
    return jnp.stack([o_cmp, o_sel, o_win]).reshape(3, B, Tq, H, d)


def _rwkv_group(u, shift0, S0, lp):
    B, T, _ = u.shape
    W = RWKV_WIDTH
    uf = u.astype(jnp.float32)
    prev = jnp.concatenate([shift0.astype(jnp.float32)[:, None], uf[:, :-1]], axis=1)
    um = uf + (prev - uf) * lp['rwkv_mu']
    r, k, v = um[..., :W], um[..., W:2 * W], um[..., 2 * W:3 * W]
    o = 3 * W
    wd = um[..., o:o + RWKV_W_RANK]
    ad = um[..., o + RWKV_W_RANK:o + RWKV_W_RANK + RWKV_A_RANK]
    gd = um[..., o + RWKV_W_RANK + RWKV_A_RANK:]
    w = lp['rwkv_w0'] + jnp.tanh(wd) @ lp['rwkv_w_up']
    log_decay = -jnp.exp(-jax.nn.softplus(-w) - 0.5)
    a = jax.nn.sigmoid(lp['rwkv_a0'] + ad @ lp['rwkv_a_up'])
    g = jax.nn.sigmoid(gd) @ lp['rwkv_g_up']
    kk = k * lp['rwkv_k_k']
    k = k * (1.0 + (a - 1.0) * lp['rwkv_k_a'])
    hd = lambda t: t.reshape(B, T, RWKV_HEADS, HEAD_DIM).astype(jnp.float32)
    r, k, v, a, g, kk = hd(r), hd(k), hd(v), hd(a), hd(g), hd(kk)
    kk = kk / jnp.maximum(jnp.sqrt(jnp.sum(kk * kk, axis=-1, keepdims=True)), 1e-12)
    flat = lambda t: t.reshape(B, T, W)
    if T % RWKV_CHUNK == 0:
        ys, S_T = _rwkv_chunked(flat(r), log_decay, flat(k), flat(v), flat(kk), flat(kk * a), S0.astype(jnp.float32))
    else:
        ys, S_T = _rwkv_scan(flat(r), jnp.exp(log_decay), flat(k), flat(v), flat(kk), flat(kk * a),
                             S0.astype(jnp.float32))
    y = _group_norm(ys.reshape(B, T, RWKV_HEADS, HEAD_DIM), lp['rwkv_ln_g'], RWKV_GN_EPS) + lp['rwkv_ln_b']
    y = y + jnp.sum(r * k * lp['rwkv_r_k'], axis=-1, keepdims=True) * v
    y = y * g
    return y.reshape(B, T, W).astype(u.dtype), u[:, -1], S_T


def _retention_group(u, pos, S0, ln_g):
    B, T, _ = u.shape
    uf = u.astype(jnp.float32)
    q, k, v, g = [t.reshape(B, T, RET_HEADS, HEAD_DIM) for t in jnp.split(uf, 4, axis=-1)]
    q = _rotary(q, pos)
    k = _rotary(k, pos) * HEAD_DIM ** -0.5
    lg = jnp.log(1.0 - 2.0 ** (-5.0 - jnp.arange(RET_HEADS, dtype=jnp.float32)))
    C = RET_CHUNK if T % RET_CHUNK == 0 else T
    nC = T // C
    n = jnp.arange(C, dtype=jnp.float32)
    diff = n[:, None] - n[None, :]
    dmask = jnp.where(diff[None] >= 0, jnp.exp(jnp.maximum(diff, 0.0)[None] * lg[:, None, None]), 0.0)
    q_dec = jnp.exp((n[:, None] + 1.0) * lg[None, :])
    k_dec = jnp.exp((C - 1.0 - n)[:, None] * lg[None, :])
    s_dec = jnp.exp(C * lg)

    def chunk(S, inp):
        qc, kc, vc = inp
        att = jnp.einsum('bnhd,bmhd->bhnm', qc, kc) * dmask
        out = jnp.einsum('bhnm,bmhe->bnhe', att, vc) + jnp.einsum('bnhd,bhde->bnhe', qc, S) * q_dec[None, :, :, None]
        S = S * s_dec[None, :, None, None] + jnp.einsum('bmhd,bmhe->bhde', kc * k_dec[None, :, :, None], vc)
        return S, out

    to_chunks = lambda t: jnp.moveaxis(t.reshape(B, nC, C, RET_HEADS, HEAD_DIM), 1, 0)
    S_T, o = lax.scan(chunk, S0.astype(jnp.float32), (to_chunks(q), to_chunks(k), to_chunks(v)))
    o = jnp.moveaxis(o, 0, 1).reshape(B, T, RET_HEADS, HEAD_DIM)
    y = jax.nn.silu(g) * _group_norm(o, ln_g, GN_EPS)
    return y.reshape(B, T, RET_WIDTH).astype(u.dtype), S_T


def _mixing_sublayer(x, lp, past_len, past_cmp, past_sel, win_buf, rwkv_S, rwkv_shift, ret_S):
    B, T, _ = x.shape
    h = _rms_norm(x, lp['norm_attn'])
    P = h @ lp['w_in']
    c = P[..., :NSA_IN]
    q = _rms_norm(c[..., :NSA_WIDTH].reshape(B, T, NSA_HEADS, HEAD_DIM), lp['nsa_q_norm'])
    kv = c[..., NSA_WIDTH:NSA_WIDTH + 6 * NSA_KV_COLS].reshape(B, T, 3, 2, NSA_KV_HEADS, HEAD_DIM)
    kv_cmp, kv_sel, kv_win = kv[:, :, 0], kv[:, :, 1], kv[:, :, 2]
    gates = jax.nn.sigmoid(c[..., NSA_WIDTH + 6 * NSA_KV_COLS:].astype(jnp.float32)).reshape(B, T, NSA_HEADS, 3)
    pos = past_len + jnp.arange(T, dtype=jnp.int32)
    slopes = _alibi_slopes(NSA_HEADS)
    prompt = past_cmp is None
    if prompt:
        rows_cmp, rows_sel, rows_win = kv_cmp, kv_sel, kv_win
        new_win = kv_win[:, T - min(WINDOW, T):]
    else:
        rows_win = jnp.concatenate([win_buf, kv_win.astype(win_buf.dtype)], axis=1)
        new_win = rows_win[:, T:]
    k_norm = lp['nsa_k_norm']
    if prompt:
        kc = _rms_norm(_nsa_compress(rows_cmp[:, :, 0], lp['nsa_cmp_pe'][0], lp['nsa_cmp_w1'][0], lp['nsa_cmp_w2'][0]), k_norm[0])
        vc = _nsa_compress(rows_cmp[:, :, 1], lp['nsa_cmp_pe'][1], lp['nsa_cmp_w1'][1], lp['nsa_cmp_w2'][1])
    else:
        n_cmp = (past_len + T - CMP_LEN) // CMP_STRIDE + 1
        assert (n_cmp - 1) * CMP_STRIDE + CMP_LEN <= past_len
        kcv = _compress_pages(past_cmp[0], past_cmp[1], B, lp['nsa_cmp_pe'], lp['nsa_cmp_w1'], lp['nsa_cmp_w2'], k_norm[0])
        kcv = kcv[:, :n_cmp].reshape(B, n_cmp, 2, NSA_KV_HEADS, HEAD_DIM)
        kc, vc = kcv[:, :, 0], kcv[:, :, 1]
    kw = _rms_norm(rows_win[:, :, 0], k_norm[2])
    vw = rows_win[:, :, 1]
    if prompt:
        o3 = _nsa_prompt_attn(q, kc, vc, _rms_norm(rows_sel[:, :, 0], k_norm[1]), rows_sel[:, :, 1], kw, vw)
    else:
        G, R, d = NSA_KV_HEADS, NSA_GROUP, HEAD_DIM
        rows_sel = jnp.concatenate([past_sel, kv_sel.astype(past_sel.dtype)], axis=1)
        ks_blk = _to_sel_blocks(_rms_norm(rows_sel[:, :, 0], k_norm[1]))
        vs_blk = _to_sel_blocks(rows_sel[:, :, 1])
        n_sel = ks_blk.shape[2]
        sl = slopes.reshape(G, R)

        def sel_branch(qg, idx):
            b_i = jnp.arange(B)[:, None, None, None]
            g_i = jnp.arange(G)[None, :, None, None]
            ks_g = ks_blk[b_i, g_i, idx]
            vs_g = vs_blk[b_i, g_i, idx]
            kpos = idx[..., None] * SEL_LEN + jnp.arange(SEL_LEN, dtype=jnp.int32)
            d_sel = (pos[None, None, :, None, None] - kpos).astype(jnp.float32)[:, :, None]
            s = (jnp.einsum('btgrd,bgtnsd->bgrtns', qg, ks_g).astype(jnp.float32) * (d ** -0.5)
                 - sl[None, :, :, None, None, None] * d_sel)
            mask = jnp.broadcast_to(d_sel >= 0.0, s.shape)
            p_sel = _masked_softmax(s.reshape(B, G, R, T, -1), mask.reshape(B, G, R, T, -1)).reshape(s.shape)
            return jnp.einsum('bgrtns,bgtnsd->btgrd', p_sel.astype(vs_g.dtype), vs_g)

        wb = win_buf.shape[1]
        pos_w = past_len - wb + jnp.arange(wb + T, dtype=jnp.int32)
        o3 = _nsa_branches(q, pos, kc, vc, n_sel, sel_branch, kw, vw, pos_w, slopes)
    o_nsa = jnp.einsum('btha,abthd->bthd', gates.astype(o3.dtype), o3)
    o_nsa = _rms_norm(o_nsa, lp['nsa_out_norm']).reshape(B, T, NSA_WIDTH)
    y_rwkv, new_shift, new_rwkv = _rwkv_group(P[..., NSA_IN:NSA_IN + RWKV_IN], rwkv_shift, rwkv_S, lp)
    y_ret, new_ret = _retention_group(P[..., NSA_IN + RWKV_IN:], pos, ret_S, lp['ret_ln_g'])
    mix = jnp.concatenate([o_nsa, y_rwkv.astype(o_nsa.dtype), y_ret.astype(o_nsa.dtype)], axis=-1)
    x = x + mix @ lp['w_out']
    return x, (kv_cmp, kv_sel, new_win, new_rwkv, new_shift, new_ret)


FFN_ROW_TILE = 512


def _dense_ffn(x, g, wg, wu, wd):
    shp = x.shape
    x2 = x.reshape(-1, shp[-1])
    tm = _pick_tile(x2.shape[0], FFN_ROW_TILE)
    n_tiles = x2.shape[0] // tm
    ones = jnp.ones((x2.shape[0], 1), jnp.float32)
    y = _ffn(x2, g, ones, wg[None], wu[None], wd[None], jnp.zeros((n_tiles,), jnp.int32),
             jnp.ones((n_tiles,), jnp.int32), tm=tm)
    return (x2 + y).reshape(shp)


def _moe_ffn(xs, g, router, wg, wu, wd):
    D = xs[0].shape[-1]
    E, tm = router.shape[1], FFN_ROW_TILE
    flat = [x.reshape(-1, D) for x in xs]
    x2 = jnp.concatenate(flat, axis=0)
    logits = jnp.concatenate([_router_logits(x, g, router) for x in flat], axis=0)
    top_val, top_idx = lax.top_k(logits, TOP_K)
    gate = jax.nn.softmax(top_val, axis=-1)
    N = x2.shape[0]
    A = N * TOP_K
    flat_e, flat_w = top_idx.reshape(A).astype(jnp.int32), gate.reshape(A)
    order = jnp.argsort(flat_e, stable=True).astype(jnp.int32)
    counts = jnp.sum(jax.nn.one_hot(flat_e, E, dtype=jnp.int32), axis=0)
    padded = (counts + tm - 1) // tm * tm
    start, p_end = jnp.cumsum(counts) - counts, jnp.cumsum(padded)
    p_start = p_end - padded
    sorted_e = flat_e[order]
    dest = p_start[sorted_e] + (jnp.arange(A, dtype=jnp.int32) - start[sorted_e])
    P = -(-(A + E * (tm - 1)) // tm) * tm
    row_tok = jnp.zeros((P,), jnp.int32).at[dest].set(order // TOP_K)
    row_w = jnp.zeros((P,), jnp.float32).at[dest].set(flat_w[order])
    tile_start = jnp.arange(P // tm, dtype=jnp.int32) * tm
    tile_e = jnp.minimum(jnp.searchsorted(p_end, tile_start, side='right'), E - 1).astype(jnp.int32)
    tile_used = (tile_start < p_end[-1]).astype(jnp.int32)
    ys = _ffn(x2[row_tok], g, row_w[:, None], wg, wu, wd, tile_e, tile_used, tm=tm)
    slot = jnp.zeros((A,), jnp.int32).at[order].set(dest).reshape(N, TOP_K)
    y = x2 + ys[slot[:, 0]] + ys[slot[:, 1]]
    outs, off = [], 0
    for x in xs:
        n = x.size // D
        outs.append(y[off:off + n].reshape(x.shape))
        off += n
    return outs


def kernel(x_prompt, x_sample, cache_nsa_cmp, cache_nsa_sel, cache_nsa_win, state_rwkv, state_rwkv_shift,
           state_ret, page_table, norm_attn, norm_ffn, w_in, w_out, nsa_q_norm, nsa_k_norm, nsa_cmp_pe,
           nsa_cmp_w1, nsa_cmp_w2, nsa_out_norm, rwkv_mu, rwkv_w0, rwkv_w_up, rwkv_a0, rwkv_a_up, rwkv_g_up,
           rwkv_k_k, rwkv_k_a, rwkv_r_k, rwkv_ln_g, rwkv_ln_b, ret_ln_g, ffn_w_gate, ffn_w_up, ffn_w_down,
           moe_router, moe_w_gate, moe_w_up, moe_w_down):
    Bp, Bs = x_prompt.shape[0], x_sample.shape[0]
    past_len = page_table.shape[1] * cache_nsa_cmp.shape[2]
    xp, xs = x_prompt, x_sample
    outs_p, outs_s = [], []
    for l in range(DEPTH):
        lp = {'norm_attn': norm_attn[l], 'w_in': w_in[l], 'w_out': w_out[l], 'nsa_q_norm': nsa_q_norm[l],
              'nsa_k_norm': nsa_k_norm[l], 'nsa_cmp_pe': nsa_cmp_pe[l], 'nsa_cmp_w1': nsa_cmp_w1[l],
              'nsa_cmp_w2': nsa_cmp_w2[l], 'nsa_out_norm': nsa_out_norm[l], 'rwkv_mu': rwkv_mu[l],
              'rwkv_w0': rwkv_w0[l], 'rwkv_w_up': rwkv_w_up[l], 'rwkv_a0': rwkv_a0[l], 'rwkv_a_up': rwkv_a_up[l],
              'rwkv_g_up': rwkv_g_up[l], 'rwkv_k_k': rwkv_k_k[l], 'rwkv_k_a': rwkv_k_a[l], 'rwkv_r_k': rwkv_r_k[l],
              'rwkv_ln_g': rwkv_ln_g[l], 'rwkv_ln_b': rwkv_ln_b[l], 'ret_ln_g': ret_ln_g[l]}
        xp, st = _mixing_sublayer(xp, lp, 0, None, None, None,
                                  jnp.zeros((Bp, RWKV_HEADS, HEAD_DIM, HEAD_DIM), jnp.float32),
                                  jnp.zeros((Bp, RWKV_IN), xp.dtype),
                                  jnp.zeros((Bp, RET_HEADS, HEAD_DIM, HEAD_DIM), jnp.float32))
        outs_p.append(st)
        n_pool, page_rows = cache_nsa_sel.shape[1], cache_nsa_sel.shape[2]
        pages = (page_table + l * n_pool).reshape(-1)
        past_cmp = (cache_nsa_cmp.reshape(DEPTH * n_pool, page_rows, -1), pages)
        past_sel = cache_nsa_sel[l][page_table].reshape(Bs, past_len, 2, NSA_KV_HEADS, HEAD_DIM)
        xs, st = _mixing_sublayer(xs, lp, past_len, past_cmp, past_sel, cache_nsa_win[l], state_rwkv[l],
                                  state_rwkv_shift[l], state_ret[l])
        outs_s.append(st)
        i = l // 2
        if l % 2 == 0:
            xp = _dense_ffn(xp, norm_ffn[l], ffn_w_gate[i], ffn_w_up[i], ffn_w_down[i])
            xs = _dense_ffn(xs, norm_ffn[l], ffn_w_gate[i], ffn_w_up[i], ffn_w_down[i])
        else:
            xp, xs = _moe_ffn([xp, xs], norm_ffn[l], moe_router[i], moe_w_gate[i], moe_w_up[i], moe_w_down[i])
    kv_cmp_p, kv_sel_p, win_p, rwkv_p, shift_p, ret_p = [jnp.stack([o[j] for o in outs_p]) for j in range(6)]
    kv_cmp_s, kv_sel_s, win_s, rwkv_s, shift_s, ret_s = [jnp.stack([o[j] for o in outs_s]) for j in range(6)]
    return (xp, xs, kv_cmp_p, kv_sel_p, win_p, rwkv_p, shift_p, ret_p,
            kv_cmp_s, kv_sel_s, win_s, rwkv_s, shift_s, ret_s)
```

```python
import functools

import jax
import jax.numpy as jnp
from jax import lax
from jax.experimental import pallas as pl
from jax.experimental.pallas import tpu as pltpu

D_MODEL = 1024
DEPTH = 2
HEAD_DIM = 64
NSA_WIDTH = D_MODEL // 2
RWKV_WIDTH = D_MODEL // 4
RET_WIDTH = D_MODEL - NSA_WIDTH - RWKV_WIDTH
NSA_HEADS = NSA_WIDTH // HEAD_DIM
NSA_KV_HEADS = 2
NSA_GROUP = NSA_HEADS // NSA_KV_HEADS
CMP_LEN = 32
CMP_STRIDE = 16
SEL_LEN = 64
SEL_TOP = 16
WINDOW = 512
Q_BLOCK = 128
RWKV_HEADS = RWKV_WIDTH // HEAD_DIM
RWKV_W_RANK = 64
RWKV_A_RANK = 64
RWKV_G_RANK = 128
RWKV_GN_EPS = 64e-5
RET_HEADS = RET_WIDTH // HEAD_DIM
RET_CHUNK = 128
ROPE_BASE = 10000.0
N_EXPERTS = 8
TOP_K = 2
NSA_KV_COLS = NSA_KV_HEADS * HEAD_DIM
NSA_IN = NSA_WIDTH + 6 * NSA_KV_COLS + 3 * NSA_HEADS
RWKV_IN = 3 * RWKV_WIDTH + RWKV_W_RANK + RWKV_A_RANK + RWKV_G_RANK
RET_IN = 4 * RET_WIDTH
EPS = 1e-6
GN_EPS = 1e-5
NEG = -1e30
TINY = 1e-30
FORCE = 1e9

VMEM_LIMIT_BYTES = 56 * 1024 * 1024


def _pick_tile(n, target):
    t = min(n, target)
    while n % t:
        t //= 2
    return t


def _ffn_body(expert_ref, used_ref, x_ref, g_ref, s_ref, wg_ref, wu_ref, wd_ref, y_ref, h_scr, acc_scr):
    i = pl.program_id(0)
    j = pl.program_id(1)
    last = pl.num_programs(1) - 1
    used = used_ref[i] > 0

    @pl.when(used & (j == 0))
    def _():
        x = x_ref[...]
        h = x * lax.rsqrt(jnp.mean(x * x, axis=-1, keepdims=True) + EPS) * g_ref[...]
        h_scr[...] = h.astype(jnp.bfloat16)
        acc_scr[...] = jnp.zeros_like(acc_scr)

    @pl.when(used)
    def _():
        h = h_scr[...]
        a = jnp.dot(h, wg_ref[...].astype(jnp.bfloat16), preferred_element_type=jnp.float32)
        b = jnp.dot(h, wu_ref[...].astype(jnp.bfloat16), preferred_element_type=jnp.float32)
        z = (a * jax.nn.sigmoid(a)) * b
        acc_scr[...] += jnp.dot(z.astype(jnp.bfloat16), wd_ref[...].astype(jnp.bfloat16),
                                preferred_element_type=jnp.float32)

    @pl.when(used & (j == last))
    def _():
        y_ref[...] = acc_scr[...] * s_ref[...]

    @pl.when(jnp.logical_not(used) & (j == last))
    def _():
        y_ref[...] = jnp.zeros_like(y_ref)


def _ffn(x, g, scale, wg, wu, wd, tile_expert, tile_used, *, tm, tf=512):
    M, D = x.shape
    F = wg.shape[2]
    tf = _pick_tile(F, tf)
    grid_spec = pltpu.PrefetchScalarGridSpec(
        num_scalar_prefetch=2,
        grid=(M // tm, F // tf),
        in_specs=[
            pl.BlockSpec((tm, D), lambda i, j, e, u: (i, 0)),
            pl.BlockSpec((1, D), lambda i, j, e, u: (0, 0)),
            pl.BlockSpec((tm, 1), lambda i, j, e, u: (i, 0)),
            pl.BlockSpec((None, D, tf), lambda i, j, e, u: (e[i], 0, j)),
            pl.BlockSpec((None, D, tf), lambda i, j, e, u: (e[i], 0, j)),
            pl.BlockSpec((None, tf, D), lambda i, j, e, u: (e[i], j, 0)),
        ],
        out_specs=pl.BlockSpec((tm, D), lambda i, j, e, u: (i, 0)),
        scratch_shapes=[pltpu.VMEM((tm, D), jnp.bfloat16), pltpu.VMEM((tm, D), jnp.float32)],
    )
    return pl.pallas_call(
        _ffn_body,
        grid_spec=grid_spec,
        out_shape=jax.ShapeDtypeStruct((M, D), jnp.float32),
        compiler_params=pltpu.CompilerParams(
            dimension_semantics=("parallel", "arbitrary"), vmem_limit_bytes=VMEM_LIMIT_BYTES),
        name="ffn",
    )(tile_expert, tile_used, x, g.reshape(1, D), scale, wg, wu, wd)


def _router_body(x_ref, g_ref, w_ref, o_ref):
    x = x_ref[...]
    h = x * lax.rsqrt(jnp.mean(x * x, axis=-1, keepdims=True) + EPS) * g_ref[...]
    o_ref[...] = jnp.dot(h, w_ref[...], preferred_element_type=jnp.float32, precision=lax.Precision.HIGHEST)


def _router_logits(x, g, router, *, tm=512):
    M, D = x.shape
    E = router.shape[1]
    tm = _pick_tile(M, tm)
    lanes = 128
    w = jnp.pad(router, ((0, 0), (0, lanes - E)))
    out = pl.pallas_call(
        _router_body,
        grid=(M // tm,),
        in_specs=[pl.BlockSpec((tm, D), lambda i: (i, 0)), pl.BlockSpec((1, D), lambda i: (0, 0)),
                  pl.BlockSpec((D, lanes), lambda i: (0, 0))],
        out_specs=pl.BlockSpec((tm, lanes), lambda i: (i, 0)),
        out_shape=jax.ShapeDtypeStruct((M, lanes), jnp.float32),
        compiler_params=pltpu.CompilerParams(dimension_semantics=("parallel",), vmem_limit_bytes=VMEM_LIMIT_BYTES),
        name="router",
    )(x, g.reshape(1, D), w)
    return out[:, :E]


def _rwkv_scan_body(r_ref, w_ref, k_ref, v_ref, kk_ref, b_ref, s0_ref, y_ref, st_ref, s_scr, *, sub):
    j = pl.program_id(1)
    n_b, t_blk, _ = r_ref.shape
    d = HEAD_DIM

    @pl.when(j == 0)
    def _():
        s_scr[...] = s0_ref[...]

    eye = lax.broadcasted_iota(jnp.int32, (d, d), 0) == lax.broadcasted_iota(jnp.int32, (d, d), 1)

    def sub_block(i, carry):
        t0 = pl.multiple_of(i * sub, sub)
        for bb in range(n_b):
            blk = [ref[bb, pl.ds(t0, sub), :] for ref in (r_ref, w_ref, k_ref, v_ref, kk_ref, b_ref)]
            for h in range(RWKV_HEADS):
                cols = slice(h * d, (h + 1) * d)
                S = s_scr[bb, h]
                y_rows = []
                for t in range(sub):
                    r_t, w_t, k_t, v_t, kk_t, b_t = [x[t:t + 1, cols] for x in blk]
                    sa = jnp.sum(S * kk_t, axis=1, keepdims=True)
                    v_col = jnp.sum(jnp.where(eye, v_t, 0.0), axis=1, keepdims=True)
                    S = S * w_t - sa * b_t + v_col * k_t
                    y_col = jnp.sum(S * r_t, axis=1, keepdims=True)
                    y_rows.append(jnp.sum(jnp.where(eye, y_col, 0.0), axis=0, keepdims=True))
                s_scr[bb, h] = S
                y_ref[bb, pl.ds(t0, sub), cols] = jnp.concatenate(y_rows, axis=0)
        return carry

    lax.fori_loop(0, t_blk // sub, sub_block, 0)

    @pl.when(j == pl.num_programs(1) - 1)
    def _():
        st_ref[...] = s_scr[...]


def _rwkv_scan(r, w, k, v, kk, b, s0, *, n_b=2, t_blk=256):
    B, T, W = r.shape
    n_b = _pick_tile(B, n_b)
    t_blk = _pick_tile(T, t_blk)
    sub = 8 if t_blk % 8 == 0 else t_blk
    seq = pl.BlockSpec((n_b, t_blk, W), lambda i, j: (i, j, 0))
    st = pl.BlockSpec((n_b, RWKV_HEADS, HEAD_DIM, HEAD_DIM), lambda i, j: (i, 0, 0, 0))
    return pl.pallas_call(
        functools.partial(_rwkv_scan_body, sub=sub),
        grid=(B // n_b, T // t_blk),
        in_specs=[seq] * 6 + [st],
        out_specs=[seq, st],
        out_shape=[jax.ShapeDtypeStruct((B, T, W), jnp.float32),
                   jax.ShapeDtypeStruct((B, RWKV_HEADS, HEAD_DIM, HEAD_DIM), jnp.float32)],
        scratch_shapes=[pltpu.VMEM((n_b, RWKV_HEADS, HEAD_DIM, HEAD_DIM), jnp.float32)],
        compiler_params=pltpu.CompilerParams(
            dimension_semantics=("parallel", "arbitrary"), vmem_limit_bytes=VMEM_LIMIT_BYTES),
        name="rwkv_scan",
    )(r, w, k, v, kk, b, s0)


RWKV_CHUNK = 64
RWKV_CHUNKS_PER_STEP = 4


_F32_DOT = dict(preferred_element_type=jnp.float32, precision=lax.Precision.HIGHEST)


def _dot(a, b):
    return jnp.dot(a, b, **_F32_DOT)


def _dot_t(a, b):
    return lax.dot_general(a, b, (((1,), (1,)), ((), ())), **_F32_DOT)


def _dot_0(a, b):
    return lax.dot_general(a, b, (((0,), (0,)), ((), ())), **_F32_DOT)


_NN = (((1,), (0,)), ((), ()))
_NT = (((1,), (1,)), ((), ()))


def _split(x):
    hi = x.astype(jnp.bfloat16)
    return hi, (x - hi.astype(jnp.float32)).astype(jnp.bfloat16)


def _mm3(a, b, dims):
    dot = lambda x, y: lax.dot_general(x, y, dims, preferred_element_type=jnp.float32)
    return dot(a[0], b[0]) + (dot(a[0], b[1]) + dot(a[1], b[0]))


def _rwkv_chunk_body(r_ref, lw_ref, k_ref, v_ref, kk_ref, b_ref, qe_ref, y0_ref, pm_ref, z_ref):
    L, d = RWKV_CHUNK, HEAD_DIM
    n = r_ref.shape[1]
    row = lax.broadcasted_iota(jnp.int32, (n, n), 0)
    col = lax.broadcasted_iota(jnp.int32, (n, n), 1)
    same = (row // L) == (col // L)
    lower = same & (row >= col)
    strict = same & (row > col)
    ones_lower = jnp.where(lower, 1.0, 0.0)
    eye = jnp.where(row == col, 1.0, 0.0)
    eye_d = eye[:d, :d]
    ones_bf = ones_lower.astype(jnp.bfloat16)
    for h in range(RWKV_HEADS):
        cols = slice(h * d, (h + 1) * d)
        r, lw, k, v, kk, b = [ref[0, :, cols] for ref in (r_ref, lw_ref, k_ref, v_ref, kk_ref, b_ref)]
        lw_hi, lw_lo = _split(lw)
        lw_rest = (lw - lw_hi.astype(jnp.float32) - lw_lo.astype(jnp.float32)).astype(jnp.bfloat16)
        G = sum(jnp.dot(ones_bf, t, preferred_element_type=jnp.float32) for t in (lw_hi, lw_lo, lw_rest))
        g_inv = jnp.exp(-G)
        kap, bt, kt, rt = kk * jnp.exp(G - lw), b * g_inv, k * g_inv, r * jnp.exp(G)
        kap2, bt2, kt2, rt2, v2 = _split(kap), _split(bt), _split(kt), _split(rt), _split(v)
        N = jnp.where(strict, _mm3(kap2, bt2, _NT), 0.0)
        Mk = jnp.where(strict, _mm3(kap2, kt2, _NT), 0.0)
        RB = jnp.where(lower, _mm3(rt2, bt2, _NT), 0.0)
        RK = jnp.where(lower, _mm3(rt2, kt2, _NT), 0.0)
        X, P2 = eye - N, _split(N)
        for _ in range(L.bit_length() - 2):
            P2 = _split(_mm3(P2, P2, _NN))
            X = X + _mm3(_split(X), P2, _NN)
        X2, RB2 = _split(X), _split(RB)
        A = _mm3(X2, kap2, _NN)
        C = _mm3(X2, _split(_mm3(_split(Mk), v2, _NN)), _NN)
        qe_ref[0, h] = rt - _mm3(RB2, _split(A), _NN)
        y0_ref[0, h] = _mm3(_split(RK), v2, _NN) - _mm3(RB2, _split(C), _NN)
        for c in range(n // L):
            rows = slice(c * L, (c + 1) * L)
            g_end = jnp.exp(G[(c + 1) * L - 1:(c + 1) * L, :])
            pm_ref[0, h, c] = (eye_d - _dot_0(A[rows], bt[rows])) * g_end
            z_ref[0, h, c] = (_dot_0(v[rows], kt[rows]) - _dot_0(C[rows], bt[rows])) * g_end


def _rwkv_walk_body(qe_ref, y0_ref, pm_ref, z_ref, s0_ref, y_ref, st_ref, s_scr):
    j = pl.program_id(0)
    B, H, n_c = pm_ref.shape[:3]
    L = RWKV_CHUNK

    @pl.when(j == 0)
    def _():
        s_scr[...] = s0_ref[...]

    def one_chunk(c, carry):
        t0 = pl.multiple_of(c * L, L)
        for bb in range(B):
            for h in range(H):
                S = s_scr[bb, h]
                y_ref[bb, h, pl.ds(t0, L), :] = _dot_t(qe_ref[bb, h, pl.ds(t0, L), :], S) + y0_ref[bb, h, pl.ds(t0, L), :]
                s_scr[bb, h] = _dot(S, pm_ref[bb, h, c]) + z_ref[bb, h, c]
        return carry

    lax.fori_loop(0, n_c, one_chunk, 0)

    @pl.when(j == pl.num_programs(0) - 1)
    def _():
        st_ref[...] = s_scr[...]


def _rwkv_chunked(r, lw, k, v, kk, b, s0, *, chunks_per_step=16):
    B, T, W = r.shape
    H, d, L = RWKV_HEADS, HEAD_DIM, RWKV_CHUNK
    n_c = T // L
    group = _pick_tile(n_c, RWKV_CHUNKS_PER_STEP)
    seq = pl.BlockSpec((1, group * L, W), lambda i, c: (i, c, 0))
    per_tok = pl.BlockSpec((1, H, group * L, d), lambda i, c: (i, 0, c, 0))
    per_chunk = pl.BlockSpec((1, H, group, d, d), lambda i, c: (i, 0, c, 0, 0))
    qe, y0, pm, z = pl.pallas_call(
        _rwkv_chunk_body,
        grid=(B, n_c // group),
        in_specs=[seq] * 6,
        out_specs=[per_tok, per_tok, per_chunk, per_chunk],
        out_shape=[jax.ShapeDtypeStruct((B, H, T, d), jnp.float32)] * 2
                  + [jax.ShapeDtypeStruct((B, H, n_c, d, d), jnp.float32)] * 2,
        compiler_params=pltpu.CompilerParams(
            dimension_semantics=("parallel", "parallel"), vmem_limit_bytes=VMEM_LIMIT_BYTES),
        name="rwkv_chunk",
    )(r, lw, k, v, kk, b)
    cs = _pick_tile(n_c, chunks_per_step)
    tok = pl.BlockSpec((B, H, cs * L, d), lambda j: (0, 0, j, 0))
    chk = pl.BlockSpec((B, H, cs, d, d), lambda j: (0, 0, j, 0, 0))
    st = pl.BlockSpec((B, H, d, d), lambda j: (0, 0, 0, 0))
    y, s_t = pl.pallas_call(
        _rwkv_walk_body,
        grid=(n_c // cs,),
        in_specs=[tok, tok, chk, chk, st],
        out_specs=[tok, st],
        out_shape=[jax.ShapeDtypeStruct((B, H, T, d), jnp.float32), jax.ShapeDtypeStruct((B, H, d, d), jnp.float32)],
        scratch_shapes=[pltpu.VMEM((B, H, d, d), jnp.float32)],
        compiler_params=pltpu.CompilerParams(
            dimension_semantics=("arbitrary",), vmem_limit_bytes=VMEM_LIMIT_BYTES),
        name="rwkv_walk",
    )(qe, y0, pm, z, s0)
    return jnp.transpose(y, (0, 2, 1, 3)).reshape(B, T, W), s_t


SEL_CHUNK = 512
WIN_SPAN = WINDOW + Q_BLOCK


def _nsa_scores(k_aug, qT_ref):
    return [jnp.dot(k_aug, qT_ref[0, 0, r], preferred_element_type=jnp.float32) for r in range(NSA_GROUP)]


def _nsa_softmax_cols(s_list, neg):
    out = []
    for s in s_list:
        s = s + neg
        m = jnp.max(s, axis=0, keepdims=True)
        e = jnp.exp(s - m)
        inv = jnp.where(m > 0.5 * NEG, 1.0, 0.0) / jnp.maximum(jnp.sum(e, axis=0, keepdims=True), TINY)
        out.append(e * inv)
    return out


def _nsa_prompt_body(qT_ref, kc_ref, vcT_ref, ks_ref, vsT_ref, kw_ref, vwT_ref, covT_ref, o_ref,
                     sel_scr, m_scr, l_scr, acc_scr):
    i = pl.program_id(2)
    R = NSA_GROUP
    n_cmp = kc_ref.shape[2]
    n_sel = covT_ref.shape[0]
    q0 = i * Q_BLOCK
    pos = q0 + lax.broadcasted_iota(jnp.int32, (1, Q_BLOCK), 1)
    posf = pos.astype(jnp.float32)
    bf = jnp.bfloat16

    c_end = (lax.broadcasted_iota(jnp.int32, (n_cmp, Q_BLOCK), 0) * CMP_STRIDE + (CMP_LEN - 1)).astype(jnp.float32)
    neg_c = jnp.where(posf - c_end >= 0.0, 0.0, NEG)
    p_c = _nsa_softmax_cols(_nsa_scores(kc_ref[0, 0], qT_ref), neg_c)
    vcT = vcT_ref[0, 0]
    o_c = [jnp.dot(vcT, p.astype(bf), preferred_element_type=jnp.float32) for p in p_c]
    for r in range(R):
        o_ref[0, 0, 0, r] = o_c[r]
    p_sum = (p_c[0] + p_c[1]) + (p_c[2] + p_c[3])
    imp = jnp.dot(covT_ref[...], p_sum, preferred_element_type=jnp.float32, precision=lax.Precision.HIGHEST)
    blk = lax.broadcasted_iota(jnp.int32, (n_sel, Q_BLOCK), 0)
    avail = blk * SEL_LEN <= pos
    forced = (blk == jnp.right_shift(pos, 6)) | (blk == 0)
    imp = jnp.where(avail, jnp.where(forced, FORCE, imp), -FORCE)

    sub8 = lax.broadcasted_iota(jnp.int32, (8, Q_BLOCK), 0)
    last_block = (q0 + Q_BLOCK - 1) // SEL_LEN

    def rank_group(jj):
        vj = imp[8 * jj:8 * jj + 8]
        cnt = jnp.zeros((8, Q_BLOCK), jnp.float32)
        for k in range(n_sel):
            row = imp[k:k + 1]
            if k < 8 * jj:
                cnt = cnt + jnp.where(row >= vj, 1.0, 0.0)
            elif k >= 8 * jj + 8:
                cnt = cnt + jnp.where(row > vj, 1.0, 0.0)
            else:
                cnt = cnt + jnp.where(sub8 > (k - 8 * jj), jnp.where(row >= vj, 1.0, 0.0),
                                      jnp.where(row > vj, 1.0, 0.0))
        sel_scr[8 * jj:8 * jj + 8, :] = jnp.where(cnt < float(SEL_TOP), 1.0, 0.0)

    def skip_group(jj):
        sel_scr[8 * jj:8 * jj + 8, :] = jnp.zeros((8, Q_BLOCK), jnp.float32)

    for jj in range(n_sel // 8):
        pl.when(8 * jj <= last_block)(functools.partial(rank_group, jj))
        pl.when(8 * jj > last_block)(functools.partial(skip_group, jj))

    m_scr[...] = jnp.full_like(m_scr, NEG)
    l_scr[...] = jnp.zeros_like(l_scr)
    acc_scr[...] = jnp.zeros_like(acc_scr)
    key_iota = lax.broadcasted_iota(jnp.int32, (SEL_CHUNK, Q_BLOCK), 0).astype(jnp.float32)
    blocks_per_chunk = SEL_CHUNK // SEL_LEN

    def chunk(c, carry):
        k0 = pl.multiple_of(c * SEL_CHUNK, SEL_CHUNK)
        s_all = _nsa_scores(ks_ref[0, 0, pl.ds(k0, SEL_CHUNK), :], qT_ref)
        dist = (posf - k0.astype(jnp.float32)) - key_iota
        picked = jnp.concatenate(
            [jnp.broadcast_to(sel_scr[pl.ds(c * blocks_per_chunk + b, 1), :], (SEL_LEN, Q_BLOCK))
             for b in range(blocks_per_chunk)], axis=0)
        neg = jnp.where((picked > 0.5) & (dist >= 0.0), 0.0, NEG)
        m_old = [m_scr[r] for r in range(R)]
        m_new, p_all, l_add = [], [], []
        for r in range(R):
            s = s_all[r] + neg
            m = jnp.maximum(m_old[r], jnp.max(s, axis=0, keepdims=True))
            p = jnp.exp(s - m)
            m_new.append(m)
            l_add.append(jnp.sum(p, axis=0, keepdims=True))
            p_all.append(p.astype(bf))
        vsT = vsT_ref[0, 0, :, pl.ds(k0, SEL_CHUNK)]
        pv = [jnp.dot(vsT, p, preferred_element_type=jnp.float32) for p in p_all]
        for r in range(R):
            alpha = jnp.exp(m_old[r] - m_new[r])
            l_scr[r] = alpha * l_scr[r] + l_add[r]
            acc_scr[r] = alpha * acc_scr[r] + pv[r]
            m_scr[r] = m_new[r]
        return carry

    lax.fori_loop(0, q0 // SEL_CHUNK + 1, chunk, 0)
    for r in range(R):
        o_ref[1, 0, 0, r] = acc_scr[r] / jnp.maximum(l_scr[r], TINY)

    k0w = pl.multiple_of(jnp.maximum(q0 - WINDOW, 0), Q_BLOCK)
    d_win = (posf - k0w.astype(jnp.float32)) - lax.broadcasted_iota(jnp.int32, (WIN_SPAN, Q_BLOCK), 0).astype(jnp.float32)
    neg_w = jnp.where((d_win >= 0.0) & (d_win < float(WINDOW)), 0.0, NEG)
    p_w = _nsa_softmax_cols(_nsa_scores(kw_ref[0, 0, pl.ds(k0w, WIN_SPAN), :], qT_ref), neg_w)
    vwT = vwT_ref[0, 0, :, pl.ds(k0w, WIN_SPAN)]
    o_w = [jnp.dot(vwT, p.astype(bf), preferred_element_type=jnp.float32) for p in p_w]
    for r in range(R):
        o_ref[2, 0, 0, r] = o_w[r]


def _nsa_prompt_attn(q, kc, vc, ks, vs, kw, vw):
    B, T, H, d = q.shape
    G, R = NSA_KV_HEADS, NSA_GROUP
    n_c = kc.shape[1]
    n_cmp = -(-n_c // 128) * 128
    n_sel = T // SEL_LEN
    bf = jnp.bfloat16
    lanes = 128

    def key_rows(k, key_pos):
        n = k.shape[1]
        extra = jnp.zeros((n, lanes - d), jnp.float32).at[:, 0].set((key_pos // SEL_LEN).astype(jnp.float32))
        extra = extra.at[:, 1].set((key_pos % SEL_LEN).astype(jnp.float32))
        extra = jnp.broadcast_to(extra[None, None], (B, G, n, lanes - d))
        return jnp.concatenate([jnp.transpose(k, (0, 2, 1, 3)), extra], axis=-1).astype(bf)

    cols = lambda t: jnp.transpose(t, (0, 2, 3, 1)).astype(bf)
    slopes = (2.0 ** -jnp.arange(1, H + 1, dtype=jnp.float32)).reshape(G, R)
    q_extra = jnp.zeros((G, R, lanes - d), jnp.float32).at[:, :, 0].set(SEL_LEN * slopes).at[:, :, 1].set(slopes)
    qT = jnp.transpose((q * (d ** -0.5)).reshape(B, T, G, R, d), (0, 2, 3, 4, 1))
    qT = jnp.concatenate([qT, jnp.broadcast_to(q_extra[None, :, :, :, None], (B, G, R, lanes - d, T))],
                         axis=3).astype(bf)
    pad_c = ((0, 0), (0, n_cmp - n_c), (0, 0), (0, 0))
    kc_r = key_rows(jnp.pad(kc, pad_c), jnp.arange(n_cmp, dtype=jnp.int32) * CMP_STRIDE + (CMP_LEN - 1))
    vc_c = cols(jnp.pad(vc, pad_c))
    tok = jnp.arange(T, dtype=jnp.int32)
    rows = lambda t: key_rows(t, tok)
    c_start = jnp.arange(n_cmp, dtype=jnp.int32) * CMP_STRIDE
    s_start = jnp.arange(n_sel, dtype=jnp.int32) * SEL_LEN
    covT = jnp.maximum(jnp.minimum(c_start[None, :] + CMP_LEN, s_start[:, None] + SEL_LEN)
                       - jnp.maximum(c_start[None, :], s_start[:, None]), 0).astype(jnp.float32) / CMP_LEN
    full_r = lambda n: pl.BlockSpec((1, 1, n, lanes), lambda b, g, i: (b, g, 0, 0))
    full_c = lambda n: pl.BlockSpec((1, 1, d, n), lambda b, g, i: (b, g, 0, 0))
    oT = pl.pallas_call(
        _nsa_prompt_body,
        grid=(B, G, T // Q_BLOCK),
        in_specs=[pl.BlockSpec((1, 1, R, lanes, Q_BLOCK), lambda b, g, i: (b, g, 0, 0, i)),
                  full_r(n_cmp), full_c(n_cmp), full_r(T), full_c(T), full_r(T), full_c(T),
                  pl.BlockSpec((n_sel, n_cmp), lambda b, g, i: (0, 0))],
        out_specs=pl.BlockSpec((3, 1, 1, R, d, Q_BLOCK), lambda b, g, i: (0, b, g, 0, 0, i)),
        out_shape=jax.ShapeDtypeStruct((3, B, G, R, d, T), jnp.float32),
        scratch_shapes=[pltpu.VMEM((n_sel, Q_BLOCK), jnp.float32),
                        pltpu.VMEM((R, 1, Q_BLOCK), jnp.float32),
                        pltpu.VMEM((R, 1, Q_BLOCK), jnp.float32),
                        pltpu.VMEM((R, d, Q_BLOCK), jnp.float32)],
        compiler_params=pltpu.CompilerParams(
            dimension_semantics=("parallel", "parallel", "arbitrary"), vmem_limit_bytes=VMEM_LIMIT_BYTES),
        name="nsa_prompt",
    )(qT, kc_r, vc_c, rows(ks), cols(vs), rows(kw), cols(vw), covT)
    return jnp.transpose(oT, (0, 1, 5, 2, 3, 4)).reshape(3, B, T, H, d)


PAGES_PER_STEP = 8
def _compress_pages_body(pt_ref, *rest):
    page_refs = rest[:PAGES_PER_STEP]
    w1_ref, c0_ref, w2_ref, ones_ref, gain_ref, o_ref, x_scr = rest[PAGES_PER_STEP:]
    j = pl.program_id(1)
    rows = page_refs[0].shape[0]
    for i, ref in enumerate(page_refs):
        x_scr[pl.ds(pl.multiple_of((j * PAGES_PER_STEP + i) * rows, rows), rows), :] = ref[...]

    @pl.when(j == pl.num_programs(1) - 1)
    def _():
        n, half = x_scr.shape[0], w2_ref.shape[0]
        ab = jnp.dot(x_scr[...].astype(jnp.bfloat16), w1_ref[...], preferred_element_type=jnp.float32)
        top, bottom = ab[:, :half], ab[:, half:]
        nxt = jnp.concatenate([bottom[1:], jnp.zeros((1, half), jnp.float32)], axis=0)
        hid = jax.nn.gelu(top + nxt + c0_ref[...])
        y = jnp.dot(hid.astype(jnp.bfloat16), w2_ref[...], preferred_element_type=jnp.float32)
        mean_sq = jnp.dot(y * y, ones_ref[...], preferred_element_type=jnp.float32, precision=lax.Precision.HIGHEST)
        is_key = lax.broadcasted_iota(jnp.int32, (n, half), 1) < half // 2
        o_ref[0] = jnp.where(is_key, y * lax.rsqrt(mean_sq + EPS) * gain_ref[...], y)


def _compress_pages(pool, pages, n_batch, pe, w1, w2, k_gain):
    d, G = HEAD_DIM, NSA_KV_HEADS
    N, page_rows, W = pool.shape
    cpp = page_rows // CMP_STRIDE
    n_pages = pages.shape[0] // n_batch
    halves = CMP_LEN // CMP_STRIDE
    w1r = w1.reshape(2, halves, CMP_STRIDE, d, d)[jnp.array([0] * G + [1] * G)]
    eye = jnp.eye(2 * G, dtype=jnp.float32)
    w_big = jnp.transpose(w1r, (2, 0, 3, 1, 4))[:, :, :, :, None, :] * eye[None, :, None, None, :, None]
    w_big = w_big.reshape(CMP_STRIDE * W, halves * W).astype(jnp.bfloat16)
    c0 = jnp.einsum('kn,kne->ke', pe.reshape(2, CMP_LEN * d), w1, precision=lax.Precision.HIGHEST)
    c0 = jnp.repeat(c0, G, axis=0).reshape(1, W)
    w2_big = (w2[jnp.array([0] * G + [1] * G)][:, :, None, :] * eye[:, None, :, None]).reshape(W, W).astype(jnp.bfloat16)
    ones_blk = jnp.kron(eye, jnp.full((d, d), 1.0 / d, jnp.float32))
    gain = jnp.concatenate([jnp.tile(k_gain, G), jnp.ones((G * d,), jnp.float32)]).reshape(1, W)
    pool_chunks = pool.reshape(N, cpp, CMP_STRIDE * W)
    page_spec = lambda k: pl.BlockSpec((None, cpp, CMP_STRIDE * W),
                                       lambda b, j, pt: (pt[b * n_pages + j * PAGES_PER_STEP + k], 0, 0))
    const = lambda shape: pl.BlockSpec(shape, lambda b, j, pt: (0, 0))
    grid_spec = pltpu.PrefetchScalarGridSpec(
        num_scalar_prefetch=1,
        grid=(n_batch, n_pages // PAGES_PER_STEP),
        in_specs=[page_spec(k) for k in range(PAGES_PER_STEP)]
                 + [const(w_big.shape), const((1, W)), const((W, W)), const((W, W)), const((1, W))],
        out_specs=pl.BlockSpec((1, n_pages * cpp, W), lambda b, j, pt: (b, 0, 0)),
        scratch_shapes=[pltpu.VMEM((n_pages * cpp, CMP_STRIDE * W), jnp.float32)],
    )
    return pl.pallas_call(
        _compress_pages_body,
        grid_spec=grid_spec,
        out_shape=jax.ShapeDtypeStruct((n_batch, n_pages * cpp, W), jnp.float32),
        compiler_params=pltpu.CompilerParams(
            dimension_semantics=("parallel", "arbitrary"), vmem_limit_bytes=VMEM_LIMIT_BYTES),
        name="compress_pages",
    )(pages, *([pool_chunks] * PAGES_PER_STEP), w_big, c0, w2_big, ones_blk, gain)


def _rms_norm(x, g):
    xf = x.astype(jnp.float32)
    y = xf * lax.rsqrt(jnp.mean(xf * xf, axis=-1, keepdims=True) + EPS)
    return (y * g.astype(jnp.float32)).astype(x.dtype)


def _group_norm(x, g, eps):
    xf = x.astype(jnp.float32)
    mu = jnp.mean(xf, axis=-1, keepdims=True)
    var = jnp.mean(jnp.square(xf - mu), axis=-1, keepdims=True)
    return (xf - mu) * lax.rsqrt(var + eps) * g.astype(jnp.float32)


def _masked_softmax(s, mask):
    s = jnp.where(mask, s, NEG)
    m = jnp.max(s, axis=-1, keepdims=True)
    e = jnp.where(mask, jnp.exp(s - m), 0.0)
    return e / jnp.maximum(jnp.sum(e, axis=-1, keepdims=True), TINY)


def _alibi_slopes(n):
    return 2.0 ** (-8.0 * jnp.arange(1, n + 1, dtype=jnp.float32) / n)


def _rotary(x, pos):
    half = x.shape[-1] // 2
    freqs = ROPE_BASE ** (-jnp.arange(half, dtype=jnp.float32) / half)
    ang = pos.astype(jnp.float32)[:, None] * freqs[None, :]
    cos, sin = jnp.cos(ang)[None, :, None, :], jnp.sin(ang)[None, :, None, :]
    x1, x2 = x[..., :half], x[..., half:]
    return jnp.concatenate([x1 * cos - x2 * sin, x1 * sin + x2 * cos], axis=-1)


def _nsa_compress(rows, pe, w1, w2):
    B, T, G, d = rows.shape
    n_cmp = (T - CMP_LEN) // CMP_STRIDE + 1
    idx = (jnp.arange(n_cmp, dtype=jnp.int32) * CMP_STRIDE)[:, None] + jnp.arange(CMP_LEN, dtype=jnp.int32)[None, :]
    blk = rows[:, idx] + pe[None, None, :, None, :]
    blk = jnp.transpose(blk, (0, 1, 3, 2, 4)).reshape(B, n_cmp, G, CMP_LEN * d)
    return jax.nn.gelu(blk @ w1) @ w2


def _to_sel_blocks(rows):
    B, T, G, d = rows.shape
    n_sel = -(-T // SEL_LEN)
    rows = jnp.pad(rows, ((0, 0), (0, n_sel * SEL_LEN - T), (0, 0), (0, 0)))
    return jnp.transpose(rows.reshape(B, n_sel, SEL_LEN, G, d), (0, 3, 1, 2, 4))


def _nsa_branches(q, pos, kc, vc, n_sel, sel_branch, kw, vw, pos_w, slopes):
    B, Tq, H, d = q.shape
    G, R = NSA_KV_HEADS, NSA_GROUP
    scale = d ** -0.5
    qg = q.reshape(B, Tq, G, R, d)
    sl = slopes.reshape(G, R)
    posf = pos.astype(jnp.float32)
    n_cmp = kc.shape[1]
    c_start = jnp.arange(n_cmp, dtype=jnp.int32) * CMP_STRIDE
    d_cmp = posf[:, None] - (c_start + CMP_LEN - 1).astype(jnp.float32)[None, :]
    s = jnp.einsum('btgrd,bngd->bgrtn', qg, kc).astype(jnp.float32) * scale - sl[None, :, :, None, None] * d_cmp
    p_cmp = _masked_softmax(s, (d_cmp >= 0.0)[None, None, None])
    o_cmp = jnp.einsum('bgrtn,bngd->btgrd', p_cmp.astype(vc.dtype), vc)
    s_start = jnp.arange(n_sel, dtype=jnp.int32) * SEL_LEN
    cover = jnp.maximum(jnp.minimum(c_start[:, None] + CMP_LEN, s_start[None, :] + SEL_LEN)
                        - jnp.maximum(c_start[:, None], s_start[None, :]), 0).astype(jnp.float32) / CMP_LEN
    imp = jnp.einsum('bgrtn,nj->bgtj', p_cmp, cover)
    blk = jnp.arange(n_sel, dtype=jnp.int32)
    avail = s_start[None, :] <= pos[:, None]
    forced = (blk[None, :] == (pos // SEL_LEN)[:, None]) | (blk[None, :] == 0)
    imp = jnp.where(avail, jnp.where(forced, FORCE, imp), -FORCE)
    n_top = min(SEL_TOP, n_sel)
    _, idx = lax.top_k(imp, n_top)
    o_sel = sel_branch(qg, idx)
    d_win = posf[:, None] - pos_w.astype(jnp.float32)[None, :]
    s = jnp.einsum('btgrd,bwgd->bgrtw', qg, kw).astype(jnp.float32) * scale - sl[None, :, :, None, None] * d_win
    win_mask = (d_win >= 0.0) & (d_win < WINDOW) & (pos_w >= 0)[None, :]
    p_win = _masked_softmax(s, win_mask[None, None, None])
    o_win = jnp.einsum('bgrtw,bwgd->btgrd', p_win.astype(vw.dtype), vw)
    return jnp.stack([o_cmp, o_sel, o_win]).reshape(3, B, Tq, H, d)


def _rwkv_group(u, shift0, S0, lp):
    B, T, _ = u.shape
    W = RWKV_WIDTH
    uf = u.astype(jnp.float32)
    prev = jnp.concatenate([shift0.astype(jnp.float32)[:, None], uf[:, :-1]], axis=1)
    um = uf + (prev - uf) * lp['rwkv_mu']
    r, k, v = um[..., :W], um[..., W:2 * W], um[..., 2 * W:3 * W]
    o = 3 * W
    wd = um[..., o:o + RWKV_W_RANK]
    ad = um[..., o + RWKV_W_RANK:o + RWKV_W_RANK + RWKV_A_RANK]
    gd = um[..., o + RWKV_W_RANK + RWKV_A_RANK:]
    w = lp['rwkv_w0'] + jnp.tanh(wd) @ lp['rwkv_w_up']
    log_decay = -jnp.exp(-jax.nn.softplus(-w) - 0.5)
    a = jax.nn.sigmoid(lp['rwkv_a0'] + ad @ lp['rwkv_a_up'])
    g = jax.nn.sigmoid(gd) @ lp['rwkv_g_up']
    kk = k * lp['rwkv_k_k']
    k = k * (1.0 + (a - 1.0) * lp['rwkv_k_a'])
    hd = lambda t: t.reshape(B, T, RWKV_HEADS, HEAD_DIM).astype(jnp.float32)
    r, k, v, a, g, kk = hd(r), hd(k), hd(v), hd(a), hd(g), hd(kk)
    kk = kk / jnp.maximum(jnp.sqrt(jnp.sum(kk * kk, axis=-1, keepdims=True)), 1e-12)
    flat = lambda t: t.reshape(B, T, W)
    if T % RWKV_CHUNK == 0:
        ys, S_T = _rwkv_chunked(flat(r), log_decay, flat(k), flat(v), flat(kk), flat(kk * a), S0.astype(jnp.float32))
    else:
        ys, S_T = _rwkv_scan(flat(r), jnp.exp(log_decay), flat(k), flat(v), flat(kk), flat(kk * a),
                             S0.astype(jnp.float32))
    y = _group_norm(ys.reshape(B, T, RWKV_HEADS, HEAD_DIM), lp['rwkv_ln_g'], RWKV_GN_EPS) + lp['rwkv_ln_b']
    y = y + jnp.sum(r * k * lp['rwkv_r_k'], axis=-1, keepdims=True) * v
    y = y * g
    return y.reshape(B, T, W).astype(u.dtype), u[:, -1], S_T


def _retention_group(u, pos, S0, ln_g):
    B, T, _ = u.shape
    uf = u.astype(jnp.float32)
    q, k, v, g = [t.reshape(B, T, RET_HEADS, HEAD_DIM) for t in jnp.split(uf, 4, axis=-1)]
    q = _rotary(q, pos)
    k = _rotary(k, pos) * HEAD_DIM ** -0.5
    lg = jnp.log(1.0 - 2.0 ** (-5.0 - jnp.arange(RET_HEADS, dtype=jnp.float32)))
    C = RET_CHUNK if T % RET_CHUNK == 0 else T
    nC = T // C
    n = jnp.arange(C, dtype=jnp.float32)
    diff = n[:, None] - n[None, :]
    dmask = jnp.where(diff[None] >= 0, jnp.exp(jnp.maximum(diff, 0.0)[None] * lg[:, None, None]), 0.0)
    q_dec = jnp.exp((n[:, None] + 1.0) * lg[None, :])
    k_dec = jnp.exp((C - 1.0 - n)[:, None] * lg[None, :])
    s_dec = jnp.exp(C * lg)

    def chunk(S, inp):
        qc, kc, vc = inp
        att = jnp.einsum('bnhd,bmhd->bhnm', qc, kc) * dmask
        out = jnp.einsum('bhnm,bmhe->bnhe', att, vc) + jnp.einsum('bnhd,bhde->bnhe', qc, S) * q_dec[None, :, :, None]
        S = S * s_dec[None, :, None, None] + jnp.einsum('bmhd,bmhe->bhde', kc * k_dec[None, :, :, None], vc)
        return S, out

    to_chunks = lambda t: jnp.moveaxis(t.reshape(B, nC, C, RET_HEADS, HEAD_DIM), 1, 0)
    S_T, o = lax.scan(chunk, S0.astype(jnp.float32), (to_chunks(q), to_chunks(k), to_chunks(v)))
    o = jnp.moveaxis(o, 0, 1).reshape(B, T, RET_HEADS, HEAD_DIM)
    y = jax.nn.silu(g) * _group_norm(o, ln_g, GN_EPS)
    return y.reshape(B, T, RET_WIDTH).astype(u.dtype), S_T


def _mixing_sublayer(x, lp, past_len, past_cmp, past_sel, win_buf, rwkv_S, rwkv_shift, ret_S):
    B, T, _ = x.shape
    h = _rms_norm(x, lp['norm_attn'])
    P = h @ lp['w_in']
    c = P[..., :NSA_IN]
    q = _rms_norm(c[..., :NSA_WIDTH].reshape(B, T, NSA_HEADS, HEAD_DIM), lp['nsa_q_norm'])
    kv = c[..., NSA_WIDTH:NSA_WIDTH + 6 * NSA_KV_COLS].reshape(B, T, 3, 2, NSA_KV_HEADS, HEAD_DIM)
    kv_cmp, kv_sel, kv_win = kv[:, :, 0], kv[:, :, 1], kv[:, :, 2]
    gates = jax.nn.sigmoid(c[..., NSA_WIDTH + 6 * NSA_KV_COLS:].astype(jnp.float32)).reshape(B, T, NSA_HEADS, 3)
    pos = past_len + jnp.arange(T, dtype=jnp.int32)
    slopes = _alibi_slopes(NSA_HEADS)
    prompt = past_cmp is None
    if prompt:
        rows_cmp, rows_sel, rows_win = kv_cmp, kv_sel, kv_win
        new_win = kv_win[:, T - min(WINDOW, T):]
    else:
        rows_win = jnp.concatenate([win_buf, kv_win.astype(win_buf.dtype)], axis=1)
        new_win = rows_win[:, T:]
    k_norm = lp['nsa_k_norm']
    if prompt:
        kc = _rms_norm(_nsa_compress(rows_cmp[:, :, 0], lp['nsa_cmp_pe'][0], lp['nsa_cmp_w1'][0], lp['nsa_cmp_w2'][0]), k_norm[0])
        vc = _nsa_compress(rows_cmp[:, :, 1], lp['nsa_cmp_pe'][1], lp['nsa_cmp_w1'][1], lp['nsa_cmp_w2'][1])
    else:
        n_cmp = (past_len + T - CMP_LEN) // CMP_STRIDE + 1
        assert (n_cmp - 1) * CMP_STRIDE + CMP_LEN <= past_len
        kcv = _compress_pages(past_cmp[0], past_cmp[1], B, lp['nsa_cmp_pe'], lp['nsa_cmp_w1'], lp['nsa_cmp_w2'], k_norm[0])
        kcv = kcv[:, :n_cmp].reshape(B, n_cmp, 2, NSA_KV_HEADS, HEAD_DIM)
        kc, vc = kcv[:, :, 0], kcv[:, :, 1]
    kw = _rms_norm(rows_win[:, :, 0], k_norm[2])
    vw = rows_win[:, :, 1]
    if prompt:
        o3 = _nsa_prompt_attn(q, kc, vc, _rms_norm(rows_sel[:, :, 0], k_norm[1]), rows_sel[:, :, 1], kw, vw)
    else:
        G, R, d = NSA_KV_HEADS, NSA_GROUP, HEAD_DIM
        rows_sel = jnp.concatenate([past_sel, kv_sel.astype(past_sel.dtype)], axis=1)
        ks_blk = _to_sel_blocks(_rms_norm(rows_sel[:, :, 0], k_norm[1]))
        vs_blk = _to_sel_blocks(rows_sel[:, :, 1])
        n_sel = ks_blk.shape[2]
        sl = slopes.reshape(G, R)

        def sel_branch(qg, idx):
            b_i = jnp.arange(B)[:, None, None, None]
            g_i = jnp.arange(G)[None, :, None, None]
            ks_g = ks_blk[b_i, g_i, idx]
            vs_g = vs_blk[b_i, g_i, idx]
            kpos = idx[..., None] * SEL_LEN + jnp.arange(SEL_LEN, dtype=jnp.int32)
            d_sel = (pos[None, None, :, None, None] - kpos).astype(jnp.float32)[:, :, None]
            s = (jnp.einsum('btgrd,bgtnsd->bgrtns', qg, ks_g).astype(jnp.float32) * (d ** -0.5)
                 - sl[None, :, :, None, None, None] * d_sel)
            mask = jnp.broadcast_to(d_sel >= 0.0, s.shape)
            p_sel = _masked_softmax(s.reshape(B, G, R, T, -1), mask.reshape(B, G, R, T, -1)).reshape(s.shape)
            return jnp.einsum('bgrtns,bgtnsd->btgrd', p_sel.astype(vs_g.dtype), vs_g)

        wb = win_buf.shape[1]
        pos_w = past_len - wb + jnp.arange(wb + T, dtype=jnp.int32)
        o3 = _nsa_branches(q, pos, kc, vc, n_sel, sel_branch, kw, vw, pos_w, slopes)
    o_nsa = jnp.einsum('btha,abthd->bthd', gates.astype(o3.dtype), o3)
    o_nsa = _rms_norm(o_nsa, lp['nsa_out_norm']).reshape(B, T, NSA_WIDTH)
    y_rwkv, new_shift, new_rwkv = _rwkv_group(P[..., NSA_IN:NSA_IN + RWKV_IN], rwkv_shift, rwkv_S, lp)
    y_ret, new_ret = _retention_group(P[..., NSA_IN + RWKV_IN:], pos, ret_S, lp['ret_ln_g'])
    mix = jnp.concatenate([o_nsa, y_rwkv.astype(o_nsa.dtype), y_ret.astype(o_nsa.dtype)], axis=-1)
    x = x + mix @ lp['w_out']
    return x, (kv_cmp, kv_sel, new_win, new_rwkv, new_shift, new_ret)


FFN_ROW_TILE = 512


def _dense_ffn(x, g, wg, wu, wd):
    shp = x.shape
    x2 = x.reshape(-1, shp[-1])
    tm = _pick_tile(x2.shape[0], FFN_ROW_TILE)
    n_tiles = x2.shape[0] // tm
    ones = jnp.ones((x2.shape[0], 1), jnp.float32)
    y = _ffn(x2, g, ones, wg[None], wu[None], wd[None], jnp.zeros((n_tiles,), jnp.int32),
             jnp.ones((n_tiles,), jnp.int32), tm=tm)
    return (x2 + y).reshape(shp)


def _moe_ffn(xs, g, router, wg, wu, wd):
    D = xs[0].shape[-1]
    E, tm = router.shape[1], FFN_ROW_TILE
    flat = [x.reshape(-1, D) for x in xs]
    x2 = jnp.concatenate(flat, axis=0)
    logits = jnp.concatenate([_router_logits(x, g, router) for x in flat], axis=0)
    top_val, top_idx = lax.top_k(logits, TOP_K)
    gate = jax.nn.softmax(top_val, axis=-1)
    N = x2.shape[0]
    A = N * TOP_K
    flat_e, flat_w = top_idx.reshape(A).astype(jnp.int32), gate.reshape(A)
    order = jnp.argsort(flat_e, stable=True).astype(jnp.int32)
    counts = jnp.sum(jax.nn.one_hot(flat_e, E, dtype=jnp.int32), axis=0)
    padded = (counts + tm - 1) // tm * tm
    start, p_end = jnp.cumsum(counts) - counts, jnp.cumsum(padded)
    p_start = p_end - padded
    sorted_e = flat_e[order]
    dest = p_start[sorted_e] + (jnp.arange(A, dtype=jnp.int32) - start[sorted_e])
    P = -(-(A + E * (tm - 1)) // tm) * tm
    row_tok = jnp.zeros((P,), jnp.int32).at[dest].set(order // TOP_K)
    row_w = jnp.zeros((P,), jnp.float32).at[dest].set(flat_w[order])
    tile_start = jnp.arange(P // tm, dtype=jnp.int32) * tm
    tile_e = jnp.minimum(jnp.searchsorted(p_end, tile_start, side='right'), E - 1).astype(jnp.int32)
    tile_used = (tile_start < p_end[-1]).astype(jnp.int32)
    ys = _ffn(x2[row_tok], g, row_w[:, None], wg, wu, wd, tile_e, tile_used, tm=tm)
    slot = jnp.zeros((A,), jnp.int32).at[order].set(dest).reshape(N, TOP_K)
    y = x2 + ys[slot[:, 0]] + ys[slot[:, 1]]
    outs, off = [], 0
    for x in xs:
        n = x.size // D
        outs.append(y[off:off + n].reshape(x.shape))
        off += n
    return outs


def kernel(x_prompt, x_sample, cache_nsa_cmp, cache_nsa_sel, cache_nsa_win, state_rwkv, state_rwkv_shift,
           state_ret, page_table, norm_attn, norm_ffn, w_in, w_out, nsa_q_norm, nsa_k_norm, nsa_cmp_pe,
           nsa_cmp_w1, nsa_cmp_w2, nsa_out_norm, rwkv_mu, rwkv_w0, rwkv_w_up, rwkv_a0, rwkv_a_up, rwkv_g_up,
           rwkv_k_k, rwkv_k_a, rwkv_r_k, rwkv_ln_g, rwkv_ln_b, ret_ln_g, ffn_w_gate, ffn_w_up, ffn_w_down,
           moe_router, moe_w_gate, moe_w_up, moe_w_down):
    Bp, Bs = x_prompt.shape[0], x_sample.shape[0]
    past_len = page_table.shape[1] * cache_nsa_cmp.shape[2]
    xp, xs = x_prompt, x_sample
    outs_p, outs_s = [], []
    for l in range(DEPTH):
        lp = {'norm_attn': norm_attn[l], 'w_in': w_in[l], 'w_out': w_out[l], 'nsa_q_norm': nsa_q_norm[l],
              'nsa_k_norm': nsa_k_norm[l], 'nsa_cmp_pe': nsa_cmp_pe[l], 'nsa_cmp_w1': nsa_cmp_w1[l],
              'nsa_cmp_w2': nsa_cmp_w2[l], 'nsa_out_norm': nsa_out_norm[l], 'rwkv_mu': rwkv_mu[l],
              'rwkv_w0': rwkv_w0[l], 'rwkv_w_up': rwkv_w_up[l], 'rwkv_a0': rwkv_a0[l], 'rwkv_a_up': rwkv_a_up[l],
              'rwkv_g_up': rwkv_g_up[l], 'rwkv_k_k': rwkv_k_k[l], 'rwkv_k_a': rwkv_k_a[l], 'rwkv_r_k': rwkv_r_k[l],
              'rwkv_ln_g': rwkv_ln_g[l], 'rwkv_ln_b': rwkv_ln_b[l], 'ret_ln_g': ret_ln_g[l]}
        xp, st = _mixing_sublayer(xp, lp, 0, None, None, None,
                                  jnp.zeros((Bp, RWKV_HEADS, HEAD_DIM, HEAD_DIM), jnp.float32),
                                  jnp.zeros((Bp, RWKV_IN), xp.dtype),
                                  jnp.zeros((Bp, RET_HEADS, HEAD_DIM, HEAD_DIM), jnp.float32))
        outs_p.append(st)
        n_pool, page_rows = cache_nsa_sel.shape[1], cache_nsa_sel.shape[2]
        pages = (page_table + l * n_pool).reshape(-1)
        past_cmp = (cache_nsa_cmp.reshape(DEPTH * n_pool, page_rows, -1), pages)
        past_sel = cache_nsa_sel[l][page_table].reshape(Bs, past_len, 2, NSA_KV_HEADS, HEAD_DIM)
        xs, st = _mixing_sublayer(xs, lp, past_len, past_cmp, past_sel, cache_nsa_win[l], state_rwkv[l],
                                  state_rwkv_shift[l], state_ret[l])
        outs_s.append(st)
        i = l // 2
        if l % 2 == 0:
            xp = _dense_ffn(xp, norm_ffn[l], ffn_w_gate[i], ffn_w_up[i], ffn_w_down[i])
            xs = _dense_ffn(xs, norm_ffn[l], ffn_w_gate[i], ffn_w_up[i], ffn_w_down[i])
        else:
            xp, xs = _moe_ffn([xp, xs], norm_ffn[l], moe_router[i], moe_w_gate[i], moe_w_up[i], moe_w_down[i])
    kv_cmp_p, kv_sel_p, win_p, rwkv_p, shift_p, ret_p = [jnp.stack([o[j] for o in outs_p]) for j in range(6)]
    kv_cmp_s, kv_sel_s, win_s, rwkv_s, shift_s, ret_s = [jnp.stack([o[j] for o in outs_s]) for j in range(6)]
    return (xp, xs, kv_cmp_p, kv_sel_p, win_p, rwkv_p, shift_p, ret_p,
            kv_cmp_s, kv_sel_s, win_s, rwkv_s, shift_s, ret_s)
```

```python
import functools

import jax
import jax.numpy as jnp
from jax import lax
from jax.experimental import pallas as pl
from jax.experimental.pallas import tpu as pltpu

D_MODEL = 1024
DEPTH = 2
HEAD_DIM = 64
NSA_WIDTH = D_MODEL // 2
RWKV_WIDTH = D_MODEL // 4
RET_WIDTH = D_MODEL - NSA_WIDTH - RWKV_WIDTH
NSA_HEADS = NSA_WIDTH // HEAD_DIM
NSA_KV_HEADS = 2
NSA_GROUP = NSA_HEADS // NSA_KV_HEADS
CMP_LEN = 32
CMP_STRIDE = 16
SEL_LEN = 64
SEL_TOP = 16
WINDOW = 512
Q_BLOCK = 128
RWKV_HEADS = RWKV_WIDTH // HEAD_DIM
RWKV_W_RANK = 64
RWKV_A_RANK = 64
RWKV_G_RANK = 128
RWKV_GN_EPS = 64e-5
RET_HEADS = RET_WIDTH // HEAD_DIM
RET_CHUNK = 128
ROPE_BASE = 10000.0
N_EXPERTS = 8
TOP_K = 2
NSA_KV_COLS = NSA_KV_HEADS * HEAD_DIM
NSA_IN = NSA_WIDTH + 6 * NSA_KV_COLS + 3 * NSA_HEADS
RWKV_IN = 3 * RWKV_WIDTH + RWKV_W_RANK + RWKV_A_RANK + RWKV_G_RANK
RET_IN = 4 * RET_WIDTH
EPS = 1e-6
GN_EPS = 1e-5
NEG = -1e30
TINY = 1e-30
FORCE = 1e9

VMEM_LIMIT_BYTES = 56 * 1024 * 1024


def _pick_tile(n, target):
    t = min(n, target)
    while n % t:
        t //= 2
    return t


def _ffn_body(expert_ref, used_ref, x_ref, g_ref, s_ref, wg_ref, wu_ref, wd_ref, y_ref, h_scr, acc_scr):
    i = pl.program_id(0)
    j = pl.program_id(1)
    last = pl.num_programs(1) - 1
    used = used_ref[i] > 0

    @pl.when(used & (j == 0))
    def _():
        x = x_ref[...]
        h = x * lax.rsqrt(jnp.mean(x * x, axis=-1, keepdims=True) + EPS) * g_ref[...]
        h_scr[...] = h.astype(jnp.bfloat16)
        acc_scr[...] = jnp.zeros_like(acc_scr)

    @pl.when(used)
    def _():
        h = h_scr[...]
        a = jnp.dot(h, wg_ref[...].astype(jnp.bfloat16), preferred_element_type=jnp.float32)
        b = jnp.dot(h, wu_ref[...].astype(jnp.bfloat16), preferred_element_type=jnp.float32)
        z = (a * jax.nn.sigmoid(a)) * b
        acc_scr[...] += jnp.dot(z.astype(jnp.bfloat16), wd_ref[...].astype(jnp.bfloat16),
                                preferred_element_type=jnp.float32)

    @pl.when(used & (j == last))
    def _():
        y_ref[...] = acc_scr[...] * s_ref[...]

    @pl.when(jnp.logical_not(used) & (j == last))
    def _():
        y_ref[...] = jnp.zeros_like(y_ref)


def _ffn(x, g, scale, wg, wu, wd, tile_expert, tile_used, *, tm, tf=512):
    M, D = x.shape
    F = wg.shape[2]
    tf = _pick_tile(F, tf)
    grid_spec = pltpu.PrefetchScalarGridSpec(
        num_scalar_prefetch=2,
        grid=(M // tm, F // tf),
        in_specs=[
            pl.BlockSpec((tm, D), lambda i, j, e, u: (i, 0)),
            pl.BlockSpec((1, D), lambda i, j, e, u: (0, 0)),
            pl.BlockSpec((tm, 1), lambda i, j, e, u: (i, 0)),
            pl.BlockSpec((None, D, tf), lambda i, j, e, u: (e[i], 0, j)),
            pl.BlockSpec((None, D, tf), lambda i, j, e, u: (e[i], 0, j)),
            pl.BlockSpec((None, tf, D), lambda i, j, e, u: (e[i], j, 0)),
        ],
        out_specs=pl.BlockSpec((tm, D), lambda i, j, e, u: (i, 0)),
        scratch_shapes=[pltpu.VMEM((tm, D), jnp.bfloat16), pltpu.VMEM((tm, D), jnp.float32)],
    )
    return pl.pallas_call(
        _ffn_body,
        grid_spec=grid_spec,
        out_shape=jax.ShapeDtypeStruct((M, D), jnp.float32),
        compiler_params=pltpu.CompilerParams(
            dimension_semantics=("parallel", "arbitrary"), vmem_limit_bytes=VMEM_LIMIT_BYTES),
        name="ffn",
    )(tile_expert, tile_used, x, g.reshape(1, D), scale, wg, wu, wd)


def _router_body(x_ref, g_ref, w_ref, o_ref):
    x = x_ref[...]
    h = x * lax.rsqrt(jnp.mean(x * x, axis=-1, keepdims=True) + EPS) * g_ref[...]
    o_ref[...] = jnp.dot(h, w_ref[...], preferred_element_type=jnp.float32, precision=lax.Precision.HIGHEST)


def _router_logits(x, g, router, *, tm=512):
    M, D = x.shape
    E = router.shape[1]
    tm = _pick_tile(M, tm)
    lanes = 128
    w = jnp.pad(router, ((0, 0), (0, lanes - E)))
    out = pl.pallas_call(
        _router_body,
        grid=(M // tm,),
        in_specs=[pl.BlockSpec((tm, D), lambda i: (i, 0)), pl.BlockSpec((1, D), lambda i: (0, 0)),
                  pl.BlockSpec((D, lanes), lambda i: (0, 0))],
        out_specs=pl.BlockSpec((tm, lanes), lambda i: (i, 0)),
        out_shape=jax.ShapeDtypeStruct((M, lanes), jnp.float32),
        compiler_params=pltpu.CompilerParams(dimension_semantics=("parallel",), vmem_limit_bytes=VMEM_LIMIT_BYTES),
        name="router",
    )(x, g.reshape(1, D), w)
    return out[:, :E]


def _rwkv_scan_body(r_ref, w_ref, k_ref, v_ref, kk_ref, b_ref, s0_ref, y_ref, st_ref, s_scr, *, sub):
    j = pl.program_id(1)
    n_b, t_blk, _ = r_ref.shape
    d = HEAD_DIM

    @pl.when(j == 0)
    def _():
        s_scr[...] = s0_ref[...]

    eye = lax.broadcasted_iota(jnp.int32, (d, d), 0) == lax.broadcasted_iota(jnp.int32, (d, d), 1)

    def sub_block(i, carry):
        t0 = pl.multiple_of(i * sub, sub)
        for bb in range(n_b):
            blk = [ref[bb, pl.ds(t0, sub), :] for ref in (r_ref, w_ref, k_ref, v_ref, kk_ref, b_ref)]
            for h in range(RWKV_HEADS):
                cols = slice(h * d, (h + 1) * d)
                S = s_scr[bb, h]
                y_rows = []
                for t in range(sub):
                    r_t, w_t, k_t, v_t, kk_t, b_t = [x[t:t + 1, cols] for x in blk]
                    sa = jnp.sum(S * kk_t, axis=1, keepdims=True)
                    v_col = jnp.sum(jnp.where(eye, v_t, 0.0), axis=1, keepdims=True)
                    S = S * w_t - sa * b_t + v_col * k_t
                    y_col = jnp.sum(S * r_t, axis=1, keepdims=True)
                    y_rows.append(jnp.sum(jnp.where(eye, y_col, 0.0), axis=0, keepdims=True))
                s_scr[bb, h] = S
                y_ref[bb, pl.ds(t0, sub), cols] = jnp.concatenate(y_rows, axis=0)
        return carry

    lax.fori_loop(0, t_blk // sub, sub_block, 0)

    @pl.when(j == pl.num_programs(1) - 1)
    def _():
        st_ref[...] = s_scr[...]


def _rwkv_scan(r, w, k, v, kk, b, s0, *, n_b=2, t_blk=256):
    B, T, W = r.shape
    n_b = _pick_tile(B, n_b)
    t_blk = _pick_tile(T, t_blk)
    sub = 8 if t_blk % 8 == 0 else t_blk
    seq = pl.BlockSpec((n_b, t_blk, W), lambda i, j: (i, j, 0))
    st = pl.BlockSpec((n_b, RWKV_HEADS, HEAD_DIM, HEAD_DIM), lambda i, j: (i, 0, 0, 0))
    return pl.pallas_call(
        functools.partial(_rwkv_scan_body, sub=sub),
        grid=(B // n_b, T // t_blk),
        in_specs=[seq] * 6 + [st],
        out_specs=[seq, st],
        out_shape=[jax.ShapeDtypeStruct((B, T, W), jnp.float32),
                   jax.ShapeDtypeStruct((B, RWKV_HEADS, HEAD_DIM, HEAD_DIM), jnp.float32)],
        scratch_shapes=[pltpu.VMEM((n_b, RWKV_HEADS, HEAD_DIM, HEAD_DIM), jnp.float32)],
        compiler_params=pltpu.CompilerParams(
            dimension_semantics=("parallel", "arbitrary"), vmem_limit_bytes=VMEM_LIMIT_BYTES),
        name="rwkv_scan",
    )(r, w, k, v, kk, b, s0)


RWKV_CHUNK = 64
RWKV_CHUNKS_PER_STEP = 4


_F32_DOT = dict(preferred_element_type=jnp.float32, precision=lax.Precision.HIGHEST)


def _dot(a, b):
    return jnp.dot(a, b, **_F32_DOT)


def _dot_t(a, b):
    return lax.dot_general(a, b, (((1,), (1,)), ((), ())), **_F32_DOT)


def _dot_0(a, b):
    return lax.dot_general(a, b, (((0,), (0,)), ((), ())), **_F32_DOT)


_NN = (((1,), (0,)), ((), ()))
_NT = (((1,), (1,)), ((), ()))


def _split(x):
    hi = x.astype(jnp.bfloat16)
    return hi, (x - hi.astype(jnp.float32)).astype(jnp.bfloat16)


def _mm3(a, b, dims):
    dot = lambda x, y: lax.dot_general(x, y, dims, preferred_element_type=jnp.float32)
    return dot(a[0], b[0]) + (dot(a[0], b[1]) + dot(a[1], b[0]))


def _rwkv_chunk_body(r_ref, lw_ref, k_ref, v_ref, kk_ref, b_ref, qe_ref, y0_ref, pm_ref, z_ref):
    L, d = RWKV_CHUNK, HEAD_DIM
    n = r_ref.shape[1]
    row = lax.broadcasted_iota(jnp.int32, (n, n), 0)
    col = lax.broadcasted_iota(jnp.int32, (n, n), 1)
    same = (row // L) == (col // L)
    lower = same & (row >= col)
    strict = same & (row > col)
    ones_lower = jnp.where(lower, 1.0, 0.0)
    eye = jnp.where(row == col, 1.0, 0.0)
    eye_d = eye[:d, :d]
    ones_bf = ones_lower.astype(jnp.bfloat16)
    for h in range(RWKV_HEADS):
        cols = slice(h * d, (h + 1) * d)
        r, lw, k, v, kk, b = [ref[0, :, cols] for ref in (r_ref, lw_ref, k_ref, v_ref, kk_ref, b_ref)]
        lw_hi, lw_lo = _split(lw)
        lw_rest = (lw - lw_hi.astype(jnp.float32) - lw_lo.astype(jnp.float32)).astype(jnp.bfloat16)
        G = sum(jnp.dot(ones_bf, t, preferred_element_type=jnp.float32) for t in (lw_hi, lw_lo, lw_rest))
        g_inv = jnp.exp(-G)
        kap, bt, kt, rt = kk * jnp.exp(G - lw), b * g_inv, k * g_inv, r * jnp.exp(G)
        kap2, bt2, kt2, rt2, v2 = _split(kap), _split(bt), _split(kt), _split(rt), _split(v)
        N = jnp.where(strict, _mm3(kap2, bt2, _NT), 0.0)
        Mk = jnp.where(strict, _mm3(kap2, kt2, _NT), 0.0)
        RB = jnp.where(lower, _mm3(rt2, bt2, _NT), 0.0)
        RK = jnp.where(lower, _mm3(rt2, kt2, _NT), 0.0)
        X, P2 = eye - N, _split(N)
        for _ in range(L.bit_length() - 2):
            P2 = _split(_mm3(P2, P2, _NN))
            X = X + _mm3(_split(X), P2, _NN)
        X2, RB2 = _split(X), _split(RB)
        A = _mm3(X2, kap2, _NN)
        C = _mm3(X2, _split(_mm3(_split(Mk), v2, _NN)), _NN)
        qe_ref[0, h] = rt - _mm3(RB2, _split(A), _NN)
        y0_ref[0, h] = _mm3(_split(RK), v2, _NN) - _mm3(RB2, _split(C), _NN)
        for c in range(n // L):
            rows = slice(c * L, (c + 1) * L)
            g_end = jnp.exp(G[(c + 1) * L - 1:(c + 1) * L, :])
            pm_ref[0, h, c] = (eye_d - _dot_0(A[rows], bt[rows])) * g_end
            z_ref[0, h, c] = (_dot_0(v[rows], kt[rows]) - _dot_0(C[rows], bt[rows])) * g_end


def _rwkv_walk_body(qe_ref, y0_ref, pm_ref, z_ref, s0_ref, y_ref, st_ref, s_scr):
    j = pl.program_id(0)
    B, H, n_c = pm_ref.shape[:3]
    L = RWKV_CHUNK

    @pl.when(j == 0)
    def _():
        s_scr[...] = s0_ref[...]

    def one_chunk(c, carry):
        t0 = pl.multiple_of(c * L, L)
        for bb in range(B):
            for h in range(H):
                S = s_scr[bb, h]
                y_ref[bb, h, pl.ds(t0, L), :] = _dot_t(qe_ref[bb, h, pl.ds(t0, L), :], S) + y0_ref[bb, h, pl.ds(t0, L), :]
                s_scr[bb, h] = _dot(S, pm_ref[bb, h, c]) + z_ref[bb, h, c]
        return carry

    lax.fori_loop(0, n_c, one_chunk, 0)

    @pl.when(j == pl.num_programs(0) - 1)
    def _():
        st_ref[...] = s_scr[...]


def _rwkv_chunked(r, lw, k, v, kk, b, s0, *, chunks_per_step=16):
    B, T, W = r.shape
    H, d, L = RWKV_HEADS, HEAD_DIM, RWKV_CHUNK
    n_c = T // L
    group = _pick_tile(n_c, RWKV_CHUNKS_PER_STEP)
    seq = pl.BlockSpec((1, group * L, W), lambda i, c: (i, c, 0))
    per_tok = pl.BlockSpec((1, H, group * L, d), lambda i, c: (i, 0, c, 0))
    per_chunk = pl.BlockSpec((1, H, group, d, d), lambda i, c: (i, 0, c, 0, 0))
    qe, y0, pm, z = pl.pallas_call(
        _rwkv_chunk_body,
        grid=(B, n_c // group),
        in_specs=[seq] * 6,
        out_specs=[per_tok, per_tok, per_chunk, per_chunk],
        out_shape=[jax.ShapeDtypeStruct((B, H, T, d), jnp.float32)] * 2
                  + [jax.ShapeDtypeStruct((B, H, n_c, d, d), jnp.float32)] * 2,
        compiler_params=pltpu.CompilerParams(
            dimension_semantics=("parallel", "parallel"), vmem_limit_bytes=VMEM_LIMIT_BYTES),
        name="rwkv_chunk",
    )(r, lw, k, v, kk, b)
    cs = _pick_tile(n_c, chunks_per_step)
    tok = pl.BlockSpec((B, H, cs * L, d), lambda j: (0, 0, j, 0))
    chk = pl.BlockSpec((B, H, cs, d, d), lambda j: (0, 0, j, 0, 0))
    st = pl.BlockSpec((B, H, d, d), lambda j: (0, 0, 0, 0))
    y, s_t = pl.pallas_call(
        _rwkv_walk_body,
        grid=(n_c // cs,),
        in_specs=[tok, tok, chk, chk, st],
        out_specs=[tok, st],
        out_shape=[jax.ShapeDtypeStruct((B, H, T, d), jnp.float32), jax.ShapeDtypeStruct((B, H, d, d), jnp.float32)],
        scratch_shapes=[pltpu.VMEM((B, H, d, d), jnp.float32)],
        compiler_params=pltpu.CompilerParams(
            dimension_semantics=("arbitrary",), vmem_limit_bytes=VMEM_LIMIT_BYTES),
        name="rwkv_walk",
    )(qe, y0, pm, z, s0)
    return jnp.transpose(y, (0, 2, 1, 3)).reshape(B, T, W), s_t


SEL_CHUNK = 512
WIN_SPAN = WINDOW + Q_BLOCK


def _nsa_scores(k_aug, qT_ref):
    return [jnp.dot(k_aug, qT_ref[0, 0, r], preferred_element_type=jnp.float32) for r in range(NSA_GROUP)]


def _nsa_softmax_cols(s_list, neg):
    out = []
    for s in s_list:
        s = s + neg
        m = jnp.max(s, axis=0, keepdims=True)
        e = jnp.exp(s - m)
        inv = jnp.where(m > 0.5 * NEG, 1.0, 0.0) / jnp.maximum(jnp.sum(e, axis=0, keepdims=True), TINY)
        out.append(e * inv)
    return out


def _nsa_prompt_body(qT_ref, kc_ref, vcT_ref, ks_ref, vsT_ref, kw_ref, vwT_ref, covT_ref, o_ref,
                     sel_scr, m_scr, l_scr, acc_scr):
    i = pl.program_id(2)
    R = NSA_GROUP
    n_cmp = kc_ref.shape[2]
    n_sel = covT_ref.shape[0]
    q0 = i * Q_BLOCK
    pos = q0 + lax.broadcasted_iota(jnp.int32, (1, Q_BLOCK), 1)
    posf = pos.astype(jnp.float32)
    bf = jnp.bfloat16

    c_end = (lax.broadcasted_iota(jnp.int32, (n_cmp, Q_BLOCK), 0) * CMP_STRIDE + (CMP_LEN - 1)).astype(jnp.float32)
    neg_c = jnp.where(posf - c_end >= 0.0, 0.0, NEG)
    p_c = _nsa_softmax_cols(_nsa_scores(kc_ref[0, 0], qT_ref), neg_c)
    vcT = vcT_ref[0, 0]
    o_c = [jnp.dot(vcT, p.astype(bf), preferred_element_type=jnp.float32) for p in p_c]
    for r in range(R):
        o_ref[0, 0, 0, r] = o_c[r]
    p_sum = (p_c[0] + p_c[1]) + (p_c[2] + p_c[3])
    imp = jnp.dot(covT_ref[...], p_sum, preferred_element_type=jnp.float32, precision=lax.Precision.HIGHEST)
    blk = lax.broadcasted_iota(jnp.int32, (n_sel, Q_BLOCK), 0)
    avail = blk * SEL_LEN <= pos
    forced = (blk == jnp.right_shift(pos, 6)) | (blk == 0)
    imp = jnp.where(avail, jnp.where(forced, FORCE, imp), -FORCE)

    sub8 = lax.broadcasted_iota(jnp.int32, (8, Q_BLOCK), 0)
    for jj in range(n_sel // 8):
        vj = imp[8 * jj:8 * jj + 8]
        cnt = jnp.zeros((8, Q_BLOCK), jnp.float32)
        for k in range(n_sel):
            row = imp[k:k + 1]
            if k < 8 * jj:
                cnt = cnt + jnp.where(row >= vj, 1.0, 0.0)
            elif k >= 8 * jj + 8:
                cnt = cnt + jnp.where(row > vj, 1.0, 0.0)
            else:
                cnt = cnt + jnp.where(sub8 > (k - 8 * jj), jnp.where(row >= vj, 1.0, 0.0),
                                      jnp.where(row > vj, 1.0, 0.0))
        sel_scr[8 * jj:8 * jj + 8, :] = jnp.where(cnt < float(SEL_TOP), 1.0, 0.0)

    m_scr[...] = jnp.full_like(m_scr, NEG)
    l_scr[...] = jnp.zeros_like(l_scr)
    acc_scr[...] = jnp.zeros_like(acc_scr)
    key_iota = lax.broadcasted_iota(jnp.int32, (SEL_CHUNK, Q_BLOCK), 0).astype(jnp.float32)
    blocks_per_chunk = SEL_CHUNK // SEL_LEN

    def chunk(c, carry):
        k0 = pl.multiple_of(c * SEL_CHUNK, SEL_CHUNK)
        s_all = _nsa_scores(ks_ref[0, 0, pl.ds(k0, SEL_CHUNK), :], qT_ref)
        dist = (posf - k0.astype(jnp.float32)) - key_iota
        picked = jnp.concatenate(
            [jnp.broadcast_to(sel_scr[pl.ds(c * blocks_per_chunk + b, 1), :], (SEL_LEN, Q_BLOCK))
             for b in range(blocks_per_chunk)], axis=0)
        neg = jnp.where((picked > 0.5) & (dist >= 0.0), 0.0, NEG)
        m_old = [m_scr[r] for r in range(R)]
        m_new, p_all, l_add = [], [], []
        for r in range(R):
            s = s_all[r] + neg
            m = jnp.maximum(m_old[r], jnp.max(s, axis=0, keepdims=True))
            p = jnp.exp(s - m)
            m_new.append(m)
            l_add.append(jnp.sum(p, axis=0, keepdims=True))
            p_all.append(p.astype(bf))
        vsT = vsT_ref[0, 0, :, pl.ds(k0, SEL_CHUNK)]
        pv = [jnp.dot(vsT, p, preferred_element_type=jnp.float32) for p in p_all]
        for r in range(R):
            alpha = jnp.exp(m_old[r] - m_new[r])
            l_scr[r] = alpha * l_scr[r] + l_add[r]
            acc_scr[r] = alpha * acc_scr[r] + pv[r]
            m_scr[r] = m_new[r]
        return carry

    lax.fori_loop(0, q0 // SEL_CHUNK + 1, chunk, 0)
    for r in range(R):
        o_ref[1, 0, 0, r] = acc_scr[r] / jnp.maximum(l_scr[r], TINY)

    k0w = pl.multiple_of(jnp.maximum(q0 - WINDOW, 0), Q_BLOCK)
    d_win = (posf - k0w.astype(jnp.float32)) - lax.broadcasted_iota(jnp.int32, (WIN_SPAN, Q_BLOCK), 0).astype(jnp.float32)
    neg_w = jnp.where((d_win >= 0.0) & (d_win < float(WINDOW)), 0.0, NEG)
    p_w = _nsa_softmax_cols(_nsa_scores(kw_ref[0, 0, pl.ds(k0w, WIN_SPAN), :], qT_ref), neg_w)
    vwT = vwT_ref[0, 0, :, pl.ds(k0w, WIN_SPAN)]
    o_w = [jnp.dot(vwT, p.astype(bf), preferred_element_type=jnp.float32) for p in p_w]
    for r in range(R):
        o_ref[2, 0, 0, r] = o_w[r]


def _nsa_prompt_attn(q, kc, vc, ks, vs, kw, vw):
    B, T, H, d = q.shape
    G, R = NSA_KV_HEADS, NSA_GROUP
    n_c = kc.shape[1]
    n_cmp = -(-n_c // 128) * 128
    n_sel = T // SEL_LEN
    bf = jnp.bfloat16
    lanes = 128

    def key_rows(k, key_pos):
        n = k.shape[1]
        extra = jnp.zeros((n, lanes - d), jnp.float32).at[:, 0].set((key_pos // SEL_LEN).astype(jnp.float32))
        extra = extra.at[:, 1].set((key_pos % SEL_LEN).astype(jnp.float32))
        extra = jnp.broadcast_to(extra[None, None], (B, G, n, lanes - d))
        return jnp.concatenate([jnp.transpose(k, (0, 2, 1, 3)), extra], axis=-1).astype(bf)

    cols = lambda t: jnp.transpose(t, (0, 2, 3, 1)).astype(bf)
    slopes = (2.0 ** -jnp.arange(1, H + 1, dtype=jnp.float32)).reshape(G, R)
    q_extra = jnp.zeros((G, R, lanes - d), jnp.float32).at[:, :, 0].set(SEL_LEN * slopes).at[:, :, 1].set(slopes)
    qT = jnp.transpose((q * (d ** -0.5)).reshape(B, T, G, R, d), (0, 2, 3, 4, 1))
    qT = jnp.concatenate([qT, jnp.broadcast_to(q_extra[None, :, :, :, None], (B, G, R, lanes - d, T))],
                         axis=3).astype(bf)
    pad_c = ((0, 0), (0, n_cmp - n_c), (0, 0), (0, 0))
    kc_r = key_rows(jnp.pad(kc, pad_c), jnp.arange(n_cmp, dtype=jnp.int32) * CMP_STRIDE + (CMP_LEN - 1))
    vc_c = cols(jnp.pad(vc, pad_c))
    tok = jnp.arange(T, dtype=jnp.int32)
    rows = lambda t: key_rows(t, tok)
    c_start = jnp.arange(n_cmp, dtype=jnp.int32) * CMP_STRIDE
    s_start = jnp.arange(n_sel, dtype=jnp.int32) * SEL_LEN
    covT = jnp.maximum(jnp.minimum(c_start[None, :] + CMP_LEN, s_start[:, None] + SEL_LEN)
                       - jnp.maximum(c_start[None, :], s_start[:, None]), 0).astype(jnp.float32) / CMP_LEN
    full_r = lambda n: pl.BlockSpec((1, 1, n, lanes), lambda b, g, i: (b, g, 0, 0))
    full_c = lambda n: pl.BlockSpec((1, 1, d, n), lambda b, g, i: (b, g, 0, 0))
    oT = pl.pallas_call(
        _nsa_prompt_body,
        grid=(B, G, T // Q_BLOCK),
        in_specs=[pl.BlockSpec((1, 1, R, lanes, Q_BLOCK), lambda b, g, i: (b, g, 0, 0, i)),
                  full_r(n_cmp), full_c(n_cmp), full_r(T), full_c(T), full_r(T), full_c(T),
                  pl.BlockSpec((n_sel, n_cmp), lambda b, g, i: (0, 0))],
        out_specs=pl.BlockSpec((3, 1, 1, R, d, Q_BLOCK), lambda b, g, i: (0, b, g, 0, 0, i)),
        out_shape=jax.ShapeDtypeStruct((3, B, G, R, d, T), jnp.float32),
        scratch_shapes=[pltpu.VMEM((n_sel, Q_BLOCK), jnp.float32),
                        pltpu.VMEM((R, 1, Q_BLOCK), jnp.float32),
                        pltpu.VMEM((R, 1, Q_BLOCK), jnp.float32),
                        pltpu.VMEM((R, d, Q_BLOCK), jnp.float32)],
        compiler_params=pltpu.CompilerParams(
            dimension_semantics=("parallel", "parallel", "arbitrary"), vmem_limit_bytes=VMEM_LIMIT_BYTES),
        name="nsa_prompt",
    )(qT, kc_r, vc_c, rows(ks), cols(vs), rows(kw), cols(vw), covT)
    return jnp.transpose(oT, (0, 1, 5, 2, 3, 4)).reshape(3, B, T, H, d)


PAGES_PER_STEP = 8
def _compress_pages_body(pt_ref, *rest):
    page_refs = rest[:PAGES_PER_STEP]
    w1_ref, c0_ref, w2_ref, ones_ref, gain_ref, o_ref, x_scr = rest[PAGES_PER_STEP:]
    j = pl.program_id(1)
    rows = page_refs[0].shape[0]
    for i, ref in enumerate(page_refs):
        x_scr[pl.ds(pl.multiple_of((j * PAGES_PER_STEP + i) * rows, rows), rows), :] = ref[...]

    @pl.when(j == pl.num_programs(1) - 1)
    def _():
        n, half = x_scr.shape[0], w2_ref.shape[0]
        ab = jnp.dot(x_scr[...].astype(jnp.bfloat16), w1_ref[...], preferred_element_type=jnp.float32)
        top, bottom = ab[:, :half], ab[:, half:]
        nxt = jnp.concatenate([bottom[1:], jnp.zeros((1, half), jnp.float32)], axis=0)
        hid = jax.nn.gelu(top + nxt + c0_ref[...])
        y = jnp.dot(hid.astype(jnp.bfloat16), w2_ref[...], preferred_element_type=jnp.float32)
        mean_sq = jnp.dot(y * y, ones_ref[...], preferred_element_type=jnp.float32, precision=lax.Precision.HIGHEST)
        is_key = lax.broadcasted_iota(jnp.int32, (n, half), 1) < half // 2
        o_ref[0] = jnp.where(is_key, y * lax.rsqrt(mean_sq + EPS) * gain_ref[...], y)


def _compress_pages(pool, pages, n_batch, pe, w1, w2, k_gain):
    d, G = HEAD_DIM, NSA_KV_HEADS
    N, page_rows, W = pool.shape
    cpp = page_rows // CMP_STRIDE
    n_pages = pages.shape[0] // n_batch
    halves = CMP_LEN // CMP_STRIDE
    w1r = w1.reshape(2, halves, CMP_STRIDE, d, d)[jnp.array([0] * G + [1] * G)]
    eye = jnp.eye(2 * G, dtype=jnp.float32)
    w_big = jnp.transpose(w1r, (2, 0, 3, 1, 4))[:, :, :, :, None, :] * eye[None, :, None, None, :, None]
    w_big = w_big.reshape(CMP_STRIDE * W, halves * W).astype(jnp.bfloat16)
    c0 = jnp.einsum('kn,kne->ke', pe.reshape(2, CMP_LEN * d), w1, precision=lax.Precision.HIGHEST)
    c0 = jnp.repeat(c0, G, axis=0).reshape(1, W)
    w2_big = (w2[jnp.array([0] * G + [1] * G)][:, :, None, :] * eye[:, None, :, None]).reshape(W, W).astype(jnp.bfloat16)
    ones_blk = jnp.kron(eye, jnp.full((d, d), 1.0 / d, jnp.float32))
    gain = jnp.concatenate([jnp.tile(k_gain, G), jnp.ones((G * d,), jnp.float32)]).reshape(1, W)
    pool_chunks = pool.reshape(N, cpp, CMP_STRIDE * W)
    page_spec = lambda k: pl.BlockSpec((None, cpp, CMP_STRIDE * W),
                                       lambda b, j, pt: (pt[b * n_pages + j * PAGES_PER_STEP + k], 0, 0))
    const = lambda shape: pl.BlockSpec(shape, lambda b, j, pt: (0, 0))
    grid_spec = pltpu.PrefetchScalarGridSpec(
        num_scalar_prefetch=1,
        grid=(n_batch, n_pages // PAGES_PER_STEP),
        in_specs=[page_spec(k) for k in range(PAGES_PER_STEP)]
                 + [const(w_big.shape), const((1, W)), const((W, W)), const((W, W)), const((1, W))],
        out_specs=pl.BlockSpec((1, n_pages * cpp, W), lambda b, j, pt: (b, 0, 0)),
        scratch_shapes=[pltpu.VMEM((n_pages * cpp, CMP_STRIDE * W), jnp.float32)],
    )
    return pl.pallas_call(
        _compress_pages_body,
        grid_spec=grid_spec,
        out_shape=jax.ShapeDtypeStruct((n_batch, n_pages * cpp, W), jnp.float32),
        compiler_params=pltpu.CompilerParams(
            dimension_semantics=("parallel", "arbitrary"), vmem_limit_bytes=VMEM_LIMIT_BYTES),
        name="compress_pages",
    )(pages, *([pool_chunks] * PAGES_PER_STEP), w_big, c0, w2_big, ones_blk, gain)


def _rms_norm(x, g):
    xf = x.astype(jnp.float32)
    y = xf * lax.rsqrt(jnp.mean(xf * xf, axis=-1, keepdims=True) + EPS)
    return (y * g.astype(jnp.float32)).astype(x.dtype)


def _group_norm(x, g, eps):
    xf = x.astype(jnp.float32)
    mu = jnp.mean(xf, axis=-1, keepdims=True)
    var = jnp.mean(jnp.square(xf - mu), axis=-1, keepdims=True)
    return (xf - mu) * lax.rsqrt(var + eps) * g.astype(jnp.float32)


def _masked_softmax(s, mask):
    s = jnp.where(mask, s, NEG)
    m = jnp.max(s, axis=-1, keepdims=True)
    e = jnp.where(mask, jnp.exp(s - m), 0.0)
    return e / jnp.maximum(jnp.sum(e, axis=-1, keepdims=True), TINY)


def _alibi_slopes(n):
    return 2.0 ** (-8.0 * jnp.arange(1, n + 1, dtype=jnp.float32) / n)


def _rotary(x, pos):
    half = x.shape[-1] // 2
    freqs = ROPE_BASE ** (-jnp.arange(half, dtype=jnp.float32) / half)
    ang = pos.astype(jnp.float32)[:, None] * freqs[None, :]
    cos, sin = jnp.cos(ang)[None, :, None, :], jnp.sin(ang)[None, :, None, :]
    x1, x2 = x[..., :half], x[..., half:]
    return jnp.concatenate([x1 * cos - x2 * sin, x1 * sin + x2 * cos], axis=-1)


def _nsa_compress(rows, pe, w1, w2):
    B, T, G, d = rows.shape
    n_cmp = (T - CMP_LEN) // CMP_STRIDE + 1
    idx = (jnp.arange(n_cmp, dtype=jnp.int32) * CMP_STRIDE)[:, None] + jnp.arange(CMP_LEN, dtype=jnp.int32)[None, :]
    blk = rows[:, idx] + pe[None, None, :, None, :]
    blk = jnp.transpose(blk, (0, 1, 3, 2, 4)).reshape(B, n_cmp, G, CMP_LEN * d)
    return jax.nn.gelu(blk @ w1) @ w2


def _to_sel_blocks(rows):
    B, T, G, d = rows.shape
    n_sel = -(-T // SEL_LEN)
    rows = jnp.pad(rows, ((0, 0), (0, n_sel * SEL_LEN - T), (0, 0), (0, 0)))
    return jnp.transpose(rows.reshape(B, n_sel, SEL_LEN, G, d), (0, 3, 1, 2, 4))


def _nsa_branches(q, pos, kc, vc, n_sel, sel_branch, kw, vw, pos_w, slopes):
    B, Tq, H, d = q.shape
    G, R = NSA_KV_HEADS, NSA_GROUP
    scale = d ** -0.5
    qg = q.reshape(B, Tq, G, R, d)
    sl = slopes.reshape(G, R)
    posf = pos.astype(jnp.float32)
    n_cmp = kc.shape[1]
    c_start = jnp.arange(n_cmp, dtype=jnp.int32) * CMP_STRIDE
    d_cmp = posf[:, None] - (c_start + CMP_LEN - 1).astype(jnp.float32)[None, :]
    s = jnp.einsum('btgrd,bngd->bgrtn', qg, kc).astype(jnp.float32) * scale - sl[None, :, :, None, None] * d_cmp
    p_cmp = _masked_softmax(s, (d_cmp >= 0.0)[None, None, None])
    o_cmp = jnp.einsum('bgrtn,bngd->btgrd', p_cmp.astype(vc.dtype), vc)
    s_start = jnp.arange(n_sel, dtype=jnp.int32) * SEL_LEN
    cover = jnp.maximum(jnp.minimum(c_start[:, None] + CMP_LEN, s_start[None, :] + SEL_LEN)
                        - jnp.maximum(c_start[:, None], s_start[None, :]), 0).astype(jnp.float32) / CMP_LEN
    imp = jnp.einsum('bgrtn,nj->bgtj', p_cmp, cover)
    blk = jnp.arange(n_sel, dtype=jnp.int32)
    avail = s_start[None, :] <= pos[:, None]
    forced = (blk[None, :] == (pos // SEL_LEN)[:, None]) | (blk[None, :] == 0)
    imp = jnp.where(avail, jnp.where(forced, FORCE, imp), -FORCE)
    n_top = min(SEL_TOP, n_sel)
    _, idx = lax.top_k(imp, n_top)
    o_sel = sel_branch(qg, idx)
    d_win = posf[:, None] - pos_w.astype(jnp.float32)[None, :]
    s = jnp.einsum('btgrd,bwgd->bgrtw', qg, kw).astype(jnp.float32) * scale - sl[None, :, :, None, None] * d_win
    win_mask = (d_win >= 0.0) & (d_win < WINDOW) & (pos_w >= 0)[None, :]
    p_win = _masked_softmax(s, win_mask[None, None, None])
    o_win = jnp.einsum('bgrtw,bwgd->btgrd', p_win.astype(vw.dtype), vw)
    return jnp.stack([o_cmp, o_sel, o_win]).reshape(3, B, Tq, H, d)


def _rwkv_group(u, shift0, S0, lp):
    B, T, _ = u.shape
    W = RWKV_WIDTH
    uf = u.astype(jnp.float32)
    prev = jnp.concatenate([shift0.astype(jnp.float32)[:, None], uf[:, :-1]], axis=1)
    um = uf + (prev - uf) * lp['rwkv_mu']
    r, k, v = um[..., :W], um[..., W:2 * W], um[..., 2 * W:3 * W]
    o = 3 * W
    wd = um[..., o:o + RWKV_W_RANK]
    ad = um[..., o + RWKV_W_RANK:o + RWKV_W_RANK + RWKV_A_RANK]
    gd = um[..., o + RWKV_W_RANK + RWKV_A_RANK:]
    w = lp['rwkv_w0'] + jnp.tanh(wd) @ lp['rwkv_w_up']
    log_decay = -jnp.exp(-jax.nn.softplus(-w) - 0.5)
    a = jax.nn.sigmoid(lp['rwkv_a0'] + ad @ lp['rwkv_a_up'])
    g = jax.nn.sigmoid(gd) @ lp['rwkv_g_up']
    kk = k * lp['rwkv_k_k']
    k = k * (1.0 + (a - 1.0) * lp['rwkv_k_a'])
    hd = lambda t: t.reshape(B, T, RWKV_HEADS, HEAD_DIM).astype(jnp.float32)
    r, k, v, a, g, kk = hd(r), hd(k), hd(v), hd(a), hd(g), hd(kk)
    kk = kk / jnp.maximum(jnp.sqrt(jnp.sum(kk * kk, axis=-1, keepdims=True)), 1e-12)
    flat = lambda t: t.reshape(B, T, W)
    if T % RWKV_CHUNK == 0:
        ys, S_T = _rwkv_chunked(flat(r), log_decay, flat(k), flat(v), flat(kk), flat(kk * a), S0.astype(jnp.float32))
    else:
        ys, S_T = _rwkv_scan(flat(r), jnp.exp(log_decay), flat(k), flat(v), flat(kk), flat(kk * a),
                             S0.astype(jnp.float32))
    y = _group_norm(ys.reshape(B, T, RWKV_HEADS, HEAD_DIM), lp['rwkv_ln_g'], RWKV_GN_EPS) + lp['rwkv_ln_b']
    y = y + jnp.sum(r * k * lp['rwkv_r_k'], axis=-1, keepdims=True) * v
    y = y * g
    return y.reshape(B, T, W).astype(u.dtype), u[:, -1], S_T


def _retention_group(u, pos, S0, ln_g):
    B, T, _ = u.shape
    uf = u.astype(jnp.float32)
    q, k, v, g = [t.reshape(B, T, RET_HEADS, HEAD_DIM) for t in jnp.split(uf, 4, axis=-1)]
    q = _rotary(q, pos)
    k = _rotary(k, pos) * HEAD_DIM ** -0.5
    lg = jnp.log(1.0 - 2.0 ** (-5.0 - jnp.arange(RET_HEADS, dtype=jnp.float32)))
    C = RET_CHUNK if T % RET_CHUNK == 0 else T
    nC = T // C
    n = jnp.arange(C, dtype=jnp.float32)
    diff = n[:, None] - n[None, :]
    dmask = jnp.where(diff[None] >= 0, jnp.exp(jnp.maximum(diff, 0.0)[None] * lg[:, None, None]), 0.0)
    q_dec = jnp.exp((n[:, None] + 1.0) * lg[None, :])
    k_dec = jnp.exp((C - 1.0 - n)[:, None] * lg[None, :])
    s_dec = jnp.exp(C * lg)

    def chunk(S, inp):
        qc, kc, vc = inp
        att = jnp.einsum('bnhd,bmhd->bhnm', qc, kc) * dmask
        out = jnp.einsum('bhnm,bmhe->bnhe', att, vc) + jnp.einsum('bnhd,bhde->bnhe', qc, S) * q_dec[None, :, :, None]
        S = S * s_dec[None, :, None, None] + jnp.einsum('bmhd,bmhe->bhde', kc * k_dec[None, :, :, None], vc)
        return S, out

    to_chunks = lambda t: jnp.moveaxis(t.reshape(B, nC, C, RET_HEADS, HEAD_DIM), 1, 0)
    S_T, o = lax.scan(chunk, S0.astype(jnp.float32), (to_chunks(q), to_chunks(k), to_chunks(v)))
    o = jnp.moveaxis(o, 0, 1).reshape(B, T, RET_HEADS, HEAD_DIM)
    y = jax.nn.silu(g) * _group_norm(o, ln_g, GN_EPS)
    return y.reshape(B, T, RET_WIDTH).astype(u.dtype), S_T


def _mixing_sublayer(x, lp, past_len, past_cmp, past_sel, win_buf, rwkv_S, rwkv_shift, ret_S):
    B, T, _ = x.shape
    h = _rms_norm(x, lp['norm_attn'])
    P = h @ lp['w_in']
    c = P[..., :NSA_IN]
    q = _rms_norm(c[..., :NSA_WIDTH].reshape(B, T, NSA_HEADS, HEAD_DIM), lp['nsa_q_norm'])
    kv = c[..., NSA_WIDTH:NSA_WIDTH + 6 * NSA_KV_COLS].reshape(B, T, 3, 2, NSA_KV_HEADS, HEAD_DIM)
    kv_cmp, kv_sel, kv_win = kv[:, :, 0], kv[:, :, 1], kv[:, :, 2]
    gates = jax.nn.sigmoid(c[..., NSA_WIDTH + 6 * NSA_KV_COLS:].astype(jnp.float32)).reshape(B, T, NSA_HEADS, 3)
    pos = past_len + jnp.arange(T, dtype=jnp.int32)
    slopes = _alibi_slopes(NSA_HEADS)
    prompt = past_cmp is None
    if prompt:
        rows_cmp, rows_sel, rows_win = kv_cmp, kv_sel, kv_win
        new_win = kv_win[:, T - min(WINDOW, T):]
    else:
        rows_win = jnp.concatenate([win_buf, kv_win.astype(win_buf.dtype)], axis=1)
        new_win = rows_win[:, T:]
    k_norm = lp['nsa_k_norm']
    if prompt:
        kc = _rms_norm(_nsa_compress(rows_cmp[:, :, 0], lp['nsa_cmp_pe'][0], lp['nsa_cmp_w1'][0], lp['nsa_cmp_w2'][0]), k_norm[0])
        vc = _nsa_compress(rows_cmp[:, :, 1], lp['nsa_cmp_pe'][1], lp['nsa_cmp_w1'][1], lp['nsa_cmp_w2'][1])
    else:
        n_cmp = (past_len + T - CMP_LEN) // CMP_STRIDE + 1
        assert (n_cmp - 1) * CMP_STRIDE + CMP_LEN <= past_len
        kcv = _compress_pages(past_cmp[0], past_cmp[1], B, lp['nsa_cmp_pe'], lp['nsa_cmp_w1'], lp['nsa_cmp_w2'], k_norm[0])
        kcv = kcv[:, :n_cmp].reshape(B, n_cmp, 2, NSA_KV_HEADS, HEAD_DIM)
        kc, vc = kcv[:, :, 0], kcv[:, :, 1]
    kw = _rms_norm(rows_win[:, :, 0], k_norm[2])
    vw = rows_win[:, :, 1]
    if prompt:
        o3 = _nsa_prompt_attn(q, kc, vc, _rms_norm(rows_sel[:, :, 0], k_norm[1]), rows_sel[:, :, 1], kw, vw)
    else:
        G, R, d = NSA_KV_HEADS, NSA_GROUP, HEAD_DIM
        rows_sel = jnp.concatenate([past_sel, kv_sel.astype(past_sel.dtype)], axis=1)
        ks_blk = _to_sel_blocks(_rms_norm(rows_sel[:, :, 0], k_norm[1]))
        vs_blk = _to_sel_blocks(rows_sel[:, :, 1])
        n_sel = ks_blk.shape[2]
        sl = slopes.reshape(G, R)

        def sel_branch(qg, idx):
            b_i = jnp.arange(B)[:, None, None, None]
            g_i = jnp.arange(G)[None, :, None, None]
            ks_g = ks_blk[b_i, g_i, idx]
            vs_g = vs_blk[b_i, g_i, idx]
            kpos = idx[..., None] * SEL_LEN + jnp.arange(SEL_LEN, dtype=jnp.int32)
            d_sel = (pos[None, None, :, None, None] - kpos).astype(jnp.float32)[:, :, None]
            s = (jnp.einsum('btgrd,bgtnsd->bgrtns', qg, ks_g).astype(jnp.float32) * (d ** -0.5)
                 - sl[None, :, :, None, None, None] * d_sel)
            mask = jnp.broadcast_to(d_sel >= 0.0, s.shape)
            p_sel = _masked_softmax(s.reshape(B, G, R, T, -1), mask.reshape(B, G, R, T, -1)).reshape(s.shape)
            return jnp.einsum('bgrtns,bgtnsd->btgrd', p_sel.astype(vs_g.dtype), vs_g)

        wb = win_buf.shape[1]
        pos_w = past_len - wb + jnp.arange(wb + T, dtype=jnp.int32)
        o3 = _nsa_branches(q, pos, kc, vc, n_sel, sel_branch, kw, vw, pos_w, slopes)
    o_nsa = jnp.einsum('btha,abthd->bthd', gates.astype(o3.dtype), o3)
    o_nsa = _rms_norm(o_nsa, lp['nsa_out_norm']).reshape(B, T, NSA_WIDTH)
    y_rwkv, new_shift, new_rwkv = _rwkv_group(P[..., NSA_IN:NSA_IN + RWKV_IN], rwkv_shift, rwkv_S, lp)
    y_ret, new_ret = _retention_group(P[..., NSA_IN + RWKV_IN:], pos, ret_S, lp['ret_ln_g'])
    mix = jnp.concatenate([o_nsa, y_rwkv.astype(o_nsa.dtype), y_ret.astype(o_nsa.dtype)], axis=-1)
    x = x + mix @ lp['w_out']
    return x, (kv_cmp, kv_sel, new_win, new_rwkv, new_shift, new_ret)


FFN_ROW_TILE = 512


def _dense_ffn(x, g, wg, wu, wd):
    shp = x.shape
    x2 = x.reshape(-1, shp[-1])
    tm = _pick_tile(x2.shape[0], FFN_ROW_TILE)
    n_tiles = x2.shape[0] // tm
    ones = jnp.ones((x2.shape[0], 1), jnp.float32)
    y = _ffn(x2, g, ones, wg[None], wu[None], wd[None], jnp.zeros((n_tiles,), jnp.int32),
             jnp.ones((n_tiles,), jnp.int32), tm=tm)
    return (x2 + y).reshape(shp)


def _moe_ffn(xs, g, router, wg, wu, wd):
    D = xs[0].shape[-1]
    E, tm = router.shape[1], FFN_ROW_TILE
    flat = [x.reshape(-1, D) for x in xs]
    x2 = jnp.concatenate(flat, axis=0)
    logits = jnp.concatenate([_router_logits(x, g, router) for x in flat], axis=0)
    top_val, top_idx = lax.top_k(logits, TOP_K)
    gate = jax.nn.softmax(top_val, axis=-1)
    N = x2.shape[0]
    A = N * TOP_K
    flat_e, flat_w = top_idx.reshape(A).astype(jnp.int32), gate.reshape(A)
    order = jnp.argsort(flat_e, stable=True).astype(jnp.int32)
    counts = jnp.sum(jax.nn.one_hot(flat_e, E, dtype=jnp.int32), axis=0)
    padded = (counts + tm - 1) // tm * tm
    start, p_end = jnp.cumsum(counts) - counts, jnp.cumsum(padded)
    p_start = p_end - padded
    sorted_e = flat_e[order]
    dest = p_start[sorted_e] + (jnp.arange(A, dtype=jnp.int32) - start[sorted_e])
    P = -(-(A + E * (tm - 1)) // tm) * tm
    row_tok = jnp.zeros((P,), jnp.int32).at[dest].set(order // TOP_K)
    row_w = jnp.zeros((P,), jnp.float32).at[dest].set(flat_w[order])
    tile_start = jnp.arange(P // tm, dtype=jnp.int32) * tm
    tile_e = jnp.minimum(jnp.searchsorted(p_end, tile_start, side='right'), E - 1).astype(jnp.int32)
    tile_used = (tile_start < p_end[-1]).astype(jnp.int32)
    ys = _ffn(x2[row_tok], g, row_w[:, None], wg, wu, wd, tile_e, tile_used, tm=tm)
    slot = jnp.zeros((A,), jnp.int32).at[order].set(dest).reshape(N, TOP_K)
    y = x2 + ys[slot[:, 0]] + ys[slot[:, 1]]
    outs, off = [], 0
    for x in xs:
        n = x.size // D
        outs.append(y[off:off + n].reshape(x.shape))
        off += n
    return outs


def kernel(x_prompt, x_sample, cache_nsa_cmp, cache_nsa_sel, cache_nsa_win, state_rwkv, state_rwkv_shift,
           state_ret, page_table, norm_attn, norm_ffn, w_in, w_out, nsa_q_norm, nsa_k_norm, nsa_cmp_pe,
           nsa_cmp_w1, nsa_cmp_w2, nsa_out_norm, rwkv_mu, rwkv_w0, rwkv_w_up, rwkv_a0, rwkv_a_up, rwkv_g_up,
           rwkv_k_k, rwkv_k_a, rwkv_r_k, rwkv_ln_g, rwkv_ln_b, ret_ln_g, ffn_w_gate, ffn_w_up, ffn_w_down,
           moe_router, moe_w_gate, moe_w_up, moe_w_down):
    Bp, Bs = x_prompt.shape[0], x_sample.shape[0]
    past_len = page_table.shape[1] * cache_nsa_cmp.shape[2]
    xp, xs = x_prompt, x_sample
    outs_p, outs_s = [], []
    for l in range(DEPTH):
        lp = {'norm_attn': norm_attn[l], 'w_in': w_in[l], 'w_out': w_out[l], 'nsa_q_norm': nsa_q_norm[l],
              'nsa_k_norm': nsa_k_norm[l], 'nsa_cmp_pe': nsa_cmp_pe[l], 'nsa_cmp_w1': nsa_cmp_w1[l],
              'nsa_cmp_w2': nsa_cmp_w2[l], 'nsa_out_norm': nsa_out_norm[l], 'rwkv_mu': rwkv_mu[l],
              'rwkv_w0': rwkv_w0[l], 'rwkv_w_up': rwkv_w_up[l], 'rwkv_a0': rwkv_a0[l], 'rwkv_a_up': rwkv_a_up[l],
              'rwkv_g_up': rwkv_g_up[l], 'rwkv_k_k': rwkv_k_k[l], 'rwkv_k_a': rwkv_k_a[l], 'rwkv_r_k': rwkv_r_k[l],
              'rwkv_ln_g': rwkv_ln_g[l], 'rwkv_ln_b': rwkv_ln_b[l], 'ret_ln_g': ret_ln_g[l]}
        xp, st = _mixing_sublayer(xp, lp, 0, None, None, None,
                                  jnp.zeros((Bp, RWKV_HEADS, HEAD_DIM, HEAD_DIM), jnp.float32),
                                  jnp.zeros((Bp, RWKV_IN), xp.dtype),
                                  jnp.zeros((Bp, RET_HEADS, HEAD_DIM, HEAD_DIM), jnp.float32))
        outs_p.append(st)
        n_pool, page_rows = cache_nsa_sel.shape[1], cache_nsa_sel.shape[2]
        pages = (page_table + l * n_pool).reshape(-1)
        past_cmp = (cache_nsa_cmp.reshape(DEPTH * n_pool, page_rows, -1), pages)
        past_sel = cache_nsa_sel[l][page_table].reshape(Bs, past_len, 2, NSA_KV_HEADS, HEAD_DIM)
        xs, st = _mixing_sublayer(xs, lp, past_len, past_cmp, past_sel, cache_nsa_win[l], state_rwkv[l],
                                  state_rwkv_shift[l], state_ret[l])
        outs_s.append(st)
        i = l // 2
        if l % 2 == 0:
            xp = _dense_ffn(xp, norm_ffn[l], ffn_w_gate[i], ffn_w_up[i], ffn_w_down[i])
            xs = _dense_ffn(xs, norm_ffn[l], ffn_w_gate[i], ffn_w_up[i], ffn_w_down[i])
        else:
            xp, xs = _moe_ffn([xp, xs], norm_ffn[l], moe_router[i], moe_w_gate[i], moe_w_up[i], moe_w_down[i])
    kv_cmp_p, kv_sel_p, win_p, rwkv_p, shift_p, ret_p = [jnp.stack([o[j] for o in outs_p]) for j in range(6)]
    kv_cmp_s, kv_sel_s, win_s, rwkv_s, shift_s, ret_s = [jnp.stack([o[j] for o in outs_s]) for j in range(6)]
    return (xp, xs, kv_cmp_p, kv_sel_p, win_p, rwkv_p, shift_p, ret_p,
            kv_cmp_s, kv_sel_s, win_s, rwkv_s, shift_s, ret_s)
```

```python
import functools

import jax
import jax.numpy as jnp
from jax import lax
from jax.experimental import pallas as pl
from jax.experimental.pallas import tpu as pltpu

D_MODEL = 1024
DEPTH = 2
HEAD_DIM = 64
NSA_WIDTH = D_MODEL // 2
RWKV_WIDTH = D_MODEL // 4
RET_WIDTH = D_MODEL - NSA_WIDTH - RWKV_WIDTH
NSA_HEADS = NSA_WIDTH // HEAD_DIM
NSA_KV_HEADS = 2
NSA_GROUP = NSA_HEADS // NSA_KV_HEADS
CMP_LEN = 32
CMP_STRIDE = 16
SEL_LEN = 64
SEL_TOP = 16
WINDOW = 512
Q_BLOCK = 128
RWKV_HEADS = RWKV_WIDTH // HEAD_DIM
RWKV_W_RANK = 64
RWKV_A_RANK = 64
RWKV_G_RANK = 128
RWKV_GN_EPS = 64e-5
RET_HEADS = RET_WIDTH // HEAD_DIM
RET_CHUNK = 128
ROPE_BASE = 10000.0
N_EXPERTS = 8
TOP_K = 2
NSA_KV_COLS = NSA_KV_HEADS * HEAD_DIM
NSA_IN = NSA_WIDTH + 6 * NSA_KV_COLS + 3 * NSA_HEADS
RWKV_IN = 3 * RWKV_WIDTH + RWKV_W_RANK + RWKV_A_RANK + RWKV_G_RANK
RET_IN = 4 * RET_WIDTH
EPS = 1e-6
GN_EPS = 1e-5
NEG = -1e30
TINY = 1e-30
FORCE = 1e9

VMEM_LIMIT_BYTES = 56 * 1024 * 1024


def _pick_tile(n, target):
    t = min(n, target)
    while n % t:
        t //= 2
    return t


def _ffn_body(expert_ref, used_ref, x_ref, g_ref, s_ref, wg_ref, wu_ref, wd_ref, y_ref, h_scr, acc_scr):
    i = pl.program_id(0)
    j = pl.program_id(1)
    last = pl.num_programs(1) - 1
    used = used_ref[i] > 0

    @pl.when(used & (j == 0))
    def _():
        x = x_ref[...]
        h = x * lax.rsqrt(jnp.mean(x * x, axis=-1, keepdims=True) + EPS) * g_ref[...]
        h_scr[...] = h.astype(jnp.bfloat16)
        acc_scr[...] = jnp.zeros_like(acc_scr)

    @pl.when(used)
    def _():
        h = h_scr[...]
        a = jnp.dot(h, wg_ref[...].astype(jnp.bfloat16), preferred_element_type=jnp.float32)
        b = jnp.dot(h, wu_ref[...].astype(jnp.bfloat16), preferred_element_type=jnp.float32)
        z = (a * jax.nn.sigmoid(a)) * b
        acc_scr[...] += jnp.dot(z.astype(jnp.bfloat16), wd_ref[...].astype(jnp.bfloat16),
                                preferred_element_type=jnp.float32)

    @pl.when(used & (j == last))
    def _():
        y_ref[...] = acc_scr[...] * s_ref[...]

    @pl.when(jnp.logical_not(used) & (j == last))
    def _():
        y_ref[...] = jnp.zeros_like(y_ref)


def _ffn(x, g, scale, wg, wu, wd, tile_expert, tile_used, *, tm, tf=512):
    M, D = x.shape
    F = wg.shape[2]
    tf = _pick_tile(F, tf)
    grid_spec = pltpu.PrefetchScalarGridSpec(
        num_scalar_prefetch=2,
        grid=(M // tm, F // tf),
        in_specs=[
            pl.BlockSpec((tm, D), lambda i, j, e, u: (i, 0)),
            pl.BlockSpec((1, D), lambda i, j, e, u: (0, 0)),
            pl.BlockSpec((tm, 1), lambda i, j, e, u: (i, 0)),
            pl.BlockSpec((None, D, tf), lambda i, j, e, u: (e[i], 0, j)),
            pl.BlockSpec((None, D, tf), lambda i, j, e, u: (e[i], 0, j)),
            pl.BlockSpec((None, tf, D), lambda i, j, e, u: (e[i], j, 0)),
        ],
        out_specs=pl.BlockSpec((tm, D), lambda i, j, e, u: (i, 0)),
        scratch_shapes=[pltpu.VMEM((tm, D), jnp.bfloat16), pltpu.VMEM((tm, D), jnp.float32)],
    )
    return pl.pallas_call(
        _ffn_body,
        grid_spec=grid_spec,
        out_shape=jax.ShapeDtypeStruct((M, D), jnp.float32),
        compiler_params=pltpu.CompilerParams(
            dimension_semantics=("parallel", "arbitrary"), vmem_limit_bytes=VMEM_LIMIT_BYTES),
        name="ffn",
    )(tile_expert, tile_used, x, g.reshape(1, D), scale, wg, wu, wd)


def _router_body(x_ref, g_ref, w_ref, o_ref):
    x = x_ref[...]
    h = x * lax.rsqrt(jnp.mean(x * x, axis=-1, keepdims=True) + EPS) * g_ref[...]
    o_ref[...] = jnp.dot(h, w_ref[...], preferred_element_type=jnp.float32, precision=lax.Precision.HIGHEST)


def _router_logits(x, g, router, *, tm=512):
    M, D = x.shape
    E = router.shape[1]
    tm = _pick_tile(M, tm)
    lanes = 128
    w = jnp.pad(router, ((0, 0), (0, lanes - E)))
    out = pl.pallas_call(
        _router_body,
        grid=(M // tm,),
        in_specs=[pl.BlockSpec((tm, D), lambda i: (i, 0)), pl.BlockSpec((1, D), lambda i: (0, 0)),
                  pl.BlockSpec((D, lanes), lambda i: (0, 0))],
        out_specs=pl.BlockSpec((tm, lanes), lambda i: (i, 0)),
        out_shape=jax.ShapeDtypeStruct((M, lanes), jnp.float32),
        compiler_params=pltpu.CompilerParams(dimension_semantics=("parallel",), vmem_limit_bytes=VMEM_LIMIT_BYTES),
        name="router",
    )(x, g.reshape(1, D), w)
    return out[:, :E]


def _rwkv_scan_body(r_ref, w_ref, k_ref, v_ref, kk_ref, b_ref, s0_ref, y_ref, st_ref, s_scr, *, sub):
    j = pl.program_id(1)
    n_b, t_blk, _ = r_ref.shape
    d = HEAD_DIM

    @pl.when(j == 0)
    def _():
        s_scr[...] = s0_ref[...]

    eye = lax.broadcasted_iota(jnp.int32, (d, d), 0) == lax.broadcasted_iota(jnp.int32, (d, d), 1)

    def sub_block(i, carry):
        t0 = pl.multiple_of(i * sub, sub)
        for bb in range(n_b):
            blk = [ref[bb, pl.ds(t0, sub), :] for ref in (r_ref, w_ref, k_ref, v_ref, kk_ref, b_ref)]
            for h in range(RWKV_HEADS):
                cols = slice(h * d, (h + 1) * d)
                S = s_scr[bb, h]
                y_rows = []
                for t in range(sub):
                    r_t, w_t, k_t, v_t, kk_t, b_t = [x[t:t + 1, cols] for x in blk]
                    sa = jnp.sum(S * kk_t, axis=1, keepdims=True)
                    v_col = jnp.sum(jnp.where(eye, v_t, 0.0), axis=1, keepdims=True)
                    S = S * w_t - sa * b_t + v_col * k_t
                    y_col = jnp.sum(S * r_t, axis=1, keepdims=True)
                    y_rows.append(jnp.sum(jnp.where(eye, y_col, 0.0), axis=0, keepdims=True))
                s_scr[bb, h] = S
                y_ref[bb, pl.ds(t0, sub), cols] = jnp.concatenate(y_rows, axis=0)
        return carry

    lax.fori_loop(0, t_blk // sub, sub_block, 0)

    @pl.when(j == pl.num_programs(1) - 1)
    def _():
        st_ref[...] = s_scr[...]


def _rwkv_scan(r, w, k, v, kk, b, s0, *, n_b=2, t_blk=256):
    B, T, W = r.shape
    n_b = _pick_tile(B, n_b)
    t_blk = _pick_tile(T, t_blk)
    sub = 8 if t_blk % 8 == 0 else t_blk
    seq = pl.BlockSpec((n_b, t_blk, W), lambda i, j: (i, j, 0))
    st = pl.BlockSpec((n_b, RWKV_HEADS, HEAD_DIM, HEAD_DIM), lambda i, j: (i, 0, 0, 0))
    return pl.pallas_call(
        functools.partial(_rwkv_scan_body, sub=sub),
        grid=(B // n_b, T // t_blk),
        in_specs=[seq] * 6 + [st],
        out_specs=[seq, st],
        out_shape=[jax.ShapeDtypeStruct((B, T, W), jnp.float32),
                   jax.ShapeDtypeStruct((B, RWKV_HEADS, HEAD_DIM, HEAD_DIM), jnp.float32)],
        scratch_shapes=[pltpu.VMEM((n_b, RWKV_HEADS, HEAD_DIM, HEAD_DIM), jnp.float32)],
        compiler_params=pltpu.CompilerParams(
            dimension_semantics=("parallel", "arbitrary"), vmem_limit_bytes=VMEM_LIMIT_BYTES),
        name="rwkv_scan",
    )(r, w, k, v, kk, b, s0)


RWKV_CHUNK = 64
RWKV_CHUNKS_PER_STEP = 4


_F32_DOT = dict(preferred_element_type=jnp.float32, precision=lax.Precision.HIGHEST)


def _dot(a, b):
    return jnp.dot(a, b, **_F32_DOT)


def _dot_t(a, b):
    return lax.dot_general(a, b, (((1,), (1,)), ((), ())), **_F32_DOT)


def _dot_0(a, b):
    return lax.dot_general(a, b, (((0,), (0,)), ((), ())), **_F32_DOT)


_NN = (((1,), (0,)), ((), ()))
_NT = (((1,), (1,)), ((), ()))


def _split(x):
    hi = x.astype(jnp.bfloat16)
    return hi, (x - hi.astype(jnp.float32)).astype(jnp.bfloat16)


def _mm3(a, b, dims):
    dot = lambda x, y: lax.dot_general(x, y, dims, preferred_element_type=jnp.float32)
    return dot(a[0], b[0]) + (dot(a[0], b[1]) + dot(a[1], b[0]))


def _rwkv_chunk_body(r_ref, lw_ref, k_ref, v_ref, kk_ref, b_ref, qe_ref, y0_ref, pm_ref, z_ref):
    L, d = RWKV_CHUNK, HEAD_DIM
    n = r_ref.shape[1]
    row = lax.broadcasted_iota(jnp.int32, (n, n), 0)
    col = lax.broadcasted_iota(jnp.int32, (n, n), 1)
    same = (row // L) == (col // L)
    lower = same & (row >= col)
    strict = same & (row > col)
    ones_lower = jnp.where(lower, 1.0, 0.0)
    eye = jnp.where(row == col, 1.0, 0.0)
    eye_d = eye[:d, :d]
    ones_bf = ones_lower.astype(jnp.bfloat16)
    for h in range(RWKV_HEADS):
        cols = slice(h * d, (h + 1) * d)
        r, lw, k, v, kk, b = [ref[0, :, cols] for ref in (r_ref, lw_ref, k_ref, v_ref, kk_ref, b_ref)]
        lw_hi, lw_lo = _split(lw)
        lw_rest = (lw - lw_hi.astype(jnp.float32) - lw_lo.astype(jnp.float32)).astype(jnp.bfloat16)
        G = sum(jnp.dot(ones_bf, t, preferred_element_type=jnp.float32) for t in (lw_hi, lw_lo, lw_rest))
        g_inv = jnp.exp(-G)
        kap, bt, kt, rt = kk * jnp.exp(G - lw), b * g_inv, k * g_inv, r * jnp.exp(G)
        kap2, bt2, kt2, rt2, v2 = _split(kap), _split(bt), _split(kt), _split(rt), _split(v)
        N = jnp.where(strict, _mm3(kap2, bt2, _NT), 0.0)
        Mk = jnp.where(strict, _mm3(kap2, kt2, _NT), 0.0)
        RB = jnp.where(lower, _mm3(rt2, bt2, _NT), 0.0)
        RK = jnp.where(lower, _mm3(rt2, kt2, _NT), 0.0)
        X, P2 = eye - N, _split(N)
        for _ in range(L.bit_length() - 2):
            P2 = _split(_mm3(P2, P2, _NN))
            X = X + _mm3(_split(X), P2, _NN)
        X2, RB2 = _split(X), _split(RB)
        A = _mm3(X2, kap2, _NN)
        C = _mm3(X2, _split(_mm3(_split(Mk), v2, _NN)), _NN)
        qe_ref[0, h] = rt - _mm3(RB2, _split(A), _NN)
        y0_ref[0, h] = _mm3(_split(RK), v2, _NN) - _mm3(RB2, _split(C), _NN)
        for c in range(n // L):
            rows = slice(c * L, (c + 1) * L)
            g_end = jnp.exp(G[(c + 1) * L - 1:(c + 1) * L, :])
            pm_ref[0, h, c] = (eye_d - _dot_0(A[rows], bt[rows])) * g_end
            z_ref[0, h, c] = (_dot_0(v[rows], kt[rows]) - _dot_0(C[rows], bt[rows])) * g_end


def _rwkv_walk_body(qe_ref, y0_ref, pm_ref, z_ref, s0_ref, y_ref, st_ref, s_scr):
    j = pl.program_id(0)
    B, H, n_c = pm_ref.shape[:3]
    L = RWKV_CHUNK

    @pl.when(j == 0)
    def _():
        s_scr[...] = s0_ref[...]

    def one_chunk(c, carry):
        t0 = pl.multiple_of(c * L, L)
        for bb in range(B):
            for h in range(H):
                S = s_scr[bb, h]
                y_ref[bb, h, pl.ds(t0, L), :] = _dot_t(qe_ref[bb, h, pl.ds(t0, L), :], S) + y0_ref[bb, h, pl.ds(t0, L), :]
                s_scr[bb, h] = _dot(S, pm_ref[bb, h, c]) + z_ref[bb, h, c]
        return carry

    lax.fori_loop(0, n_c, one_chunk, 0)

    @pl.when(j == pl.num_programs(0) - 1)
    def _():
        st_ref[...] = s_scr[...]


def _rwkv_chunked(r, lw, k, v, kk, b, s0, *, chunks_per_step=16):
    B, T, W = r.shape
    H, d, L = RWKV_HEADS, HEAD_DIM, RWKV_CHUNK
    n_c = T // L
    group = _pick_tile(n_c, RWKV_CHUNKS_PER_STEP)
    seq = pl.BlockSpec((1, group * L, W), lambda i, c: (i, c, 0))
    per_tok = pl.BlockSpec((1, H, group * L, d), lambda i, c: (i, 0, c, 0))
    per_chunk = pl.BlockSpec((1, H, group, d, d), lambda i, c: (i, 0, c, 0, 0))
    qe, y0, pm, z = pl.pallas_call(
        _rwkv_chunk_body,
        grid=(B, n_c // group),
        in_specs=[seq] * 6,
        out_specs=[per_tok, per_tok, per_chunk, per_chunk],
        out_shape=[jax.ShapeDtypeStruct((B, H, T, d), jnp.float32)] * 2
                  + [jax.ShapeDtypeStruct((B, H, n_c, d, d), jnp.float32)] * 2,
        compiler_params=pltpu.CompilerParams(
            dimension_semantics=("parallel", "parallel"), vmem_limit_bytes=VMEM_LIMIT_BYTES),
        name="rwkv_chunk",
    )(r, lw, k, v, kk, b)
    cs = _pick_tile(n_c, chunks_per_step)
    tok = pl.BlockSpec((B, H, cs * L, d), lambda j: (0, 0, j, 0))
    chk = pl.BlockSpec((B, H, cs, d, d), lambda j: (0, 0, j, 0, 0))
    st = pl.BlockSpec((B, H, d, d), lambda j: (0, 0, 0, 0))
    y, s_t = pl.pallas_call(
        _rwkv_walk_body,
        grid=(n_c // cs,),
        in_specs=[tok, tok, chk, chk, st],
        out_specs=[tok, st],
        out_shape=[jax.ShapeDtypeStruct((B, H, T, d), jnp.float32), jax.ShapeDtypeStruct((B, H, d, d), jnp.float32)],
        scratch_shapes=[pltpu.VMEM((B, H, d, d), jnp.float32)],
        compiler_params=pltpu.CompilerParams(
            dimension_semantics=("arbitrary",), vmem_limit_bytes=VMEM_LIMIT_BYTES),
        name="rwkv_walk",
    )(qe, y0, pm, z, s0)
    return jnp.transpose(y, (0, 2, 1, 3)).reshape(B, T, W), s_t


SEL_CHUNK = 512
WIN_SPAN = WINDOW + Q_BLOCK


def _nsa_scores(k_aug, qT_ref):
    return [jnp.dot(k_aug, qT_ref[0, 0, r], preferred_element_type=jnp.float32) for r in range(NSA_GROUP)]


def _nsa_softmax_cols(s_list, neg):
    out = []
    for s in s_list:
        s = s + neg
        m = jnp.max(s, axis=0, keepdims=True)
        e = jnp.exp(s - m)
        inv = jnp.where(m > 0.5 * NEG, 1.0, 0.0) / jnp.maximum(jnp.sum(e, axis=0, keepdims=True), TINY)
        out.append(e * inv)
    return out


def _nsa_prompt_body(qT_ref, kc_ref, vcT_ref, ks_ref, vsT_ref, kw_ref, vwT_ref, covT_ref, o_ref,
                     sel_scr, m_scr, l_scr, acc_scr):
    i = pl.program_id(2)
    R = NSA_GROUP
    n_cmp = kc_ref.shape[2]
    n_sel = covT_ref.shape[0]
    q0 = i * Q_BLOCK
    pos = q0 + lax.broadcasted_iota(jnp.int32, (1, Q_BLOCK), 1)
    posf = pos.astype(jnp.float32)
    bf = jnp.bfloat16

    c_end = (lax.broadcasted_iota(jnp.int32, (n_cmp, Q_BLOCK), 0) * CMP_STRIDE + (CMP_LEN - 1)).astype(jnp.float32)
    neg_c = jnp.where(posf - c_end >= 0.0, 0.0, NEG)
    p_c = _nsa_softmax_cols(_nsa_scores(kc_ref[0, 0], qT_ref), neg_c)
    vcT = vcT_ref[0, 0]
    o_c = [jnp.dot(vcT, p.astype(bf), preferred_element_type=jnp.float32) for p in p_c]
    for r in range(R):
        o_ref[0, 0, 0, r] = o_c[r]
    p_sum = (p_c[0] + p_c[1]) + (p_c[2] + p_c[3])
    imp = jnp.dot(covT_ref[...], p_sum, preferred_element_type=jnp.float32, precision=lax.Precision.HIGHEST)
    blk = lax.broadcasted_iota(jnp.int32, (n_sel, Q_BLOCK), 0)
    avail = blk * SEL_LEN <= pos
    forced = (blk == jnp.right_shift(pos, 6)) | (blk == 0)
    imp = jnp.where(avail, jnp.where(forced, FORCE, imp), -FORCE)

    sub8 = lax.broadcasted_iota(jnp.int32, (8, Q_BLOCK), 0)
    for jj in range(n_sel // 8):
        vj = imp[8 * jj:8 * jj + 8]
        cnt = jnp.zeros((8, Q_BLOCK), jnp.float32)
        for k in range(n_sel):
            row = imp[k:k + 1]
            if k < 8 * jj:
                cnt = cnt + jnp.where(row >= vj, 1.0, 0.0)
            elif k >= 8 * jj + 8:
                cnt = cnt + jnp.where(row > vj, 1.0, 0.0)
            else:
                cnt = cnt + jnp.where(sub8 > (k - 8 * jj), jnp.where(row >= vj, 1.0, 0.0),
                                      jnp.where(row > vj, 1.0, 0.0))
        sel_scr[8 * jj:8 * jj + 8, :] = jnp.where(cnt < float(SEL_TOP), 1.0, 0.0)

    m_scr[...] = jnp.full_like(m_scr, NEG)
    l_scr[...] = jnp.zeros_like(l_scr)
    acc_scr[...] = jnp.zeros_like(acc_scr)
    key_iota = lax.broadcasted_iota(jnp.int32, (SEL_CHUNK, Q_BLOCK), 0).astype(jnp.float32)
    blocks_per_chunk = SEL_CHUNK // SEL_LEN

    def chunk(c, carry):
        k0 = pl.multiple_of(c * SEL_CHUNK, SEL_CHUNK)
        s_all = _nsa_scores(ks_ref[0, 0, pl.ds(k0, SEL_CHUNK), :], qT_ref)
        dist = (posf - k0.astype(jnp.float32)) - key_iota
        picked = jnp.concatenate(
            [jnp.broadcast_to(sel_scr[pl.ds(c * blocks_per_chunk + b, 1), :], (SEL_LEN, Q_BLOCK))
             for b in range(blocks_per_chunk)], axis=0)
        neg = jnp.where((picked > 0.5) & (dist >= 0.0), 0.0, NEG)
        m_old = [m_scr[r] for r in range(R)]
        m_new, p_all, l_add = [], [], []
        for r in range(R):
            s = s_all[r] + neg
            m = jnp.maximum(m_old[r], jnp.max(s, axis=0, keepdims=True))
            p = jnp.exp(s - m)
            m_new.append(m)
            l_add.append(jnp.sum(p, axis=0, keepdims=True))
            p_all.append(p.astype(bf))
        vsT = vsT_ref[0, 0, :, pl.ds(k0, SEL_CHUNK)]
        pv = [jnp.dot(vsT, p, preferred_element_type=jnp.float32) for p in p_all]
        for r in range(R):
            alpha = jnp.exp(m_old[r] - m_new[r])
            l_scr[r] = alpha * l_scr[r] + l_add[r]
            acc_scr[r] = alpha * acc_scr[r] + pv[r]
            m_scr[r] = m_new[r]
        return carry

    lax.fori_loop(0, q0 // SEL_CHUNK + 1, chunk, 0)
    for r in range(R):
        o_ref[1, 0, 0, r] = acc_scr[r] / jnp.maximum(l_scr[r], TINY)

    k0w = pl.multiple_of(jnp.maximum(q0 - WINDOW, 0), Q_BLOCK)
    d_win = (posf - k0w.astype(jnp.float32)) - lax.broadcasted_iota(jnp.int32, (WIN_SPAN, Q_BLOCK), 0).astype(jnp.float32)
    neg_w = jnp.where((d_win >= 0.0) & (d_win < float(WINDOW)), 0.0, NEG)
    p_w = _nsa_softmax_cols(_nsa_scores(kw_ref[0, 0, pl.ds(k0w, WIN_SPAN), :], qT_ref), neg_w)
    vwT = vwT_ref[0, 0, :, pl.ds(k0w, WIN_SPAN)]
    o_w = [jnp.dot(vwT, p.astype(bf), preferred_element_type=jnp.float32) for p in p_w]
    for r in range(R):
        o_ref[2, 0, 0, r] = o_w[r]


def _nsa_prompt_attn(q, kc, vc, ks, vs, kw, vw):
    B, T, H, d = q.shape
    G, R = NSA_KV_HEADS, NSA_GROUP
    n_c = kc.shape[1]
    n_cmp = -(-n_c // 128) * 128
    n_sel = T // SEL_LEN
    bf = jnp.bfloat16
    lanes = 128

    def key_rows(k, key_pos):
        n = k.shape[1]
        extra = jnp.zeros((n, lanes - d), jnp.float32).at[:, 0].set((key_pos // SEL_LEN).astype(jnp.float32))
        extra = extra.at[:, 1].set((key_pos % SEL_LEN).astype(jnp.float32))
        extra = jnp.broadcast_to(extra[None, None], (B, G, n, lanes - d))
        return jnp.concatenate([jnp.transpose(k, (0, 2, 1, 3)), extra], axis=-1).astype(bf)

    cols = lambda t: jnp.transpose(t, (0, 2, 3, 1)).astype(bf)
    slopes = (2.0 ** -jnp.arange(1, H + 1, dtype=jnp.float32)).reshape(G, R)
    q_extra = jnp.zeros((G, R, lanes - d), jnp.float32).at[:, :, 0].set(SEL_LEN * slopes).at[:, :, 1].set(slopes)
    qT = jnp.transpose((q * (d ** -0.5)).reshape(B, T, G, R, d), (0, 2, 3, 4, 1))
    qT = jnp.concatenate([qT, jnp.broadcast_to(q_extra[None, :, :, :, None], (B, G, R, lanes - d, T))],
                         axis=3).astype(bf)
    pad_c = ((0, 0), (0, n_cmp - n_c), (0, 0), (0, 0))
    kc_r = key_rows(jnp.pad(kc, pad_c), jnp.arange(n_cmp, dtype=jnp.int32) * CMP_STRIDE + (CMP_LEN - 1))
    vc_c = cols(jnp.pad(vc, pad_c))
    tok = jnp.arange(T, dtype=jnp.int32)
    rows = lambda t: key_rows(t, tok)
    c_start = jnp.arange(n_cmp, dtype=jnp.int32) * CMP_STRIDE
    s_start = jnp.arange(n_sel, dtype=jnp.int32) * SEL_LEN
    covT = jnp.maximum(jnp.minimum(c_start[None, :] + CMP_LEN, s_start[:, None] + SEL_LEN)
                       - jnp.maximum(c_start[None, :], s_start[:, None]), 0).astype(jnp.float32) / CMP_LEN
    full_r = lambda n: pl.BlockSpec((1, 1, n, lanes), lambda b, g, i: (b, g, 0, 0))
    full_c = lambda n: pl.BlockSpec((1, 1, d, n), lambda b, g, i: (b, g, 0, 0))
    oT = pl.pallas_call(
        _nsa_prompt_body,
        grid=(B, G, T // Q_BLOCK),
        in_specs=[pl.BlockSpec((1, 1, R, lanes, Q_BLOCK), lambda b, g, i: (b, g, 0, 0, i)),
                  full_r(n_cmp), full_c(n_cmp), full_r(T), full_c(T), full_r(T), full_c(T),
                  pl.BlockSpec((n_sel, n_cmp), lambda b, g, i: (0, 0))],
        out_specs=pl.BlockSpec((3, 1, 1, R, d, Q_BLOCK), lambda b, g, i: (0, b, g, 0, 0, i)),
        out_shape=jax.ShapeDtypeStruct((3, B, G, R, d, T), jnp.float32),
        scratch_shapes=[pltpu.VMEM((n_sel, Q_BLOCK), jnp.float32),
                        pltpu.VMEM((R, 1, Q_BLOCK), jnp.float32),
                        pltpu.VMEM((R, 1, Q_BLOCK), jnp.float32),
                        pltpu.VMEM((R, d, Q_BLOCK), jnp.float32)],
        compiler_params=pltpu.CompilerParams(
            dimension_semantics=("parallel", "parallel", "arbitrary"), vmem_limit_bytes=VMEM_LIMIT_BYTES),
        name="nsa_prompt",
    )(qT, kc_r, vc_c, rows(ks), cols(vs), rows(kw), cols(vw), covT)
    return jnp.transpose(oT, (0, 1, 5, 2, 3, 4)).reshape(3, B, T, H, d)


PAGES_PER_STEP = 8
def _compress_pages_body(pt_ref, *rest):
    page_refs = rest[:PAGES_PER_STEP]
    w1_ref, c0_ref, w2_ref, ones_ref, gain_ref, o_ref, x_scr = rest[PAGES_PER_STEP:]
    j = pl.program_id(1)
    rows = page_refs[0].shape[0]
    for i, ref in enumerate(page_refs):
        x_scr[pl.ds(pl.multiple_of((j * PAGES_PER_STEP + i) * rows, rows), rows), :] = ref[...]

    @pl.when(j == pl.num_programs(1) - 1)
    def _():
        n, half = x_scr.shape[0], w2_ref.shape[0]
        ab = jnp.dot(x_scr[...].astype(jnp.bfloat16), w1_ref[...], preferred_element_type=jnp.float32)
        top, bottom = ab[:, :half], ab[:, half:]
        nxt = jnp.concatenate([bottom[1:], jnp.zeros((1, half), jnp.float32)], axis=0)
        hid = jax.nn.gelu(top + nxt + c0_ref[...])
        y = jnp.dot(hid.astype(jnp.bfloat16), w2_ref[...], preferred_element_type=jnp.float32)
        mean_sq = jnp.dot(y * y, ones_ref[...], preferred_element_type=jnp.float32, precision=lax.Precision.HIGHEST)
        is_key = lax.broadcasted_iota(jnp.int32, (n, half), 1) < half // 2
        o_ref[0] = jnp.where(is_key, y * lax.rsqrt(mean_sq + EPS) * gain_ref[...], y)


def _compress_pages(pool, pages, n_batch, pe, w1, w2, k_gain):
    d, G = HEAD_DIM, NSA_KV_HEADS
    N, page_rows, W = pool.shape
    cpp = page_rows // CMP_STRIDE
    n_pages = pages.shape[0] // n_batch
    halves = CMP_LEN // CMP_STRIDE
    w1r = w1.reshape(2, halves, CMP_STRIDE, d, d)[jnp.array([0] * G + [1] * G)]
    eye = jnp.eye(2 * G, dtype=jnp.float32)
    w_big = jnp.transpose(w1r, (2, 0, 3, 1, 4))[:, :, :, :, None, :] * eye[None, :, None, None, :, None]
    w_big = w_big.reshape(CMP_STRIDE * W, halves * W).astype(jnp.bfloat16)
    c0 = jnp.einsum('kn,kne->ke', pe.reshape(2, CMP_LEN * d), w1, precision=lax.Precision.HIGHEST)
    c0 = jnp.repeat(c0, G, axis=0).reshape(1, W)
    w2_big = (w2[jnp.array([0] * G + [1] * G)][:, :, None, :] * eye[:, None, :, None]).reshape(W, W).astype(jnp.bfloat16)
    ones_blk = jnp.kron(eye, jnp.full((d, d), 1.0 / d, jnp.float32))
    gain = jnp.concatenate([jnp.tile(k_gain, G), jnp.ones((G * d,), jnp.float32)]).reshape(1, W)
    pool_chunks = pool.reshape(N, cpp, CMP_STRIDE * W)
    page_spec = lambda k: pl.BlockSpec((None, cpp, CMP_STRIDE * W),
                                       lambda b, j, pt: (pt[b * n_pages + j * PAGES_PER_STEP + k], 0, 0))
    const = lambda shape: pl.BlockSpec(shape, lambda b, j, pt: (0, 0))
    grid_spec = pltpu.PrefetchScalarGridSpec(
        num_scalar_prefetch=1,
        grid=(n_batch, n_pages // PAGES_PER_STEP),
        in_specs=[page_spec(k) for k in range(PAGES_PER_STEP)]
                 + [const(w_big.shape), const((1, W)), const((W, W)), const((W, W)), const((1, W))],
        out_specs=pl.BlockSpec((1, n_pages * cpp, W), lambda b, j, pt: (b, 0, 0)),
        scratch_shapes=[pltpu.VMEM((n_pages * cpp, CMP_STRIDE * W), jnp.float32)],
    )
    return pl.pallas_call(
        _compress_pages_body,
        grid_spec=grid_spec,
        out_shape=jax.ShapeDtypeStruct((n_batch, n_pages * cpp, W), jnp.float32),
        compiler_params=pltpu.CompilerParams(
            dimension_semantics=("parallel", "arbitrary"), vmem_limit_bytes=VMEM_LIMIT_BYTES),
        name="compress_pages",
    )(pages, *([pool_chunks] * PAGES_PER_STEP), w_big, c0, w2_big, ones_blk, gain)


def _rms_norm(x, g):
    xf = x.astype(jnp.float32)
    y = xf * lax.rsqrt(jnp.mean(xf * xf, axis=-1, keepdims=True) + EPS)
    return (y * g.astype(jnp.float32)).astype(x.dtype)


def _group_norm(x, g, eps):
    xf = x.astype(jnp.float32)
    mu = jnp.mean(xf, axis=-1, keepdims=True)
    var = jnp.mean(jnp.square(xf - mu), axis=-1, keepdims=True)
    return (xf - mu) * lax.rsqrt(var + eps) * g.astype(jnp.float32)


def _masked_softmax(s, mask):
    s = jnp.where(mask, s, NEG)
    m = jnp.max(s, axis=-1, keepdims=True)
    e = jnp.where(mask, jnp.exp(s - m), 0.0)
    return e / jnp.maximum(jnp.sum(e, axis=-1, keepdims=True), TINY)


def _alibi_slopes(n):
    return 2.0 ** (-8.0 * jnp.arange(1, n + 1, dtype=jnp.float32) / n)


def _rotary(x, pos):
    half = x.shape[-1] // 2
    freqs = ROPE_BASE ** (-jnp.arange(half, dtype=jnp.float32) / half)
    ang = pos.astype(jnp.float32)[:, None] * freqs[None, :]
    cos, sin = jnp.cos(ang)[None, :, None, :], jnp.sin(ang)[None, :, None, :]
    x1, x2 = x[..., :half], x[..., half:]
    return jnp.concatenate([x1 * cos - x2 * sin, x1 * sin + x2 * cos], axis=-1)


def _nsa_compress(rows, pe, w1, w2):
    B, T, G, d = rows.shape
    n_cmp = (T - CMP_LEN) // CMP_STRIDE + 1
    idx = (jnp.arange(n_cmp, dtype=jnp.int32) * CMP_STRIDE)[:, None] + jnp.arange(CMP_LEN, dtype=jnp.int32)[None, :]
    blk = rows[:, idx] + pe[None, None, :, None, :]
    blk = jnp.transpose(blk, (0, 1, 3, 2, 4)).reshape(B, n_cmp, G, CMP_LEN * d)
    return jax.nn.gelu(blk @ w1) @ w2


def _to_sel_blocks(rows):
    B, T, G, d = rows.shape
    n_sel = -(-T // SEL_LEN)
    rows = jnp.pad(rows, ((0, 0), (0, n_sel * SEL_LEN - T), (0, 0), (0, 0)))
    return jnp.transpose(rows.reshape(B, n_sel, SEL_LEN, G, d), (0, 3, 1, 2, 4))


def _nsa_branches(q, pos, kc, vc, n_sel, sel_branch, kw, vw, pos_w, slopes):
    B, Tq, H, d = q.shape
    G, R = NSA_KV_HEADS, NSA_GROUP
    scale = d ** -0.5
    qg = q.reshape(B, Tq, G, R, d)
    sl = slopes.reshape(G, R)
    posf = pos.astype(jnp.float32)
    n_cmp = kc.shape[1]
    c_start = jnp.arange(n_cmp, dtype=jnp.int32) * CMP_STRIDE
    d_cmp = posf[:, None] - (c_start + CMP_LEN - 1).astype(jnp.float32)[None, :]
    s = jnp.einsum('btgrd,bngd->bgrtn', qg, kc).astype(jnp.float32) * scale - sl[None, :, :, None, None] * d_cmp
    p_cmp = _masked_softmax(s, (d_cmp >= 0.0)[None, None, None])
    o_cmp = jnp.einsum('bgrtn,bngd->btgrd', p_cmp.astype(vc.dtype), vc)
    s_start = jnp.arange(n_sel, dtype=jnp.int32) * SEL_LEN
    cover = jnp.maximum(jnp.minimum(c_start[:, None] + CMP_LEN, s_start[None, :] + SEL_LEN)
                        - jnp.maximum(c_start[:, None], s_start[None, :]), 0).astype(jnp.float32) / CMP_LEN
    imp = jnp.einsum('bgrtn,nj->bgtj', p_cmp, cover)
    blk = jnp.arange(n_sel, dtype=jnp.int32)
    avail = s_start[None, :] <= pos[:, None]
    forced = (blk[None, :] == (pos // SEL_LEN)[:, None]) | (blk[None, :] == 0)
    imp = jnp.where(avail, jnp.where(forced, FORCE, imp), -FORCE)
    n_top = min(SEL_TOP, n_sel)
    _, idx = lax.top_k(imp, n_top)
    o_sel = sel_branch(qg, idx)
    d_win = posf[:, None] - pos_w.astype(jnp.float32)[None, :]
    s = jnp.einsum('btgrd,bwgd->bgrtw', qg, kw).astype(jnp.float32) * scale - sl[None, :, :, None, None] * d_win
    win_mask = (d_win >= 0.0) & (d_win < WINDOW) & (pos_w >= 0)[None, :]
    p_win = _masked_softmax(s, win_mask[None, None, None])
    o_win = jnp.einsum('bgrtw,bwgd->btgrd', p_win.astype(vw.dtype), vw)
    return jnp.stack([o_cmp, o_sel, o_win]).reshape(3, B, Tq, H, d)


def _rwkv_group(u, shift0, S0, lp):
    B, T, _ = u.shape
    W = RWKV_WIDTH
    uf = u.astype(jnp.float32)
    prev = jnp.concatenate([shift0.astype(jnp.float32)[:, None], uf[:, :-1]], axis=1)
    um = uf + (prev - uf) * lp['rwkv_mu']
    r, k, v = um[..., :W], um[..., W:2 * W], um[..., 2 * W:3 * W]
    o = 3 * W
    wd = um[..., o:o + RWKV_W_RANK]
    ad = um[..., o + RWKV_W_RANK:o + RWKV_W_RANK + RWKV_A_RANK]
    gd = um[..., o + RWKV_W_RANK + RWKV_A_RANK:]
    w = lp['rwkv_w0'] + jnp.tanh(wd) @ lp['rwkv_w_up']
    log_decay = -jnp.exp(-jax.nn.softplus(-w) - 0.5)
    a = jax.nn.sigmoid(lp['rwkv_a0'] + ad @ lp['rwkv_a_up'])
    g = jax.nn.sigmoid(gd) @ lp['rwkv_g_up']
    kk = k * lp['rwkv_k_k']
    k = k * (1.0 + (a - 1.0) * lp['rwkv_k_a'])
    hd = lambda t: t.reshape(B, T, RWKV_HEADS, HEAD_DIM).astype(jnp.float32)
    r, k, v, a, g, kk = hd(r), hd(k), hd(v), hd(a), hd(g), hd(kk)
    kk = kk / jnp.maximum(jnp.sqrt(jnp.sum(kk * kk, axis=-1, keepdims=True)), 1e-12)
    flat = lambda t: t.reshape(B, T, W)
    if T % RWKV_CHUNK == 0:
        ys, S_T = _rwkv_chunked(flat(r), log_decay, flat(k), flat(v), flat(kk), flat(kk * a), S0.astype(jnp.float32))
    else:
        ys, S_T = _rwkv_scan(flat(r), jnp.exp(log_decay), flat(k), flat(v), flat(kk), flat(kk * a),
                             S0.astype(jnp.float32))
    y = _group_norm(ys.reshape(B, T, RWKV_HEADS, HEAD_DIM), lp['rwkv_ln_g'], RWKV_GN_EPS) + lp['rwkv_ln_b']
    y = y + jnp.sum(r * k * lp['rwkv_r_k'], axis=-1, keepdims=True) * v
    y = y * g
    return y.reshape(B, T, W).astype(u.dtype), u[:, -1], S_T


def _retention_group(u, pos, S0, ln_g):
    B, T, _ = u.shape
    uf = u.astype(jnp.float32)
    q, k, v, g = [t.reshape(B, T, RET_HEADS, HEAD_DIM) for t in jnp.split(uf, 4, axis=-1)]
    q = _rotary(q, pos)
    k = _rotary(k, pos) * HEAD_DIM ** -0.5
    lg = jnp.log(1.0 - 2.0 ** (-5.0 - jnp.arange(RET_HEADS, dtype=jnp.float32)))
    C = RET_CHUNK if T % RET_CHUNK == 0 else T
    nC = T // C
    n = jnp.arange(C, dtype=jnp.float32)
    diff = n[:, None] - n[None, :]
    dmask = jnp.where(diff[None] >= 0, jnp.exp(jnp.maximum(diff, 0.0)[None] * lg[:, None, None]), 0.0)
    q_dec = jnp.exp((n[:, None] + 1.0) * lg[None, :])
    k_dec = jnp.exp((C - 1.0 - n)[:, None] * lg[None, :])
    s_dec = jnp.exp(C * lg)

    def chunk(S, inp):
        qc, kc, vc = inp
        att = jnp.einsum('bnhd,bmhd->bhnm', qc, kc) * dmask
        out = jnp.einsum('bhnm,bmhe->bnhe', att, vc) + jnp.einsum('bnhd,bhde->bnhe', qc, S) * q_dec[None, :, :, None]
        S = S * s_dec[None, :, None, None] + jnp.einsum('bmhd,bmhe->bhde', kc * k_dec[None, :, :, None], vc)
        return S, out

    to_chunks = lambda t: jnp.moveaxis(t.reshape(B, nC, C, RET_HEADS, HEAD_DIM), 1, 0)
    S_T, o = lax.scan(chunk, S0.astype(jnp.float32), (to_chunks(q), to_chunks(k), to_chunks(v)))
    o = jnp.moveaxis(o, 0, 1).reshape(B, T, RET_HEADS, HEAD_DIM)
    y = jax.nn.silu(g) * _group_norm(o, ln_g, GN_EPS)
    return y.reshape(B, T, RET_WIDTH).astype(u.dtype), S_T


def _mixing_sublayer(x, lp, past_len, past_cmp, past_sel, win_buf, rwkv_S, rwkv_shift, ret_S):
    B, T, _ = x.shape
    h = _rms_norm(x, lp['norm_attn'])
    P = h @ lp['w_in']
    c = P[..., :NSA_IN]
    q = _rms_norm(c[..., :NSA_WIDTH].reshape(B, T, NSA_HEADS, HEAD_DIM), lp['nsa_q_norm'])
    kv = c[..., NSA_WIDTH:NSA_WIDTH + 6 * NSA_KV_COLS].reshape(B, T, 3, 2, NSA_KV_HEADS, HEAD_DIM)
    kv_cmp, kv_sel, kv_win = kv[:, :, 0], kv[:, :, 1], kv[:, :, 2]
    gates = jax.nn.sigmoid(c[..., NSA_WIDTH + 6 * NSA_KV_COLS:].astype(jnp.float32)).reshape(B, T, NSA_HEADS, 3)
    pos = past_len + jnp.arange(T, dtype=jnp.int32)
    slopes = _alibi_slopes(NSA_HEADS)
    prompt = past_cmp is None
    if prompt:
        rows_cmp, rows_sel, rows_win = kv_cmp, kv_sel, kv_win
        new_win = kv_win[:, T - min(WINDOW, T):]
    else:
        rows_win = jnp.concatenate([win_buf, kv_win.astype(win_buf.dtype)], axis=1)
        new_win = rows_win[:, T:]
    k_norm = lp['nsa_k_norm']
    if prompt:
        kc = _rms_norm(_nsa_compress(rows_cmp[:, :, 0], lp['nsa_cmp_pe'][0], lp['nsa_cmp_w1'][0], lp['nsa_cmp_w2'][0]), k_norm[0])
        vc = _nsa_compress(rows_cmp[:, :, 1], lp['nsa_cmp_pe'][1], lp['nsa_cmp_w1'][1], lp['nsa_cmp_w2'][1])
    else:
        n_cmp = (past_len + T - CMP_LEN) // CMP_STRIDE + 1
        assert (n_cmp - 1) * CMP_STRIDE + CMP_LEN <= past_len
        kcv = _compress_pages(past_cmp[0], past_cmp[1], B, lp['nsa_cmp_pe'], lp['nsa_cmp_w1'], lp['nsa_cmp_w2'], k_norm[0])
        kcv = kcv[:, :n_cmp].reshape(B, n_cmp, 2, NSA_KV_HEADS, HEAD_DIM)
        kc, vc = kcv[:, :, 0], kcv[:, :, 1]
    kw = _rms_norm(rows_win[:, :, 0], k_norm[2])
    vw = rows_win[:, :, 1]
    if prompt:
        o3 = _nsa_prompt_attn(q, kc, vc, _rms_norm(rows_sel[:, :, 0], k_norm[1]), rows_sel[:, :, 1], kw, vw)
    else:
        G, R, d = NSA_KV_HEADS, NSA_GROUP, HEAD_DIM
        rows_sel = jnp.concatenate([past_sel, kv_sel.astype(past_sel.dtype)], axis=1)
        ks_blk = _to_sel_blocks(_rms_norm(rows_sel[:, :, 0], k_norm[1]))
        vs_blk = _to_sel_blocks(rows_sel[:, :, 1])
        n_sel = ks_blk.shape[2]
        sl = slopes.reshape(G, R)

        def sel_branch(qg, idx):
            b_i = jnp.arange(B)[:, None, None, None]
            g_i = jnp.arange(G)[None, :, None, None]
            ks_g = ks_blk[b_i, g_i, idx]
            vs_g = vs_blk[b_i, g_i, idx]
            kpos = idx[..., None] * SEL_LEN + jnp.arange(SEL_LEN, dtype=jnp.int32)
            d_sel = (pos[None, None, :, None, None] - kpos).astype(jnp.float32)[:, :, None]
            s = (jnp.einsum('btgrd,bgtnsd->bgrtns', qg, ks_g).astype(jnp.float32) * (d ** -0.5)
                 - sl[None, :, :, None, None, None] * d_sel)
            mask = jnp.broadcast_to(d_sel >= 0.0, s.shape)
            p_sel = _masked_softmax(s.reshape(B, G, R, T, -1), mask.reshape(B, G, R, T, -1)).reshape(s.shape)
            return jnp.einsum('bgrtns,bgtnsd->btgrd', p_sel.astype(vs_g.dtype), vs_g)

        wb = win_buf.shape[1]
        pos_w = past_len - wb + jnp.arange(wb + T, dtype=jnp.int32)
        o3 = _nsa_branches(q, pos, kc, vc, n_sel, sel_branch, kw, vw, pos_w, slopes)
    o_nsa = jnp.einsum('btha,abthd->bthd', gates.astype(o3.dtype), o3)
    o_nsa = _rms_norm(o_nsa, lp['nsa_out_norm']).reshape(B, T, NSA_WIDTH)
    y_rwkv, new_shift, new_rwkv = _rwkv_group(P[..., NSA_IN:NSA_IN + RWKV_IN], rwkv_shift, rwkv_S, lp)
    y_ret, new_ret = _retention_group(P[..., NSA_IN + RWKV_IN:], pos, ret_S, lp['ret_ln_g'])
    mix = jnp.concatenate([o_nsa, y_rwkv.astype(o_nsa.dtype), y_ret.astype(o_nsa.dtype)], axis=-1)
    x = x + mix @ lp['w_out']
    return x, (kv_cmp, kv_sel, new_win, new_rwkv, new_shift, new_ret)


FFN_ROW_TILE = 512


def _dense_ffn(x, g, wg, wu, wd):
    shp = x.shape
    x2 = x.reshape(-1, shp[-1])
    tm = _pick_tile(x2.shape[0], FFN_ROW_TILE)
    n_tiles = x2.shape[0] // tm
    ones = jnp.ones((x2.shape[0], 1), jnp.float32)
    bf = jnp.bfloat16
    y = _ffn(x2, g, ones, wg[None].astype(bf), wu[None].astype(bf), wd[None].astype(bf),
             jnp.zeros((n_tiles,), jnp.int32),
             jnp.ones((n_tiles,), jnp.int32), tm=tm)
    return (x2 + y).reshape(shp)


def _moe_ffn(xs, g, router, wg, wu, wd):
    D = xs[0].shape[-1]
    E, tm = router.shape[1], FFN_ROW_TILE
    flat = [x.reshape(-1, D) for x in xs]
    x2 = jnp.concatenate(flat, axis=0)
    logits = jnp.concatenate([_router_logits(x, g, router) for x in flat], axis=0)
    top_val, top_idx = lax.top_k(logits, TOP_K)
    gate = jax.nn.softmax(top_val, axis=-1)
    N = x2.shape[0]
    A = N * TOP_K
    flat_e, flat_w = top_idx.reshape(A).astype(jnp.int32), gate.reshape(A)
    order = jnp.argsort(flat_e, stable=True).astype(jnp.int32)
    counts = jnp.sum(jax.nn.one_hot(flat_e, E, dtype=jnp.int32), axis=0)
    padded = (counts + tm - 1) // tm * tm
    start, p_end = jnp.cumsum(counts) - counts, jnp.cumsum(padded)
    p_start = p_end - padded
    sorted_e = flat_e[order]
    dest = p_start[sorted_e] + (jnp.arange(A, dtype=jnp.int32) - start[sorted_e])
    P = -(-(A + E * (tm - 1)) // tm) * tm
    row_tok = jnp.zeros((P,), jnp.int32).at[dest].set(order // TOP_K)
    row_w = jnp.zeros((P,), jnp.float32).at[dest].set(flat_w[order])
    tile_start = jnp.arange(P // tm, dtype=jnp.int32) * tm
    tile_e = jnp.minimum(jnp.searchsorted(p_end, tile_start, side='right'), E - 1).astype(jnp.int32)
    tile_used = (tile_start < p_end[-1]).astype(jnp.int32)
    bf = jnp.bfloat16
    ys = _ffn(x2[row_tok], g, row_w[:, None], wg.astype(bf), wu.astype(bf), wd.astype(bf), tile_e, tile_used, tm=tm)
    slot = jnp.zeros((A,), jnp.int32).at[order].set(dest).reshape(N, TOP_K)
    y = x2 + ys[slot[:, 0]] + ys[slot[:, 1]]
    outs, off = [], 0
    for x in xs:
        n = x.size // D
        outs.append(y[off:off + n].reshape(x.shape))
        off += n
    return outs


def kernel(x_prompt, x_sample, cache_nsa_cmp, cache_nsa_sel, cache_nsa_win, state_rwkv, state_rwkv_shift,
           state_ret, page_table, norm_attn, norm_ffn, w_in, w_out, nsa_q_norm, nsa_k_norm, nsa_cmp_pe,
           nsa_cmp_w1, nsa_cmp_w2, nsa_out_norm, rwkv_mu, rwkv_w0, rwkv_w_up, rwkv_a0, rwkv_a_up, rwkv_g_up,
           rwkv_k_k, rwkv_k_a, rwkv_r_k, rwkv_ln_g, rwkv_ln_b, ret_ln_g, ffn_w_gate, ffn_w_up, ffn_w_down,
           moe_router, moe_w_gate, moe_w_up, moe_w_down):
    Bp, Bs = x_prompt.shape[0], x_sample.shape[0]
    past_len = page_table.shape[1] * cache_nsa_cmp.shape[2]
    xp, xs = x_prompt, x_sample
    outs_p, outs_s = [], []
    for l in range(DEPTH):
        lp = {'norm_attn': norm_attn[l], 'w_in': w_in[l], 'w_out': w_out[l], 'nsa_q_norm': nsa_q_norm[l],
              'nsa_k_norm': nsa_k_norm[l], 'nsa_cmp_pe': nsa_cmp_pe[l], 'nsa_cmp_w1': nsa_cmp_w1[l],
              'nsa_cmp_w2': nsa_cmp_w2[l], 'nsa_out_norm': nsa_out_norm[l], 'rwkv_mu': rwkv_mu[l],
              'rwkv_w0': rwkv_w0[l], 'rwkv_w_up': rwkv_w_up[l], 'rwkv_a0': rwkv_a0[l], 'rwkv_a_up': rwkv_a_up[l],
              'rwkv_g_up': rwkv_g_up[l], 'rwkv_k_k': rwkv_k_k[l], 'rwkv_k_a': rwkv_k_a[l], 'rwkv_r_k': rwkv_r_k[l],
              'rwkv_ln_g': rwkv_ln_g[l], 'rwkv_ln_b': rwkv_ln_b[l], 'ret_ln_g': ret_ln_g[l]}
        xp, st = _mixing_sublayer(xp, lp, 0, None, None, None,
                                  jnp.zeros((Bp, RWKV_HEADS, HEAD_DIM, HEAD_DIM), jnp.float32),
                                  jnp.zeros((Bp, RWKV_IN), xp.dtype),
                                  jnp.zeros((Bp, RET_HEADS, HEAD_DIM, HEAD_DIM), jnp.float32))
        outs_p.append(st)
        n_pool, page_rows = cache_nsa_sel.shape[1], cache_nsa_sel.shape[2]
        pages = (page_table + l * n_pool).reshape(-1)
        past_cmp = (cache_nsa_cmp.reshape(DEPTH * n_pool, page_rows, -1), pages)
        past_sel = cache_nsa_sel[l][page_table].reshape(Bs, past_len, 2, NSA_KV_HEADS, HEAD_DIM)
        xs, st = _mixing_sublayer(xs, lp, past_len, past_cmp, past_sel, cache_nsa_win[l], state_rwkv[l],
                                  state_rwkv_shift[l], state_ret[l])
        outs_s.append(st)
        i = l // 2
        if l % 2 == 0:
            xp = _dense_ffn(xp, norm_ffn[l], ffn_w_gate[i], ffn_w_up[i], ffn_w_down[i])
            xs = _dense_ffn(xs, norm_ffn[l], ffn_w_gate[i], ffn_w_up[i], ffn_w_down[i])
        else:
            xp, xs = _moe_ffn([xp, xs], norm_ffn[l], moe_router[i], moe_w_gate[i], moe_w_up[i], moe_w_down[i])
    kv_cmp_p, kv_sel_p, win_p, rwkv_p, shift_p, ret_p = [jnp.stack([o[j] for o in outs_p]) for j in range(6)]
    kv_cmp_s, kv_sel_s, win_s, rwkv_s, shift_s, ret_s = [jnp.stack([o[j] for o in outs_s]) for j in range(6)]
    return (xp, xs, kv_cmp_p, kv_sel_p, win_p, rwkv_p, shift_p, ret_p,
            kv_cmp_s, kv_sel_s, win_s, rwkv_s, shift_s, ret_s)
```

```python
import functools

import jax
import jax.numpy as jnp
from jax import lax
from jax.experimental import pallas as pl
from jax.experimental.pallas import tpu as pltpu

D_MODEL = 1024
DEPTH = 2
HEAD_DIM = 64
NSA_WIDTH = D_MODEL // 2
RWKV_WIDTH = D_MODEL // 4
RET_WIDTH = D_MODEL - NSA_WIDTH - RWKV_WIDTH
NSA_HEADS = NSA_WIDTH // HEAD_DIM
NSA_KV_HEADS = 2
NSA_GROUP = NSA_HEADS // NSA_KV_HEADS
CMP_LEN = 32
CMP_STRIDE = 16
SEL_LEN = 64
SEL_TOP = 16
WINDOW = 512
Q_BLOCK = 128
RWKV_HEADS = RWKV_WIDTH // HEAD_DIM
RWKV_W_RANK = 64
RWKV_A_RANK = 64
RWKV_G_RANK = 128
RWKV_GN_EPS = 64e-5
RET_HEADS = RET_WIDTH // HEAD_DIM
RET_CHUNK = 128
ROPE_BASE = 10000.0
N_EXPERTS = 8
TOP_K = 2
NSA_KV_COLS = NSA_KV_HEADS * HEAD_DIM
NSA_IN = NSA_WIDTH + 6 * NSA_KV_COLS + 3 * NSA_HEADS
RWKV_IN = 3 * RWKV_WIDTH + RWKV_W_RANK + RWKV_A_RANK + RWKV_G_RANK
RET_IN = 4 * RET_WIDTH
EPS = 1e-6
GN_EPS = 1e-5
NEG = -1e30
TINY = 1e-30
FORCE = 1e9

VMEM_LIMIT_BYTES = 56 * 1024 * 1024


def _pick_tile(n, target):
    t = min(n, target)
    while n % t:
        t //= 2
    return t


def _ffn_body(expert_ref, used_ref, x_ref, g_ref, s_ref, wg_ref, wu_ref, wd_ref, y_ref, h_scr, acc_scr):
    i = pl.program_id(0)
    j = pl.program_id(1)
    last = pl.num_programs(1) - 1
    used = used_ref[i] > 0

    @pl.when(used & (j == 0))
    def _():
        x = x_ref[...]
        h = x * lax.rsqrt(jnp.mean(x * x, axis=-1, keepdims=True) + EPS) * g_ref[...]
        h_scr[...] = h.astype(jnp.bfloat16)
        acc_scr[...] = jnp.zeros_like(acc_scr)

    @pl.when(used)
    def _():
        h = h_scr[...]
        a = jnp.dot(h, wg_ref[...].astype(jnp.bfloat16), preferred_element_type=jnp.float32)
        b = jnp.dot(h, wu_ref[...].astype(jnp.bfloat16), preferred_element_type=jnp.float32)
        z = (a * jax.nn.sigmoid(a)) * b
        acc_scr[...] += jnp.dot(z.astype(jnp.bfloat16), wd_ref[...].astype(jnp.bfloat16),
                                preferred_element_type=jnp.float32)

    @pl.when(used & (j == last))
    def _():
        y_ref[...] = acc_scr[...] * s_ref[...]

    @pl.when(jnp.logical_not(used) & (j == last))
    def _():
        y_ref[...] = jnp.zeros_like(y_ref)


def _ffn(x, g, scale, wg, wu, wd, tile_expert, tile_used, *, tm, tf=512):
    M, D = x.shape
    F = wg.shape[2]
    tf = _pick_tile(F, tf)
    grid_spec = pltpu.PrefetchScalarGridSpec(
        num_scalar_prefetch=2,
        grid=(M // tm, F // tf),
        in_specs=[
            pl.BlockSpec((tm, D), lambda i, j, e, u: (i, 0)),
            pl.BlockSpec((1, D), lambda i, j, e, u: (0, 0)),
            pl.BlockSpec((tm, 1), lambda i, j, e, u: (i, 0)),
            pl.BlockSpec((None, D, tf), lambda i, j, e, u: (e[i], 0, j)),
            pl.BlockSpec((None, D, tf), lambda i, j, e, u: (e[i], 0, j)),
            pl.BlockSpec((None, tf, D), lambda i, j, e, u: (e[i], j, 0)),
        ],
        out_specs=pl.BlockSpec((tm, D), lambda i, j, e, u: (i, 0)),
        scratch_shapes=[pltpu.VMEM((tm, D), jnp.bfloat16), pltpu.VMEM((tm, D), jnp.float32)],
    )
    return pl.pallas_call(
        _ffn_body,
        grid_spec=grid_spec,
        out_shape=jax.ShapeDtypeStruct((M, D), jnp.float32),
        compiler_params=pltpu.CompilerParams(
            dimension_semantics=("parallel", "arbitrary"), vmem_limit_bytes=VMEM_LIMIT_BYTES),
        name="ffn",
    )(tile_expert, tile_used, x, g.reshape(1, D), scale, wg, wu, wd)


def _router_body(x_ref, g_ref, w_ref, o_ref):
    x = x_ref[...]
    h = x * lax.rsqrt(jnp.mean(x * x, axis=-1, keepdims=True) + EPS) * g_ref[...]
    o_ref[...] = jnp.dot(h, w_ref[...], preferred_element_type=jnp.float32, precision=lax.Precision.HIGHEST)


def _router_logits(x, g, router, *, tm=512):
    M, D = x.shape
    E = router.shape[1]
    tm = _pick_tile(M, tm)
    lanes = 128
    w = jnp.pad(router, ((0, 0), (0, lanes - E)))
    out = pl.pallas_call(
        _router_body,
        grid=(M // tm,),
        in_specs=[pl.BlockSpec((tm, D), lambda i: (i, 0)), pl.BlockSpec((1, D), lambda i: (0, 0)),
                  pl.BlockSpec((D, lanes), lambda i: (0, 0))],
        out_specs=pl.BlockSpec((tm, lanes), lambda i: (i, 0)),
        out_shape=jax.ShapeDtypeStruct((M, lanes), jnp.float32),
        compiler_params=pltpu.CompilerParams(dimension_semantics=("parallel",), vmem_limit_bytes=VMEM_LIMIT_BYTES),
        name="router",
    )(x, g.reshape(1, D), w)
    return out[:, :E]


def _rwkv_scan_body(r_ref, w_ref, k_ref, v_ref, kk_ref, b_ref, s0_ref, y_ref, st_ref, s_scr, *, sub):
    j = pl.program_id(1)
    n_b, t_blk, _ = r_ref.shape
    d = HEAD_DIM

    @pl.when(j == 0)
    def _():
        s_scr[...] = s0_ref[...]

    eye = lax.broadcasted_iota(jnp.int32, (d, d), 0) == lax.broadcasted_iota(jnp.int32, (d, d), 1)

    def sub_block(i, carry):
        t0 = pl.multiple_of(i * sub, sub)
        for bb in range(n_b):
            blk = [ref[bb, pl.ds(t0, sub), :] for ref in (r_ref, w_ref, k_ref, v_ref, kk_ref, b_ref)]
            for h in range(RWKV_HEADS):
                cols = slice(h * d, (h + 1) * d)
                S = s_scr[bb, h]
                y_rows = []
                for t in range(sub):
                    r_t, w_t, k_t, v_t, kk_t, b_t = [x[t:t + 1, cols] for x in blk]
                    sa = jnp.sum(S * kk_t, axis=1, keepdims=True)
                    v_col = jnp.sum(jnp.where(eye, v_t, 0.0), axis=1, keepdims=True)
                    S = S * w_t - sa * b_t + v_col * k_t
                    y_col = jnp.sum(S * r_t, axis=1, keepdims=True)
                    y_rows.append(jnp.sum(jnp.where(eye, y_col, 0.0), axis=0, keepdims=True))
                s_scr[bb, h] = S
                y_ref[bb, pl.ds(t0, sub), cols] = jnp.concatenate(y_rows, axis=0)
        return carry

    lax.fori_loop(0, t_blk // sub, sub_block, 0)

    @pl.when(j == pl.num_programs(1) - 1)
    def _():
        st_ref[...] = s_scr[...]


def _rwkv_scan(r, w, k, v, kk, b, s0, *, n_b=2, t_blk=256):
    B, T, W = r.shape
    n_b = _pick_tile(B, n_b)
    t_blk = _pick_tile(T, t_blk)
    sub = 8 if t_blk % 8 == 0 else t_blk
    seq = pl.BlockSpec((n_b, t_blk, W), lambda i, j: (i, j, 0))
    st = pl.BlockSpec((n_b, RWKV_HEADS, HEAD_DIM, HEAD_DIM), lambda i, j: (i, 0, 0, 0))
    return pl.pallas_call(
        functools.partial(_rwkv_scan_body, sub=sub),
        grid=(B // n_b, T // t_blk),
        in_specs=[seq] * 6 + [st],
        out_specs=[seq, st],
        out_shape=[jax.ShapeDtypeStruct((B, T, W), jnp.float32),
                   jax.ShapeDtypeStruct((B, RWKV_HEADS, HEAD_DIM, HEAD_DIM), jnp.float32)],
        scratch_shapes=[pltpu.VMEM((n_b, RWKV_HEADS, HEAD_DIM, HEAD_DIM), jnp.float32)],
        compiler_params=pltpu.CompilerParams(
            dimension_semantics=("parallel", "arbitrary"), vmem_limit_bytes=VMEM_LIMIT_BYTES),
        name="rwkv_scan",
    )(r, w, k, v, kk, b, s0)


RWKV_CHUNK = 64
RWKV_CHUNKS_PER_STEP = 4


_F32_DOT = dict(preferred_element_type=jnp.float32, precision=lax.Precision.HIGHEST)


def _dot(a, b):
    return jnp.dot(a, b, **_F32_DOT)


def _dot_t(a, b):
    return lax.dot_general(a, b, (((1,), (1,)), ((), ())), **_F32_DOT)


def _dot_0(a, b):
    return lax.dot_general(a, b, (((0,), (0,)), ((), ())), **_F32_DOT)


_NN = (((1,), (0,)), ((), ()))
_NT = (((1,), (1,)), ((), ()))


def _split(x):
    hi = x.astype(jnp.bfloat16)
    return hi, (x - hi.astype(jnp.float32)).astype(jnp.bfloat16)


def _mm3(a, b, dims):
    dot = lambda x, y: lax.dot_general(x, y, dims, preferred_element_type=jnp.float32)
    return dot(a[0], b[0]) + (dot(a[0], b[1]) + dot(a[1], b[0]))


def _rwkv_chunk_body(r_ref, lw_ref, k_ref, v_ref, kk_ref, b_ref, qe_ref, y0_ref, pm_ref, z_ref):
    L, d = RWKV_CHUNK, HEAD_DIM
    n = r_ref.shape[1]
    row = lax.broadcasted_iota(jnp.int32, (n, n), 0)
    col = lax.broadcasted_iota(jnp.int32, (n, n), 1)
    same = (row // L) == (col // L)
    lower = same & (row >= col)
    strict = same & (row > col)
    ones_lower = jnp.where(lower, 1.0, 0.0)
    eye = jnp.where(row == col, 1.0, 0.0)
    eye_d = eye[:d, :d]
    ones_bf = ones_lower.astype(jnp.bfloat16)
    for h in range(RWKV_HEADS):
        cols = slice(h * d, (h + 1) * d)
        r, lw, k, v, kk, b = [ref[0, :, cols] for ref in (r_ref, lw_ref, k_ref, v_ref, kk_ref, b_ref)]
        lw_hi, lw_lo = _split(lw)
        lw_rest = (lw - lw_hi.astype(jnp.float32) - lw_lo.astype(jnp.float32)).astype(jnp.bfloat16)
        G = sum(jnp.dot(ones_bf, t, preferred_element_type=jnp.float32) for t in (lw_hi, lw_lo, lw_rest))
        g_inv = jnp.exp(-G)
        kap, bt, kt, rt = kk * jnp.exp(G - lw), b * g_inv, k * g_inv, r * jnp.exp(G)
        kap2, bt2, kt2, rt2, v2 = _split(kap), _split(bt), _split(kt), _split(rt), _split(v)
        N = jnp.where(strict, _mm3(kap2, bt2, _NT), 0.0)
        Mk = jnp.where(strict, _mm3(kap2, kt2, _NT), 0.0)
        RB = jnp.where(lower, _mm3(rt2, bt2, _NT), 0.0)
        RK = jnp.where(lower, _mm3(rt2, kt2, _NT), 0.0)
        X, P2 = eye - N, _split(N)
        for _ in range(L.bit_length() - 2):
            P2 = _split(_mm3(P2, P2, _NN))
            X = X + _mm3(_split(X), P2, _NN)
        X2, RB2 = _split(X), _split(RB)
        A = _mm3(X2, kap2, _NN)
        C = _mm3(X2, _split(_mm3(_split(Mk), v2, _NN)), _NN)
        qe_ref[0, h] = rt - _mm3(RB2, _split(A), _NN)
        y0_ref[0, h] = _mm3(_split(RK), v2, _NN) - _mm3(RB2, _split(C), _NN)
        for c in range(n // L):
            rows = slice(c * L, (c + 1) * L)
            g_end = jnp.exp(G[(c + 1) * L - 1:(c + 1) * L, :])
            pm_ref[0, h, c] = (eye_d - _dot_0(A[rows], bt[rows])) * g_end
            z_ref[0, h, c] = (_dot_0(v[rows], kt[rows]) - _dot_0(C[rows], bt[rows])) * g_end


def _rwkv_walk_body(qe_ref, y0_ref, pm_ref, z_ref, s0_ref, y_ref, st_ref, s_scr):
    j = pl.program_id(0)
    B, H, n_c = pm_ref.shape[:3]
    L = RWKV_CHUNK

    @pl.when(j == 0)
    def _():
        s_scr[...] = s0_ref[...]

    def one_chunk(c, carry):
        t0 = pl.multiple_of(c * L, L)
        for bb in range(B):
            for h in range(H):
                S = s_scr[bb, h]
                y_ref[bb, h, pl.ds(t0, L), :] = _dot_t(qe_ref[bb, h, pl.ds(t0, L), :], S) + y0_ref[bb, h, pl.ds(t0, L), :]
                s_scr[bb, h] = _dot(S, pm_ref[bb, h, c]) + z_ref[bb, h, c]
        return carry

    lax.fori_loop(0, n_c, one_chunk, 0)

    @pl.when(j == pl.num_programs(0) - 1)
    def _():
        st_ref[...] = s_scr[...]


def _rwkv_chunked(r, lw, k, v, kk, b, s0, *, chunks_per_step=16):
    B, T, W = r.shape
    H, d, L = RWKV_HEADS, HEAD_DIM, RWKV_CHUNK
    n_c = T // L
    group = _pick_tile(n_c, RWKV_CHUNKS_PER_STEP)
    seq = pl.BlockSpec((1, group * L, W), lambda i, c: (i, c, 0))
    per_tok = pl.BlockSpec((1, H, group * L, d), lambda i, c: (i, 0, c, 0))
    per_chunk = pl.BlockSpec((1, H, group, d, d), lambda i, c: (i, 0, c, 0, 0))
    qe, y0, pm, z = pl.pallas_call(
        _rwkv_chunk_body,
        grid=(B, n_c // group),
        in_specs=[seq] * 6,
        out_specs=[per_tok, per_tok, per_chunk, per_chunk],
        out_shape=[jax.ShapeDtypeStruct((B, H, T, d), jnp.float32)] * 2
                  + [jax.ShapeDtypeStruct((B, H, n_c, d, d), jnp.float32)] * 2,
        compiler_params=pltpu.CompilerParams(
            dimension_semantics=("parallel", "parallel"), vmem_limit_bytes=VMEM_LIMIT_BYTES),
        name="rwkv_chunk",
    )(r, lw, k, v, kk, b)
    cs = _pick_tile(n_c, chunks_per_step)
    tok = pl.BlockSpec((B, H, cs * L, d), lambda j: (0, 0, j, 0))
    chk = pl.BlockSpec((B, H, cs, d, d), lambda j: (0, 0, j, 0, 0))
    st = pl.BlockSpec((B, H, d, d), lambda j: (0, 0, 0, 0))
    y, s_t = pl.pallas_call(
        _rwkv_walk_body,
        grid=(n_c // cs,),
        in_specs=[tok, tok, chk, chk, st],
        out_specs=[tok, st],
        out_shape=[jax.ShapeDtypeStruct((B, H, T, d), jnp.float32), jax.ShapeDtypeStruct((B, H, d, d), jnp.float32)],
        scratch_shapes=[pltpu.VMEM((B, H, d, d), jnp.float32)],
        compiler_params=pltpu.CompilerParams(
            dimension_semantics=("arbitrary",), vmem_limit_bytes=VMEM_LIMIT_BYTES),
        name="rwkv_walk",
    )(qe, y0, pm, z, s0)
    return jnp.transpose(y, (0, 2, 1, 3)).reshape(B, T, W), s_t


SEL_CHUNK = 512
WIN_SPAN = WINDOW + Q_BLOCK


def _nsa_scores(k_aug, qT_ref):
    return [jnp.dot(k_aug, qT_ref[0, 0, r], preferred_element_type=jnp.float32) for r in range(NSA_GROUP)]


def _nsa_softmax_cols(s_list, neg):
    out = []
    for s in s_list:
        s = s + neg
        m = jnp.max(s, axis=0, keepdims=True)
        e = jnp.exp(s - m)
        inv = jnp.where(m > 0.5 * NEG, 1.0, 0.0) / jnp.maximum(jnp.sum(e, axis=0, keepdims=True), TINY)
        out.append(e * inv)
    return out


def _nsa_prompt_body(qT_ref, kc_ref, vcT_ref, ks_ref, vsT_ref, kw_ref, vwT_ref, covT_ref, o_ref,
                     sel_scr, m_scr, l_scr, acc_scr):
    i = pl.program_id(2)
    R = NSA_GROUP
    n_cmp = kc_ref.shape[2]
    n_sel = covT_ref.shape[0]
    q0 = i * Q_BLOCK
    pos = q0 + lax.broadcasted_iota(jnp.int32, (1, Q_BLOCK), 1)
    posf = pos.astype(jnp.float32)
    bf = jnp.bfloat16

    c_end = (lax.broadcasted_iota(jnp.int32, (n_cmp, Q_BLOCK), 0) * CMP_STRIDE + (CMP_LEN - 1)).astype(jnp.float32)
    neg_c = jnp.where(posf - c_end >= 0.0, 0.0, NEG)
    p_c = _nsa_softmax_cols(_nsa_scores(kc_ref[0, 0], qT_ref), neg_c)
    vcT = vcT_ref[0, 0]
    o_c = [jnp.dot(vcT, p.astype(bf), preferred_element_type=jnp.float32) for p in p_c]
    for r in range(R):
        o_ref[0, 0, 0, r] = o_c[r]
    p_sum = (p_c[0] + p_c[1]) + (p_c[2] + p_c[3])
    imp = jnp.dot(covT_ref[...], p_sum, preferred_element_type=jnp.float32, precision=lax.Precision.HIGHEST)
    blk = lax.broadcasted_iota(jnp.int32, (n_sel, Q_BLOCK), 0)
    avail = blk * SEL_LEN <= pos
    forced = (blk == jnp.right_shift(pos, 6)) | (blk == 0)
    imp = jnp.where(avail, jnp.where(forced, FORCE, imp), -FORCE)

    sub8 = lax.broadcasted_iota(jnp.int32, (8, Q_BLOCK), 0)
    for jj in range(n_sel // 8):
        vj = imp[8 * jj:8 * jj + 8]
        cnt = jnp.zeros((8, Q_BLOCK), jnp.float32)
        for k in range(n_sel):
            row = imp[k:k + 1]
            if k < 8 * jj:
                cnt = cnt + jnp.where(row >= vj, 1.0, 0.0)
            elif k >= 8 * jj + 8:
                cnt = cnt + jnp.where(row > vj, 1.0, 0.0)
            else:
                cnt = cnt + jnp.where(sub8 > (k - 8 * jj), jnp.where(row >= vj, 1.0, 0.0),
                                      jnp.where(row > vj, 1.0, 0.0))
        sel_scr[8 * jj:8 * jj + 8, :] = jnp.where(cnt < float(SEL_TOP), 1.0, 0.0)

    m_scr[...] = jnp.full_like(m_scr, NEG)
    l_scr[...] = jnp.zeros_like(l_scr)
    acc_scr[...] = jnp.zeros_like(acc_scr)
    key_iota = lax.broadcasted_iota(jnp.int32, (SEL_CHUNK, Q_BLOCK), 0).astype(jnp.float32)
    blocks_per_chunk = SEL_CHUNK // SEL_LEN

    def chunk(c, carry):
        k0 = pl.multiple_of(c * SEL_CHUNK, SEL_CHUNK)
        s_all = _nsa_scores(ks_ref[0, 0, pl.ds(k0, SEL_CHUNK), :], qT_ref)
        dist = (posf - k0.astype(jnp.float32)) - key_iota
        picked = jnp.concatenate(
            [jnp.broadcast_to(sel_scr[pl.ds(c * blocks_per_chunk + b, 1), :], (SEL_LEN, Q_BLOCK))
             for b in range(blocks_per_chunk)], axis=0)
        neg = jnp.where((picked > 0.5) & (dist >= 0.0), 0.0, NEG)
        m_old = [m_scr[r] for r in range(R)]
        m_new, p_all, l_add = [], [], []
        for r in range(R):
            s = s_all[r] + neg
            m = jnp.maximum(m_old[r], jnp.max(s, axis=0, keepdims=True))
            p = jnp.exp(s - m)
            m_new.append(m)
            l_add.append(jnp.sum(p, axis=0, keepdims=True))
            p_all.append(p.astype(bf))
        vsT = vsT_ref[0, 0, :, pl.ds(k0, SEL_CHUNK)]
        pv = [jnp.dot(vsT, p, preferred_element_type=jnp.float32) for p in p_all]
        for r in range(R):
            alpha = jnp.exp(m_old[r] - m_new[r])
            l_scr[r] = alpha * l_scr[r] + l_add[r]
            acc_scr[r] = alpha * acc_scr[r] + pv[r]
            m_scr[r] = m_new[r]
        return carry

    lax.fori_loop(0, q0 // SEL_CHUNK + 1, chunk, 0)
    for r in range(R):
        o_ref[1, 0, 0, r] = acc_scr[r] / jnp.maximum(l_scr[r], TINY)

    k0w = pl.multiple_of(jnp.maximum(q0 - WINDOW, 0), Q_BLOCK)
    d_win = (posf - k0w.astype(jnp.float32)) - lax.broadcasted_iota(jnp.int32, (WIN_SPAN, Q_BLOCK), 0).astype(jnp.float32)
    neg_w = jnp.where((d_win >= 0.0) & (d_win < float(WINDOW)), 0.0, NEG)
    p_w = _nsa_softmax_cols(_nsa_scores(kw_ref[0, 0, pl.ds(k0w, WIN_SPAN), :], qT_ref), neg_w)
    vwT = vwT_ref[0, 0, :, pl.ds(k0w, WIN_SPAN)]
    o_w = [jnp.dot(vwT, p.astype(bf), preferred_element_type=jnp.float32) for p in p_w]
    for r in range(R):
        o_ref[2, 0, 0, r] = o_w[r]


def _nsa_prompt_attn(q, kc, vc, ks, vs, kw, vw):
    B, T, H, d = q.shape
    G, R = NSA_KV_HEADS, NSA_GROUP
    n_c = kc.shape[1]
    n_cmp = -(-n_c // 128) * 128
    n_sel = T // SEL_LEN
    bf = jnp.bfloat16
    lanes = 128

    def key_rows(k, key_pos):
        n = k.shape[1]
        extra = jnp.zeros((n, lanes - d), jnp.float32).at[:, 0].set((key_pos // SEL_LEN).astype(jnp.float32))
        extra = extra.at[:, 1].set((key_pos % SEL_LEN).astype(jnp.float32))
        extra = jnp.broadcast_to(extra[None, None], (B, G, n, lanes - d))
        return jnp.concatenate([jnp.transpose(k, (0, 2, 1, 3)), extra], axis=-1).astype(bf)

    cols = lambda t: jnp.transpose(t, (0, 2, 3, 1)).astype(bf)
    slopes = (2.0 ** -jnp.arange(1, H + 1, dtype=jnp.float32)).reshape(G, R)
    q_extra = jnp.zeros((G, R, lanes - d), jnp.float32).at[:, :, 0].set(SEL_LEN * slopes).at[:, :, 1].set(slopes)
    qT = jnp.transpose((q * (d ** -0.5)).reshape(B, T, G, R, d), (0, 2, 3, 4, 1))
    qT = jnp.concatenate([qT, jnp.broadcast_to(q_extra[None, :, :, :, None], (B, G, R, lanes - d, T))],
                         axis=3).astype(bf)
    pad_c = ((0, 0), (0, n_cmp - n_c), (0, 0), (0, 0))
    kc_r = key_rows(jnp.pad(kc, pad_c), jnp.arange(n_cmp, dtype=jnp.int32) * CMP_STRIDE + (CMP_LEN - 1))
    vc_c = cols(jnp.pad(vc, pad_c))
    tok = jnp.arange(T, dtype=jnp.int32)
    rows = lambda t: key_rows(t, tok)
    c_start = jnp.arange(n_cmp, dtype=jnp.int32) * CMP_STRIDE
    s_start = jnp.arange(n_sel, dtype=jnp.int32) * SEL_LEN
    covT = jnp.maximum(jnp.minimum(c_start[None, :] + CMP_LEN, s_start[:, None] + SEL_LEN)
                       - jnp.maximum(c_start[None, :], s_start[:, None]), 0).astype(jnp.float32) / CMP_LEN
    full_r = lambda n: pl.BlockSpec((1, 1, n, lanes), lambda b, g, i: (b, g, 0, 0))
    full_c = lambda n: pl.BlockSpec((1, 1, d, n), lambda b, g, i: (b, g, 0, 0))
    oT = pl.pallas_call(
        _nsa_prompt_body,
        grid=(B, G, T // Q_BLOCK),
        in_specs=[pl.BlockSpec((1, 1, R, lanes, Q_BLOCK), lambda b, g, i: (b, g, 0, 0, i)),
                  full_r(n_cmp), full_c(n_cmp), full_r(T), full_c(T), full_r(T), full_c(T),
                  pl.BlockSpec((n_sel, n_cmp), lambda b, g, i: (0, 0))],
        out_specs=pl.BlockSpec((3, 1, 1, R, d, Q_BLOCK), lambda b, g, i: (0, b, g, 0, 0, i)),
        out_shape=jax.ShapeDtypeStruct((3, B, G, R, d, T), jnp.float32),
        scratch_shapes=[pltpu.VMEM((n_sel, Q_BLOCK), jnp.float32),
                        pltpu.VMEM((R, 1, Q_BLOCK), jnp.float32),
                        pltpu.VMEM((R, 1, Q_BLOCK), jnp.float32),
                        pltpu.VMEM((R, d, Q_BLOCK), jnp.float32)],
        compiler_params=pltpu.CompilerParams(
            dimension_semantics=("parallel", "parallel", "arbitrary"), vmem_limit_bytes=VMEM_LIMIT_BYTES),
        name="nsa_prompt",
    )(qT, kc_r, vc_c, rows(ks), cols(vs), rows(kw), cols(vw), covT)
    return jnp.transpose(oT, (0, 1, 5, 2, 3, 4)).reshape(3, B, T, H, d)


PAGES_PER_STEP = 8
def _compress_pages_body(pt_ref, *rest):
    page_refs = rest[:PAGES_PER_STEP]
    w1_ref, c0_ref, w2_ref, ones_ref, gain_ref, o_ref, x_scr = rest[PAGES_PER_STEP:]
    j = pl.program_id(1)
    rows = page_refs[0].shape[0]
    for i, ref in enumerate(page_refs):
        x_scr[pl.ds(pl.multiple_of((j * PAGES_PER_STEP + i) * rows, rows), rows), :] = ref[...]

    @pl.when(j == pl.num_programs(1) - 1)
    def _():
        n, half = x_scr.shape[0], w2_ref.shape[0]
        ab = jnp.dot(x_scr[...].astype(jnp.bfloat16), w1_ref[...], preferred_element_type=jnp.float32)
        top, bottom = ab[:, :half], ab[:, half:]
        nxt = jnp.concatenate([bottom[1:], jnp.zeros((1, half), jnp.float32)], axis=0)
        hid = jax.nn.gelu(top + nxt + c0_ref[...])
        y = jnp.dot(hid.astype(jnp.bfloat16), w2_ref[...], preferred_element_type=jnp.float32)
        mean_sq = jnp.dot(y * y, ones_ref[...], preferred_element_type=jnp.float32, precision=lax.Precision.HIGHEST)
        is_key = lax.broadcasted_iota(jnp.int32, (n, half), 1) < half // 2
        o_ref[0] = jnp.where(is_key, y * lax.rsqrt(mean_sq + EPS) * gain_ref[...], y)


def _compress_pages(pool, pages, n_batch, pe, w1, w2, k_gain):
    d, G = HEAD_DIM, NSA_KV_HEADS
    N, page_rows, W = pool.shape
    cpp = page_rows // CMP_STRIDE
    n_pages = pages.shape[0] // n_batch
    halves = CMP_LEN // CMP_STRIDE
    w1r = w1.reshape(2, halves, CMP_STRIDE, d, d)[jnp.array([0] * G + [1] * G)]
    eye = jnp.eye(2 * G, dtype=jnp.float32)
    w_big = jnp.transpose(w1r, (2, 0, 3, 1, 4))[:, :, :, :, None, :] * eye[None, :, None, None, :, None]
    w_big = w_big.reshape(CMP_STRIDE * W, halves * W).astype(jnp.bfloat16)
    c0 = jnp.einsum('kn,kne->ke', pe.reshape(2, CMP_LEN * d), w1, precision=lax.Precision.HIGHEST)
    c0 = jnp.repeat(c0, G, axis=0).reshape(1, W)
    w2_big = (w2[jnp.array([0] * G + [1] * G)][:, :, None, :] * eye[:, None, :, None]).reshape(W, W).astype(jnp.bfloat16)
    ones_blk = jnp.kron(eye, jnp.full((d, d), 1.0 / d, jnp.float32))
    gain = jnp.concatenate([jnp.tile(k_gain, G), jnp.ones((G * d,), jnp.float32)]).reshape(1, W)
    pool_chunks = pool.reshape(N, cpp, CMP_STRIDE * W)
    page_spec = lambda k: pl.BlockSpec((None, cpp, CMP_STRIDE * W),
                                       lambda b, j, pt: (pt[b * n_pages + j * PAGES_PER_STEP + k], 0, 0))
    const = lambda shape: pl.BlockSpec(shape, lambda b, j, pt: (0, 0))
    grid_spec = pltpu.PrefetchScalarGridSpec(
        num_scalar_prefetch=1,
        grid=(n_batch, n_pages // PAGES_PER_STEP),
        in_specs=[page_spec(k) for k in range(PAGES_PER_STEP)]
                 + [const(w_big.shape), const((1, W)), const((W, W)), const((W, W)), const((1, W))],
        out_specs=pl.BlockSpec((1, n_pages * cpp, W), lambda b, j, pt: (b, 0, 0)),
        scratch_shapes=[pltpu.VMEM((n_pages * cpp, CMP_STRIDE * W), jnp.float32)],
    )
    return pl.pallas_call(
        _compress_pages_body,
        grid_spec=grid_spec,
        out_shape=jax.ShapeDtypeStruct((n_batch, n_pages * cpp, W), jnp.float32),
        compiler_params=pltpu.CompilerParams(
            dimension_semantics=("parallel", "arbitrary"), vmem_limit_bytes=VMEM_LIMIT_BYTES),
        name="compress_pages",
    )(pages, *([pool_chunks] * PAGES_PER_STEP), w_big, c0, w2_big, ones_blk, gain)


def _rms_norm(x, g):
    xf = x.astype(jnp.float32)
    y = xf * lax.rsqrt(jnp.mean(xf * xf, axis=-1, keepdims=True) + EPS)
    return (y * g.astype(jnp.float32)).astype(x.dtype)


def _group_norm(x, g, eps):
    xf = x.astype(jnp.float32)
    mu = jnp.mean(xf, axis=-1, keepdims=True)
    var = jnp.mean(jnp.square(xf - mu), axis=-1, keepdims=True)
    return (xf - mu) * lax.rsqrt(var + eps) * g.astype(jnp.float32)


def _masked_softmax(s, mask):
    s = jnp.where(mask, s, NEG)
    m = jnp.max(s, axis=-1, keepdims=True)
    e = jnp.where(mask, jnp.exp(s - m), 0.0)
    return e / jnp.maximum(jnp.sum(e, axis=-1, keepdims=True), TINY)


def _alibi_slopes(n):
    return 2.0 ** (-8.0 * jnp.arange(1, n + 1, dtype=jnp.float32) / n)


def _rotary(x, pos):
    half = x.shape[-1] // 2
    freqs = ROPE_BASE ** (-jnp.arange(half, dtype=jnp.float32) / half)
    ang = pos.astype(jnp.float32)[:, None] * freqs[None, :]
    cos, sin = jnp.cos(ang)[None, :, None, :], jnp.sin(ang)[None, :, None, :]
    x1, x2 = x[..., :half], x[..., half:]
    return jnp.concatenate([x1 * cos - x2 * sin, x1 * sin + x2 * cos], axis=-1)


def _nsa_compress(rows, pe, w1, w2):
    B, T, G, d = rows.shape
    n_cmp = (T - CMP_LEN) // CMP_STRIDE + 1
    idx = (jnp.arange(n_cmp, dtype=jnp.int32) * CMP_STRIDE)[:, None] + jnp.arange(CMP_LEN, dtype=jnp.int32)[None, :]
    blk = rows[:, idx] + pe[None, None, :, None, :]
    blk = jnp.transpose(blk, (0, 1, 3, 2, 4)).reshape(B, n_cmp, G, CMP_LEN * d)
    return jax.nn.gelu(blk @ w1) @ w2


def _to_sel_blocks(rows):
    B, T, G, d = rows.shape
    n_sel = -(-T // SEL_LEN)
    rows = jnp.pad(rows, ((0, 0), (0, n_sel * SEL_LEN - T), (0, 0), (0, 0)))
    return jnp.transpose(rows.reshape(B, n_sel, SEL_LEN, G, d), (0, 3, 1, 2, 4))


def _nsa_branches(q, pos, kc, vc, n_sel, sel_branch, kw, vw, pos_w, slopes):
    B, Tq, H, d = q.shape
    G, R = NSA_KV_HEADS, NSA_GROUP
    scale = d ** -0.5
    qg = q.reshape(B, Tq, G, R, d)
    sl = slopes.reshape(G, R)
    posf = pos.astype(jnp.float32)
    n_cmp = kc.shape[1]
    c_start = jnp.arange(n_cmp, dtype=jnp.int32) * CMP_STRIDE
    d_cmp = posf[:, None] - (c_start + CMP_LEN - 1).astype(jnp.float32)[None, :]
    s = jnp.einsum('btgrd,bngd->bgrtn', qg, kc).astype(jnp.float32) * scale - sl[None, :, :, None, None] * d_cmp
    p_cmp = _masked_softmax(s, (d_cmp >= 0.0)[None, None, None])
    o_cmp = jnp.einsum('bgrtn,bngd->btgrd', p_cmp.astype(vc.dtype), vc)
    s_start = jnp.arange(n_sel, dtype=jnp.int32) * SEL_LEN
    cover = jnp.maximum(jnp.minimum(c_start[:, None] + CMP_LEN, s_start[None, :] + SEL_LEN)
                        - jnp.maximum(c_start[:, None], s_start[None, :]), 0).astype(jnp.float32) / CMP_LEN
    imp = jnp.einsum('bgrtn,nj->bgtj', p_cmp, cover)
    blk = jnp.arange(n_sel, dtype=jnp.int32)
    avail = s_start[None, :] <= pos[:, None]
    forced = (blk[None, :] == (pos // SEL_LEN)[:, None]) | (blk[None, :] == 0)
    imp = jnp.where(avail, jnp.where(forced, FORCE, imp), -FORCE)
    n_top = min(SEL_TOP, n_sel)
    _, idx = lax.top_k(imp, n_top)
    o_sel = sel_branch(qg, idx)
    d_win = posf[:, None] - pos_w.astype(jnp.float32)[None, :]
    s = jnp.einsum('btgrd,bwgd->bgrtw', qg, kw).astype(jnp.float32) * scale - sl[None, :, :, None, None] * d_win
    win_mask = (d_win >= 0.0) & (d_win < WINDOW) & (pos_w >= 0)[None, :]
    p_win = _masked_softmax(s, win_mask[None, None, None])
    o_win = jnp.einsum('bgrtw,bwgd->btgrd', p_win.astype(vw.dtype), vw)
    return jnp.stack([o_cmp, o_sel, o_win]).reshape(3, B, Tq, H, d)


def _rwkv_group(u, shift0, S0, lp):
    B, T, _ = u.shape
    W = RWKV_WIDTH
    uf = u.astype(jnp.float32)
    prev = jnp.concatenate([shift0.astype(jnp.float32)[:, None], uf[:, :-1]], axis=1)
    um = uf + (prev - uf) * lp['rwkv_mu']
    r, k, v = um[..., :W], um[..., W:2 * W], um[..., 2 * W:3 * W]
    o = 3 * W
    wd = um[..., o:o + RWKV_W_RANK]
    ad = um[..., o + RWKV_W_RANK:o + RWKV_W_RANK + RWKV_A_RANK]
    gd = um[..., o + RWKV_W_RANK + RWKV_A_RANK:]
    w = lp['rwkv_w0'] + jnp.tanh(wd) @ lp['rwkv_w_up']
    log_decay = -jnp.exp(-jax.nn.softplus(-w) - 0.5)
    a = jax.nn.sigmoid(lp['rwkv_a0'] + ad @ lp['rwkv_a_up'])
    g = jax.nn.sigmoid(gd) @ lp['rwkv_g_up']
    kk = k * lp['rwkv_k_k']
    k = k * (1.0 + (a - 1.0) * lp['rwkv_k_a'])
    hd = lambda t: t.reshape(B, T, RWKV_HEADS, HEAD_DIM).astype(jnp.float32)
    r, k, v, a, g, kk = hd(r), hd(k), hd(v), hd(a), hd(g), hd(kk)
    kk = kk / jnp.maximum(jnp.sqrt(jnp.sum(kk * kk, axis=-1, keepdims=True)), 1e-12)
    flat = lambda t: t.reshape(B, T, W)
    if T % RWKV_CHUNK == 0:
        ys, S_T = _rwkv_chunked(flat(r), log_decay, flat(k), flat(v), flat(kk), flat(kk * a), S0.astype(jnp.float32))
    else:
        ys, S_T = _rwkv_scan(flat(r), jnp.exp(log_decay), flat(k), flat(v), flat(kk), flat(kk * a),
                             S0.astype(jnp.float32))
    y = _group_norm(ys.reshape(B, T, RWKV_HEADS, HEAD_DIM), lp['rwkv_ln_g'], RWKV_GN_EPS) + lp['rwkv_ln_b']
    y = y + jnp.sum(r * k * lp['rwkv_r_k'], axis=-1, keepdims=True) * v
    y = y * g
    return y.reshape(B, T, W).astype(u.dtype), u[:, -1], S_T


def _retention_group(u, pos, S0, ln_g):
    B, T, _ = u.shape
    uf = u.astype(jnp.float32)
    q, k, v, g = [t.reshape(B, T, RET_HEADS, HEAD_DIM) for t in jnp.split(uf, 4, axis=-1)]
    q = _rotary(q, pos)
    k = _rotary(k, pos) * HEAD_DIM ** -0.5
    lg = jnp.log(1.0 - 2.0 ** (-5.0 - jnp.arange(RET_HEADS, dtype=jnp.float32)))
    C = RET_CHUNK if T % RET_CHUNK == 0 else T
    nC = T // C
    n = jnp.arange(C, dtype=jnp.float32)
    diff = n[:, None] - n[None, :]
    dmask = jnp.where(diff[None] >= 0, jnp.exp(jnp.maximum(diff, 0.0)[None] * lg[:, None, None]), 0.0)
    q_dec = jnp.exp((n[:, None] + 1.0) * lg[None, :])
    k_dec = jnp.exp((C - 1.0 - n)[:, None] * lg[None, :])
    s_dec = jnp.exp(C * lg)

    def chunk(S, inp):
        qc, kc, vc = inp
        att = jnp.einsum('bnhd,bmhd->bhnm', qc, kc) * dmask
        out = jnp.einsum('bhnm,bmhe->bnhe', att, vc) + jnp.einsum('bnhd,bhde->bnhe', qc, S) * q_dec[None, :, :, None]
        S = S * s_dec[None, :, None, None] + jnp.einsum('bmhd,bmhe->bhde', kc * k_dec[None, :, :, None], vc)
        return S, out

    to_chunks = lambda t: jnp.moveaxis(t.reshape(B, nC, C, RET_HEADS, HEAD_DIM), 1, 0)
    S_T, o = lax.scan(chunk, S0.astype(jnp.float32), (to_chunks(q), to_chunks(k), to_chunks(v)))
    o = jnp.moveaxis(o, 0, 1).reshape(B, T, RET_HEADS, HEAD_DIM)
    y = jax.nn.silu(g) * _group_norm(o, ln_g, GN_EPS)
    return y.reshape(B, T, RET_WIDTH).astype(u.dtype), S_T


def _mixing_sublayer(x, lp, past_len, past_cmp, past_sel, win_buf, rwkv_S, rwkv_shift, ret_S):
    B, T, _ = x.shape
    h = _rms_norm(x, lp['norm_attn'])
    P = h @ lp['w_in']
    c = P[..., :NSA_IN]
    q = _rms_norm(c[..., :NSA_WIDTH].reshape(B, T, NSA_HEADS, HEAD_DIM), lp['nsa_q_norm'])
    kv = c[..., NSA_WIDTH:NSA_WIDTH + 6 * NSA_KV_COLS].reshape(B, T, 3, 2, NSA_KV_HEADS, HEAD_DIM)
    kv_cmp, kv_sel, kv_win = kv[:, :, 0], kv[:, :, 1], kv[:, :, 2]
    gates = jax.nn.sigmoid(c[..., NSA_WIDTH + 6 * NSA_KV_COLS:].astype(jnp.float32)).reshape(B, T, NSA_HEADS, 3)
    pos = past_len + jnp.arange(T, dtype=jnp.int32)
    slopes = _alibi_slopes(NSA_HEADS)
    prompt = past_cmp is None
    if prompt:
        rows_cmp, rows_sel, rows_win = kv_cmp, kv_sel, kv_win
        new_win = kv_win[:, T - min(WINDOW, T):]
    else:
        rows_win = jnp.concatenate([win_buf, kv_win.astype(win_buf.dtype)], axis=1)
        new_win = rows_win[:, T:]
    k_norm = lp['nsa_k_norm']
    if prompt:
        kc = _rms_norm(_nsa_compress(rows_cmp[:, :, 0], lp['nsa_cmp_pe'][0], lp['nsa_cmp_w1'][0], lp['nsa_cmp_w2'][0]), k_norm[0])
        vc = _nsa_compress(rows_cmp[:, :, 1], lp['nsa_cmp_pe'][1], lp['nsa_cmp_w1'][1], lp['nsa_cmp_w2'][1])
    else:
        n_cmp = (past_len + T - CMP_LEN) // CMP_STRIDE + 1
        assert (n_cmp - 1) * CMP_STRIDE + CMP_LEN <= past_len
        kcv = _compress_pages(past_cmp[0], past_cmp[1], B, lp['nsa_cmp_pe'], lp['nsa_cmp_w1'], lp['nsa_cmp_w2'], k_norm[0])
        kcv = kcv[:, :n_cmp].reshape(B, n_cmp, 2, NSA_KV_HEADS, HEAD_DIM)
        kc, vc = kcv[:, :, 0], kcv[:, :, 1]
    kw = _rms_norm(rows_win[:, :, 0], k_norm[2])
    vw = rows_win[:, :, 1]
    if prompt:
        o3 = _nsa_prompt_attn(q, kc, vc, _rms_norm(rows_sel[:, :, 0], k_norm[1]), rows_sel[:, :, 1], kw, vw)
    else:
        G, R, d = NSA_KV_HEADS, NSA_GROUP, HEAD_DIM
        rows_sel = jnp.concatenate([past_sel, kv_sel.astype(past_sel.dtype)], axis=1)
        ks_blk = _to_sel_blocks(_rms_norm(rows_sel[:, :, 0], k_norm[1]))
        vs_blk = _to_sel_blocks(rows_sel[:, :, 1])
        n_sel = ks_blk.shape[2]
        sl = slopes.reshape(G, R)

        def sel_branch(qg, idx):
            b_i = jnp.arange(B)[:, None, None, None]
            g_i = jnp.arange(G)[None, :, None, None]
            ks_g = ks_blk[b_i, g_i, idx]
            vs_g = vs_blk[b_i, g_i, idx]
            kpos = idx[..., None] * SEL_LEN + jnp.arange(SEL_LEN, dtype=jnp.int32)
            d_sel = (pos[None, None, :, None, None] - kpos).astype(jnp.float32)[:, :, None]
            s = (jnp.einsum('btgrd,bgtnsd->bgrtns', qg, ks_g).astype(jnp.float32) * (d ** -0.5)
                 - sl[None, :, :, None, None, None] * d_sel)
            mask = jnp.broadcast_to(d_sel >= 0.0, s.shape)
            p_sel = _masked_softmax(s.reshape(B, G, R, T, -1), mask.reshape(B, G, R, T, -1)).reshape(s.shape)
            return jnp.einsum('bgrtns,bgtnsd->btgrd', p_sel.astype(vs_g.dtype), vs_g)

        wb = win_buf.shape[1]
        pos_w = past_len - wb + jnp.arange(wb + T, dtype=jnp.int32)
        o3 = _nsa_branches(q, pos, kc, vc, n_sel, sel_branch, kw, vw, pos_w, slopes)
    o_nsa = jnp.einsum('btha,abthd->bthd', gates.astype(o3.dtype), o3)
    o_nsa = _rms_norm(o_nsa, lp['nsa_out_norm']).reshape(B, T, NSA_WIDTH)
    y_rwkv, new_shift, new_rwkv = _rwkv_group(P[..., NSA_IN:NSA_IN + RWKV_IN], rwkv_shift, rwkv_S, lp)
    y_ret, new_ret = _retention_group(P[..., NSA_IN + RWKV_IN:], pos, ret_S, lp['ret_ln_g'])
    mix = jnp.concatenate([o_nsa, y_rwkv.astype(o_nsa.dtype), y_ret.astype(o_nsa.dtype)], axis=-1)
    x = x + mix @ lp['w_out']
    return x, (kv_cmp, kv_sel, new_win, new_rwkv, new_shift, new_ret)


FFN_ROW_TILE = 1024


def _dense_ffn(x, g, wg, wu, wd):
    shp = x.shape
    x2 = x.reshape(-1, shp[-1])
    tm = _pick_tile(x2.shape[0], FFN_ROW_TILE)
    n_tiles = x2.shape[0] // tm
    ones = jnp.ones((x2.shape[0], 1), jnp.float32)
    y = _ffn(x2, g, ones, wg[None], wu[None], wd[None], jnp.zeros((n_tiles,), jnp.int32),
             jnp.ones((n_tiles,), jnp.int32), tm=tm)
    return (x2 + y).reshape(shp)


def _moe_ffn(xs, g, router, wg, wu, wd):
    D = xs[0].shape[-1]
    E, tm = router.shape[1], FFN_ROW_TILE
    flat = [x.reshape(-1, D) for x in xs]
    x2 = jnp.concatenate(flat, axis=0)
    logits = jnp.concatenate([_router_logits(x, g, router) for x in flat], axis=0)
    top_val, top_idx = lax.top_k(logits, TOP_K)
    gate = jax.nn.softmax(top_val, axis=-1)
    N = x2.shape[0]
    A = N * TOP_K
    flat_e, flat_w = top_idx.reshape(A).astype(jnp.int32), gate.reshape(A)
    order = jnp.argsort(flat_e, stable=True).astype(jnp.int32)
    counts = jnp.sum(jax.nn.one_hot(flat_e, E, dtype=jnp.int32), axis=0)
    padded = (counts + tm - 1) // tm * tm
    start, p_end = jnp.cumsum(counts) - counts, jnp.cumsum(padded)
    p_start = p_end - padded
    sorted_e = flat_e[order]
    dest = p_start[sorted_e] + (jnp.arange(A, dtype=jnp.int32) - start[sorted_e])
    P = -(-(A + E * (tm - 1)) // tm) * tm
    row_tok = jnp.zeros((P,), jnp.int32).at[dest].set(order // TOP_K)
    row_w = jnp.zeros((P,), jnp.float32).at[dest].set(flat_w[order])
    tile_start = jnp.arange(P // tm, dtype=jnp.int32) * tm
    tile_e = jnp.minimum(jnp.searchsorted(p_end, tile_start, side='right'), E - 1).astype(jnp.int32)
    tile_used = (tile_start < p_end[-1]).astype(jnp.int32)
    ys = _ffn(x2[row_tok], g, row_w[:, None], wg, wu, wd, tile_e, tile_used, tm=tm)
    slot = jnp.zeros((A,), jnp.int32).at[order].set(dest).reshape(N, TOP_K)
    y = x2 + ys[slot[:, 0]] + ys[slot[:, 1]]
    outs, off = [], 0
    for x in xs:
        n = x.size // D
        outs.append(y[off:off + n].reshape(x.shape))
        off += n
    return outs


def kernel(x_prompt, x_sample, cache_nsa_cmp, cache_nsa_sel, cache_nsa_win, state_rwkv, state_rwkv_shift,
           state_ret, page_table, norm_attn, norm_ffn, w_in, w_out, nsa_q_norm, nsa_k_norm, nsa_cmp_pe,
           nsa_cmp_w1, nsa_cmp_w2, nsa_out_norm, rwkv_mu, rwkv_w0, rwkv_w_up, rwkv_a0, rwkv_a_up, rwkv_g_up,
           rwkv_k_k, rwkv_k_a, rwkv_r_k, rwkv_ln_g, rwkv_ln_b, ret_ln_g, ffn_w_gate, ffn_w_up, ffn_w_down,
           moe_router, moe_w_gate, moe_w_up, moe_w_down):
    Bp, Bs = x_prompt.shape[0], x_sample.shape[0]
    past_len = page_table.shape[1] * cache_nsa_cmp.shape[2]
    xp, xs = x_prompt, x_sample
    outs_p, outs_s = [], []
    for l in range(DEPTH):
        lp = {'norm_attn': norm_attn[l], 'w_in': w_in[l], 'w_out': w_out[l], 'nsa_q_norm': nsa_q_norm[l],
              'nsa_k_norm': nsa_k_norm[l], 'nsa_cmp_pe': nsa_cmp_pe[l], 'nsa_cmp_w1': nsa_cmp_w1[l],
              'nsa_cmp_w2': nsa_cmp_w2[l], 'nsa_out_norm': nsa_out_norm[l], 'rwkv_mu': rwkv_mu[l],
              'rwkv_w0': rwkv_w0[l], 'rwkv_w_up': rwkv_w_up[l], 'rwkv_a0': rwkv_a0[l], 'rwkv_a_up': rwkv_a_up[l],
              'rwkv_g_up': rwkv_g_up[l], 'rwkv_k_k': rwkv_k_k[l], 'rwkv_k_a': rwkv_k_a[l], 'rwkv_r_k': rwkv_r_k[l],
              'rwkv_ln_g': rwkv_ln_g[l], 'rwkv_ln_b': rwkv_ln_b[l], 'ret_ln_g': ret_ln_g[l]}
        xp, st = _mixing_sublayer(xp, lp, 0, None, None, None,
                                  jnp.zeros((Bp, RWKV_HEADS, HEAD_DIM, HEAD_DIM), jnp.float32),
                                  jnp.zeros((Bp, RWKV_IN), xp.dtype),
                                  jnp.zeros((Bp, RET_HEADS, HEAD_DIM, HEAD_DIM), jnp.float32))
        outs_p.append(st)
        n_pool, page_rows = cache_nsa_sel.shape[1], cache_nsa_sel.shape[2]
        pages = (page_table + l * n_pool).reshape(-1)
        past_cmp = (cache_nsa_cmp.reshape(DEPTH * n_pool, page_rows, -1), pages)
        past_sel = cache_nsa_sel[l][page_table].reshape(Bs, past_len, 2, NSA_KV_HEADS, HEAD_DIM)
        xs, st = _mixing_sublayer(xs, lp, past_len, past_cmp, past_sel, cache_nsa_win[l], state_rwkv[l],
                                  state_rwkv_shift[l], state_ret[l])
        outs_s.append(st)
        i = l // 2
        if l % 2 == 0:
            xp = _dense_ffn(xp, norm_ffn[l], ffn_w_gate[i], ffn_w_up[i], ffn_w_down[i])
            xs = _dense_ffn(xs, norm_ffn[l], ffn_w_gate[i], ffn_w_up[i], ffn_w_down[i])
        else:
            xp, xs = _moe_ffn([xp, xs], norm_ffn[l], moe_router[i], moe_w_gate[i], moe_w_up[i], moe_w_down[i])
    kv_cmp_p, kv_sel_p, win_p, rwkv_p, shift_p, ret_p = [jnp.stack([o[j] for o in outs_p]) for j in range(6)]
    kv_cmp_s, kv_sel_s, win_s, rwkv_s, shift_s, ret_s = [jnp.stack([o[j] for o in outs_s]) for j in range(6)]
    return (xp, xs, kv_cmp_p, kv_sel_p, win_p, rwkv_p, shift_p, ret_p,
            kv_cmp_s, kv_sel_s, win_s, rwkv_s, shift_s, ret_s)
```
